```python
import math
import jax, jax.numpy as jnp
from jax import lax
import numpy as np

D_MODEL = 2048
BATCH = 32
SEQ = 256
DEPTH = 1
DEC_BATCH = 4
DEC_SEQ = 4096
PAST_LEN = 512

GRID_W = 64
D_HYENA = 1024
N_HEADS = 16
HEAD_DIM = 64
D_ATTN = N_HEADS * HEAD_DIM
D_MIX = D_HYENA + D_ATTN
HYENA_ORDER = 2
SHORT_CONV = 3
FILTER_BANDS = 16
FILTER_EMB = 1 + 2 * FILTER_BANDS
FILTER_HIDDEN = 64
DECAY_TARGET = 1e-2
DECAY_PCT_SHORT = 0.3
DECAY_PCT_LONG = 1.5
WIN_H = 8
WIN_W = 16
Q_BLOCK = 128
COL_BLOCK = 16
KEY_COLS = 2 * WIN_W
N_EXPERTS = 32
TOP_K = 4
D_FF = D_MODEL
SWIGLU_ALPHA = 1.702
SWIGLU_LIMIT = 7.0
MOE_BLOCK = 256
EPS = 1e-6
NEG_INF = -1e30

kernel_name = 'hyena_natten_moe_prefix_dit_step'


def rmsnorm(x, g):
    x32 = x.astype(jnp.float32)
    y = x32 * lax.rsqrt(jnp.mean(x32 * x32, axis=-1, keepdims=True) + EPS)
    return (y * g.astype(jnp.float32)).astype(x.dtype)


def modulate(x, g, shift, scale):
    return rmsnorm(x, g) * (1 + scale[:, None]) + shift[:, None]


def short_conv(u, w, b):
    L = u.shape[1]
    up = jnp.pad(u, ((0, 0), (1, 1), (0, 0)))
    return up[:, :L] * w[0] + up[:, 1:L + 1] * w[1] + up[:, 2:L + 2] * w[2] + b


def hyena_filters(L, f_w1, f_b1, f_freq, f_w2, f_b2, f_w3):
    f32 = jnp.float32
    pos = jnp.arange(L, dtype=f32)
    t = pos / max(L - 1, 1)
    w = 2 * math.pi * pos / L
    bands = jnp.linspace(1e-4, FILTER_BANDS - 1, FILTER_BANDS, dtype=f32)
    fw = w[:, None] * bands[None]
    z = jnp.concatenate([t[:, None], jnp.cos(fw), -jnp.sin(fw)], axis=-1)
    freq = f_freq.astype(f32)
    hdn = jnp.sin(freq * (z @ f_w1.astype(f32) + f_b1.astype(f32)))
    hdn = jnp.sin(freq * (hdn @ f_w2.astype(f32) + f_b2.astype(f32)))
    k = (hdn @ f_w3.astype(f32)).reshape(L, HYENA_ORDER, 2, D_HYENA)
    max_decay = math.log(DECAY_TARGET) / DECAY_PCT_SHORT
    min_decay = math.log(DECAY_TARGET) / DECAY_PCT_LONG
    deltas = jnp.linspace(min_decay, max_decay, D_HYENA, dtype=f32)
    decay = jnp.exp(-t[:, None] * jnp.abs(deltas)[None])
    k = k * decay[:, None, None, :]
    return k * lax.rsqrt(jnp.sum(k * k, axis=(0, 2), keepdims=True) + EPS)


def bidir_fftconv(u, h_fwd, h_bwd, bias):
    L = u.shape[1]
    u32 = u.astype(jnp.float32)
    filt = jnp.concatenate([h_fwd, jnp.zeros((1, h_fwd.shape[1]), jnp.float32), h_bwd[:0:-1]], axis=0)
    y = jnp.fft.irfft(jnp.fft.rfft(u32, n=2 * L, axis=1) * jnp.fft.rfft(filt, axis=0)[None], n=2 * L, axis=1)[:, :L]
    return (y + u32 * bias.astype(jnp.float32)).astype(u.dtype)


def hyena_mixer(u, lp):
    L = u.shape[1]
    uc = short_conv(u, lp['conv_w'], lp['conv_b'])
    v, x1, x2 = jnp.split(uc, 3, axis=-1)
    k = hyena_filters(L, lp['f_w1'], lp['f_b1'], lp['f_freq'], lp['f_w2'], lp['f_b2'], lp['f_w3'])
    z = v
    for o, gate in enumerate((x1, x2)):
        z = gate * bidir_fftconv(z, k[:, o, 0], k[:, o, 1], lp['f_bias'][o])
    return z


def context_attention(q, k, v):
    B, H, L, d = q.shape
    nb = L // Q_BLOCK
    qb = q.reshape(B, H, nb, Q_BLOCK, d).transpose(2, 0, 1, 3, 4)
    scale = 1.0 / math.sqrt(d)

    def one_block(qi):
        s = jnp.einsum('bhqd,bhkd->bhqk', qi, k).astype(jnp.float32) * scale
        p = jax.nn.softmax(s, axis=-1).astype(v.dtype)
        return jnp.einsum('bhqk,bhkd->bhqd', p, v)

    o = lax.map(one_block, qb)
    return o.transpose(1, 2, 0, 3, 4).reshape(B, H, L, d)


def neighbourhood_attention(q, k, v, kc, vc, rpb):
    B, H, L, d = q.shape
    R = L // GRID_W
    kh = min(WIN_H, R)
    rb = Q_BLOCK // GRID_W
    nblk = R // rb
    ncb = GRID_W // COL_BLOCK
    scale = 1.0 / math.sqrt(d)
    rows = np.arange(R)
    row_start = np.clip(rows - kh // 2, 0, R - kh)
    row_idx = row_start[:, None] + np.arange(kh)[None]
    dr = row_idx - rows[:, None] + (WIN_H - 1)
    cols = np.arange(GRID_W)
    col_start = np.clip(cols - WIN_W // 2, 0, GRID_W - WIN_W)
    cb_start = np.clip(col_start[::COL_BLOCK], 0, GRID_W - KEY_COLS)
    key_col = cb_start[:, None] + np.arange(KEY_COLS)[None]
    q_col = cols.reshape(ncb, COL_BLOCK)
    kcol = key_col[:, None, :]
    cs = col_start[q_col][:, :, None]
    col_valid = (kcol >= cs) & (kcol < cs + WIN_W)
    dc = np.clip(kcol - q_col[:, :, None], -(WIN_W - 1), WIN_W - 1) + (WIN_W - 1)
    kg = k.reshape(B, H, R, GRID_W, d)
    vg = v.reshape(B, H, R, GRID_W, d)
    qg = q.reshape(B, H, nblk, rb, ncb, COL_BLOCK, d).transpose(2, 0, 1, 3, 4, 5, 6)
    n_loc = kh * KEY_COLS

    def one_block(args):
        qi, ri, dri = args
        kb = kg[:, :, ri[:, None, :, None], key_col[None, :, None, :]]
        vb = vg[:, :, ri[:, None, :, None], key_col[None, :, None, :]]
        bias = rpb[:, dri[:, None, None, :, None], dc[None, :, :, None, :]]
        s = jnp.einsum('bhrcqd,bhrckwd->bhrcqkw', qi, kb).astype(jnp.float32) * scale + bias.astype(jnp.float32)[None]
        s = jnp.where(col_valid[:, :, None, :], s, NEG_INF).reshape(B, H, rb, ncb, COL_BLOCK, n_loc)
        sc = jnp.einsum('bhrcqd,bhnd->bhrcqn', qi, kc).astype(jnp.float32) * scale
        p = jax.nn.softmax(jnp.concatenate([s, sc], axis=-1), axis=-1).astype(v.dtype)
        p_loc = p[..., :n_loc].reshape(B, H, rb, ncb, COL_BLOCK, kh, KEY_COLS)
        p_ctx = p[..., n_loc:]
        return (jnp.einsum('bhrcqkw,bhrckwd->bhrcqd', p_loc, vb)
                + jnp.einsum('bhrcqn,bhnd->bhrcqd', p_ctx, vc))

    o = lax.map(one_block, (qg, jnp.asarray(row_idx.reshape(nblk, rb, kh)), jnp.asarray(dr.reshape(nblk, rb, kh))))
    return o.transpose(1, 2, 0, 3, 4, 5, 6).reshape(B, H, L, d)


def expert_ffn(xb, w1, b1, w2, b2):
    hg = xb @ w1 + b1
    gate = jnp.minimum(hg[..., :D_FF], SWIGLU_LIMIT)
    up = jnp.clip(hg[..., D_FF:], -SWIGLU_LIMIT, SWIGLU_LIMIT)
    glu = gate * jax.nn.sigmoid(SWIGLU_ALPHA * gate)
    return ((up + 1) * glu) @ w2 + b2


def moe_ffn(h, w_router, b_router, w1, b1, w2, b2):
    T, D = h.shape
    logits = (h @ w_router).astype(jnp.float32) + b_router.astype(jnp.float32)
    top_v, top_i = lax.top_k(logits, TOP_K)
    gates = jax.nn.softmax(top_v, axis=-1)
    n_pairs = T * TOP_K
    flat_e = top_i.reshape(n_pairs)
    order = jnp.argsort(flat_e)
    sorted_e = flat_e[order]
    sorted_tok = (order // TOP_K).astype(jnp.int32)
    sorted_g = gates.reshape(n_pairs)[order]
    counts = jnp.bincount(flat_e, length=N_EXPERTS)
    padded = (counts + MOE_BLOCK - 1) // MOE_BLOCK * MOE_BLOCK
    pad_end = jnp.cumsum(padded)
    pad_start = pad_end - padded
    grp_start = jnp.cumsum(counts) - counts
    slot = pad_start[sorted_e] + (jnp.arange(n_pairs) - grp_start[sorted_e])
    n_blocks = -(-n_pairs // MOE_BLOCK) + N_EXPERTS
    slot_tok = jnp.zeros((n_blocks * MOE_BLOCK,), jnp.int32).at[slot].set(sorted_tok)
    block_e = jnp.minimum(jnp.searchsorted(pad_end, jnp.arange(n_blocks) * MOE_BLOCK, side='right'), N_EXPERTS - 1)

    def run_block(args):
        tok, e = args
        return expert_ffn(h[tok], w1[e], b1[e], w2[e], b2[e])

    out = lax.map(run_block, (slot_tok.reshape(n_blocks, MOE_BLOCK), block_e))
    out = out.reshape(n_blocks * MOE_BLOCK, D)[slot]
    return jnp.zeros_like(h).at[sorted_tok].add(sorted_g[:, None].astype(h.dtype) * out)


def trunk_layer(x, mod, lp, kc=None, vc=None):
    B, L, D = x.shape
    shift1, scale1, gate1, shift2, scale2, gate2 = jnp.split(mod, 6, axis=-1)
    h = modulate(x, lp['g_mix'], shift1, scale1)
    proj = h @ lp['w_in']
    y_hy = hyena_mixer(proj[..., :3 * D_HYENA], lp)
    q, k, v = [t.reshape(B, L, N_HEADS, HEAD_DIM).transpose(0, 2, 1, 3)
               for t in jnp.split(proj[..., 3 * D_HYENA:], 3, axis=-1)]
    if kc is None:
        y_at = context_attention(q, k, v)
    else:
        y_at = neighbourhood_attention(q, k, v, kc, vc, lp['rpb'])
    y_at = y_at.transpose(0, 2, 1, 3).reshape(B, L, D_ATTN)
    mix = jnp.concatenate([rmsnorm(y_hy, lp['g_out_hy']), rmsnorm(y_at, lp['g_out_at'])], axis=-1) @ lp['w_out']
    x = x + gate1[:, None] * mix
    h2 = modulate(x, lp['g_ffn'], shift2, scale2)
    ff = moe_ffn(h2.reshape(B * L, D), lp['w_router'], lp['b_router'], lp['w_exp1'], lp['b_exp1'],
                 lp['w_exp2'], lp['b_exp2']).reshape(B, L, D)
    x = x + gate2[:, None] * ff
    return x, k, v


def setup_inputs(seed: int = 0) -> dict:
    key = jax.random.key(seed)
    ks = jax.random.split(key, 32)

    def nrm(k, shape, scale):
        return jax.random.normal(k, shape, jnp.float32) * scale

    return {
        'x_prompt': nrm(ks[0], (BATCH, SEQ, D_MODEL), 1.0),
        'x_sample': nrm(ks[1], (DEC_BATCH, DEC_SEQ, D_MODEL), 1.0),
        'cache_k': nrm(ks[2], (DEC_BATCH, DEPTH, N_HEADS, PAST_LEN, HEAD_DIM), 1.0),
        'cache_v': nrm(ks[3], (DEC_BATCH, DEPTH, N_HEADS, PAST_LEN, HEAD_DIM), 1.0),
        'c': nrm(ks[4], (DEC_BATCH, D_MODEL), 1.0),
        'c_ctx': nrm(ks[5], (D_MODEL,), 1.0),
        'w_mod': nrm(ks[6], (DEPTH, D_MODEL, 6 * D_MODEL), D_MODEL ** -0.5),
        'b_mod': nrm(ks[7], (DEPTH, 6 * D_MODEL), 0.02),
        'g_mix': 1.0 + nrm(ks[8], (DEPTH, D_MODEL), 0.05),
        'w_in': nrm(ks[9], (DEPTH, D_MODEL, 3 * D_HYENA + 3 * D_ATTN), D_MODEL ** -0.5),
        'conv_w': nrm(ks[10], (DEPTH, SHORT_CONV, 3 * D_HYENA), SHORT_CONV ** -0.5),
        'conv_b': nrm(ks[11], (DEPTH, 3 * D_HYENA), 0.02),
        'f_w1': nrm(ks[12], (DEPTH, FILTER_EMB, FILTER_HIDDEN), FILTER_EMB ** -0.5),
        'f_b1': nrm(ks[13], (DEPTH, FILTER_HIDDEN), 0.02),
        'f_freq': 1.0 + nrm(ks[14], (DEPTH, FILTER_HIDDEN), 0.1),
        'f_w2': nrm(ks[15], (DEPTH, FILTER_HIDDEN, FILTER_HIDDEN), FILTER_HIDDEN ** -0.5),
        'f_b2': nrm(ks[16], (DEPTH, FILTER_HIDDEN), 0.02),
        'f_w3': nrm(ks[17], (DEPTH, FILTER_HIDDEN, HYENA_ORDER * 2 * D_HYENA), FILTER_HIDDEN ** -0.5),
        'f_bias': nrm(ks[18], (DEPTH, HYENA_ORDER, D_HYENA), 0.1),
        'rpb': nrm(ks[19], (DEPTH, N_HEADS, 2 * WIN_H - 1, 2 * WIN_W - 1), 0.1),
        'g_out_hy': 1.0 + nrm(ks[20], (DEPTH, D_HYENA), 0.05),
        'g_out_at': 1.0 + nrm(ks[21], (DEPTH, D_ATTN), 0.05),
        'w_out': nrm(ks[22], (DEPTH, D_MIX, D_MODEL), D_MIX ** -0.5),
        'g_ffn': 1.0 + nrm(ks[23], (DEPTH, D_MODEL), 0.05),
        'w_router': nrm(ks[24], (DEPTH, D_MODEL, N_EXPERTS), D_MODEL ** -0.5),
        'b_router': nrm(ks[25], (DEPTH, N_EXPERTS), 0.01),
        'w_exp1': nrm(ks[26], (DEPTH, N_EXPERTS, D_MODEL, 2 * D_FF), D_MODEL ** -0.5),
        'b_exp1': nrm(ks[27], (DEPTH, N_EXPERTS, 2 * D_FF), 0.02),
        'w_exp2': nrm(ks[28], (DEPTH, N_EXPERTS, D_FF, D_MODEL), D_FF ** -0.5),
        'b_exp2': nrm(ks[29], (DEPTH, N_EXPERTS, D_MODEL), 0.02),
        'g_final': 1.0 + nrm(ks[30], (D_MODEL,), 0.05),
    }


def reference(x_prompt, x_sample, cache_k, cache_v, c, c_ctx, w_mod, b_mod, g_mix, w_in, conv_w, conv_b,
              f_w1, f_b1, f_freq, f_w2, f_b2, f_w3, f_bias, rpb, g_out_hy, g_out_at, w_out, g_ffn,
              w_router, b_router, w_exp1, b_exp1, w_exp2, b_exp2, g_final):
    xp = x_prompt
    xs = x_sample
    new_k = []
    new_v = []
    for l in range(DEPTH):
        lp = {
            'g_mix': g_mix[l], 'w_in': w_in[l], 'conv_w': conv_w[l], 'conv_b': conv_b[l],
            'f_w1': f_w1[l], 'f_b1': f_b1[l], 'f_freq': f_freq[l], 'f_w2': f_w2[l], 'f_b2': f_b2[l],
            'f_w3': f_w3[l], 'f_bias': f_bias[l], 'rpb': rpb[l], 'g_out_hy': g_out_hy[l],
            'g_out_at': g_out_at[l], 'w_out': w_out[l], 'g_ffn': g_ffn[l], 'w_router': w_router[l],
            'b_router': b_router[l], 'w_exp1': w_exp1[l], 'b_exp1': b_exp1[l], 'w_exp2': w_exp2[l],
            'b_exp2': b_exp2[l],
        }
        mod_ctx = (jax.nn.silu(c_ctx) @ w_mod[l] + b_mod[l])[None]
        mod_lat = jax.nn.silu(c) @ w_mod[l] + b_mod[l]
        xp, k_l, v_l = trunk_layer(xp, mod_ctx, lp)
        new_k.append(k_l)
        new_v.append(v_l)
        xs, _, _ = trunk_layer(xs, mod_lat, lp, cache_k[:, l], cache_v[:, l])
    y_prompt = rmsnorm(xp, g_final)
    y_sample = rmsnorm(xs, g_final)
    state_k = jnp.stack(new_k, axis=1)
    state_v = jnp.stack(new_v, axis=1)
    return (y_prompt, y_sample, state_k, state_v)
```

```python
import functools
import math

import numpy as np
import jax
import jax.numpy as jnp
from jax import lax
from jax.experimental import pallas as pl
from jax.experimental.pallas import tpu as pltpu

F32 = jnp.float32
BF16 = jnp.bfloat16
HIGHEST = lax.Precision.HIGHEST

GRID_W = 64
N_HEADS = 16
HEAD_DIM = 64
D_HYENA = 1024
HYENA_ORDER = 2
FILTER_BANDS = 16
DECAY_TARGET = 1e-2
DECAY_PCT_SHORT = 0.3
DECAY_PCT_LONG = 1.5
WIN_H = 8
WIN_W = 16
N_EXPERTS = 32
TOP_K = 4
SWIGLU_ALPHA = 1.702
SWIGLU_LIMIT = 7.0
EPS = 1e-6
NEG_INF = -1e30

LANES = 128
SUBLANES = 8
VMEM_LIMIT = 56 * 1024 * 1024

ROW_TILE = 256
NA_Q_ROWS = 4
MOE_CHUNK = 1024
MOE_SUB = 256
MOE_FF_TILE = 256


def _params(*sem):
    return pltpu.CompilerParams(dimension_semantics=sem, vmem_limit_bytes=VMEM_LIMIT)


def _rms(x, g):
    return x * lax.rsqrt(jnp.mean(x * x, axis=-1, keepdims=True) + EPS) * g


def _split_bf16(x):
    hi = x.astype(BF16)
    lo = (x - hi.astype(F32)).astype(BF16)
    return hi, lo


def _dot(a, b):
    return jnp.dot(a, b, preferred_element_type=F32)


def _dot_nt(a, b):
    return lax.dot_general(a, b, (((1,), (1,)), ((), ())), preferred_element_type=F32)


def _mod_body(c_ref, w_ref, b_ref, o_ref):
    c = c_ref[...]
    s = c / (1.0 + jnp.exp(-c))
    o_ref[...] = jnp.dot(s, w_ref[...], precision=HIGHEST, preferred_element_type=F32) + b_ref[...]


def _modulation(cc, w, b):
    D, N = w.shape
    tn = min(N, 1536)
    return pl.pallas_call(
        _mod_body,
        grid=(N // tn,),
        in_specs=[pl.BlockSpec((SUBLANES, D), lambda j: (0, 0)),
                  pl.BlockSpec((D, tn), lambda j: (0, j)),
                  pl.BlockSpec((1, tn), lambda j: (0, j))],
        out_specs=pl.BlockSpec((SUBLANES, tn), lambda j: (0, j)),
        out_shape=jax.ShapeDtypeStruct((SUBLANES, N), F32),
        compiler_params=_params("arbitrary"),
    )(cc, w, b)


def _inproj_body(x_ref, shift_ref, scale_ref, g_ref, w_ref, o_ref, *, n_chunk):
    h = _rms(x_ref[...], g_ref[...]) * (1.0 + scale_ref[...]) + shift_ref[...]
    hb = h.astype(BF16)
    n_out = o_ref.shape[-1]

    def col(j, carry):
        c0 = pl.multiple_of(j * n_chunk, n_chunk)
        o_ref[:, pl.ds(c0, n_chunk)] = _dot(hb, w_ref[:, pl.ds(c0, n_chunk)])
        return carry

    lax.fori_loop(0, n_out // n_chunk, col, 0)


def _in_projection(x, mod3, row0, g, w_bf16):
    B, L, D = x.shape
    N = w_bf16.shape[1]
    tm = min(ROW_TILE, L)
    mrow = (lambda b: 0) if row0 == 0 else (lambda b: b + row0)
    return pl.pallas_call(
        functools.partial(_inproj_body, n_chunk=512),
        grid=(B, L // tm),
        in_specs=[pl.BlockSpec((None, tm, D), lambda b, i: (b, i, 0)),
                  pl.BlockSpec((None, 1, D), lambda b, i: (mrow(b), 0, 0)),
                  pl.BlockSpec((None, 1, D), lambda b, i: (mrow(b), 0, 1)),
                  pl.BlockSpec((1, D), lambda b, i: (0, 0)),
                  pl.BlockSpec((D, N), lambda b, i: (0, 0), pipeline_mode=pl.Buffered(1))],
        out_specs=pl.BlockSpec((None, tm, N), lambda b, i: (b, i, 0)),
        out_shape=jax.ShapeDtypeStruct((B, L, N), F32),
        compiler_params=_params("arbitrary", "arbitrary"),
    )(x, mod3, mod3, g, w_bf16)


def _head_lane_masks(rows, hd):
    lane = lax.broadcasted_iota(jnp.int32, (rows, LANES), 1)
    return [(lane >= h * hd) & (lane < (h + 1) * hd) for h in range(LANES // hd)]


def _ctx_attn_body(q_ref, k_ref, v_ref, o_ref, *, scale, hd):
    q = q_ref[...]
    k = k_ref[...].astype(BF16)
    v = v_ref[...].astype(BF16)
    masks = _head_lane_masks(q.shape[0], hd)
    out = None
    for msk in masks:
        qh = jnp.where(msk, q, 0.0).astype(BF16)
        s = _dot_nt(qh, k) * scale
        p = jnp.exp(s - jnp.max(s, axis=-1, keepdims=True))
        o = _dot(p.astype(BF16), v) / jnp.sum(p, axis=-1, keepdims=True)
        out = o if out is None else jnp.where(msk, o, out)
    o_ref[...] = out


def _context_attention(proj, q_off):
    B, L, _ = proj.shape
    d_attn = N_HEADS * HEAD_DIM
    nhp = d_attn // LANES
    qb, kb, vb = (q_off // LANES, (q_off + d_attn) // LANES, (q_off + 2 * d_attn) // LANES)
    spec = lambda base: pl.BlockSpec((None, L, LANES), lambda b, h: (b, 0, base + h))
    return pl.pallas_call(
        functools.partial(_ctx_attn_body, scale=1.0 / math.sqrt(HEAD_DIM), hd=HEAD_DIM),
        grid=(B, nhp),
        in_specs=[spec(qb), spec(kb), spec(vb)],
        out_specs=pl.BlockSpec((None, L, LANES), lambda b, h: (b, 0, h)),
        out_shape=jax.ShapeDtypeStruct((B, L, d_attn), F32),
        compiler_params=_params("arbitrary", "arbitrary"),
    )(proj, proj, proj)


def _na_tables(rows, rpb):
    W = GRID_W
    kh = min(WIN_H, rows)
    rbq = NA_Q_ROWS
    kwr = rbq + kh
    assert rows % rbq == 0 and kwr <= rows
    nblk = rows // rbq
    ws = np.clip(np.arange(nblk) * rbq - kh // 2, 0, rows - kwr)
    rq = np.arange(nblk)[:, None, None] * rbq + np.arange(rbq)[None, :, None]
    rk = ws[:, None, None] + np.arange(kwr)[None, None, :]
    rs = np.clip(rq - kh // 2, 0, rows - kh)
    vr = (rk >= rs) & (rk < rs + kh)
    dr = np.where(vr, rk - rq + WIN_H - 1, 0)
    patterns = np.concatenate([dr.reshape(nblk, -1), vr.reshape(nblk, -1)], axis=1)
    _, first, btype = np.unique(patterns, axis=0, return_index=True, return_inverse=True)
    dr_t, vr_t = dr[first], vr[first]
    cq = np.arange(W)[:, None]
    ck = np.arange(W)[None, :]
    cs = np.clip(cq - WIN_W // 2, 0, W - WIN_W)
    vc = (ck >= cs) & (ck < cs + WIN_W)
    dc = np.clip(ck - cq, -(WIN_W - 1), WIN_W - 1) + WIN_W - 1
    vals = rpb[:, dr_t[:, :, None, :, None], dc[None, None, :, None, :]]
    valid = vr_t[:, :, None, :, None] & vc[None, None, :, None, :]
    bias = jnp.where(valid[None], vals.astype(F32), NEG_INF)
    nt = dr_t.shape[0]
    bias = bias.reshape(rpb.shape[0], nt, rbq * W, kwr * W)
    return ws.astype(np.int32), btype.reshape(-1).astype(np.int32), bias


def _na_body(ws_ref, bt_ref, q_ref, k_ref, v_ref, kc_ref, vc_ref, bias_ref, o_ref, *, scale, hd, nk):
    i = pl.program_id(2)
    start = pl.multiple_of(ws_ref[i] * GRID_W, GRID_W)
    q = q_ref[...]
    kl = k_ref[pl.ds(start, nk), :].astype(BF16)
    vl = v_ref[pl.ds(start, nk), :].astype(BF16)
    kc = kc_ref[...].astype(BF16)
    vc = vc_ref[...].astype(BF16)
    masks = _head_lane_masks(q.shape[0], hd)
    out = None
    for h, msk in enumerate(masks):
        qh = jnp.where(msk, q, 0.0).astype(BF16)
        sl = _dot_nt(qh, kl) * scale + bias_ref[h]
        sc = _dot_nt(qh, kc) * scale
        m = jnp.maximum(jnp.max(sl, axis=-1, keepdims=True), jnp.max(sc, axis=-1, keepdims=True))
        p_l = jnp.exp(sl - m)
        p_c = jnp.exp(sc - m)
        den = jnp.sum(p_l, axis=-1, keepdims=True) + jnp.sum(p_c, axis=-1, keepdims=True)
        o = (_dot(p_l.astype(BF16), vl) + _dot(p_c.astype(BF16), vc)) / den
        out = o if out is None else jnp.where(msk, o, out)
    o_ref[...] = out


def _neighbourhood_attention(proj, q_off, kc, vc, rpb):
    B, L, _ = proj.shape
    Lc = kc.shape[1]
    d_attn = N_HEADS * HEAD_DIM
    hpb = LANES // HEAD_DIM
    nhp = d_attn // LANES
    rows = L // GRID_W
    ws, btype, bias = _na_tables(rows, rpb)
    nq = NA_Q_ROWS * GRID_W
    nk = (NA_Q_ROWS + min(WIN_H, rows)) * GRID_W
    nblk = rows // NA_Q_ROWS
    qb, kb, vb = (q_off // LANES, (q_off + d_attn) // LANES, (q_off + 2 * d_attn) // LANES)
    full = lambda base: pl.BlockSpec((None, L, LANES), lambda b, h, i, ws_r, bt_r: (b, 0, base + h))
    ctx = pl.BlockSpec((None, Lc, LANES), lambda b, h, i, ws_r, bt_r: (b, 0, h))
    grid_spec = pltpu.PrefetchScalarGridSpec(
        num_scalar_prefetch=2,
        grid=(B, nhp, nblk),
        in_specs=[pl.BlockSpec((None, nq, LANES), lambda b, h, i, ws_r, bt_r: (b, i, qb + h)),
                  full(kb), full(vb), ctx, ctx,
                  pl.BlockSpec((hpb, None, nq, nk), lambda b, h, i, ws_r, bt_r: (h, bt_r[i], 0, 0))],
        out_specs=pl.BlockSpec((None, nq, LANES), lambda b, h, i, ws_r, bt_r: (b, i, h)),
    )
    return pl.pallas_call(
        functools.partial(_na_body, scale=1.0 / math.sqrt(HEAD_DIM), hd=HEAD_DIM, nk=nk),
        grid_spec=grid_spec,
        out_shape=jax.ShapeDtypeStruct((B, L, d_attn), F32),
        compiler_params=_params("arbitrary", "arbitrary", "arbitrary"),
    )(jnp.asarray(ws), jnp.asarray(btype), proj, proj, proj, kc, vc, bias)


def _outproj_body(yhp_ref, yap_ref, xp_ref, yhs_ref, yas_ref, xs_ref, gate1_ref, shift2_ref, scale2_ref, ghy_ref,
                  gat_ref, w_ref, gffn_ref, wrh_ref, wrl_ref, br_ref, x1_ref, h2u_ref, lg_ref, *, n_ctx_tiles):
    def run(yh_ref, ya_ref, x_ref):
        cat = jnp.concatenate([_rms(yh_ref[...], ghy_ref[...]), _rms(ya_ref[...], gat_ref[...])], axis=-1)
        mix = _dot(cat.astype(BF16), w_ref[...])
        x1 = x_ref[...] + gate1_ref[...] * mix
        x1_ref[...] = x1
        h2 = _rms(x1, gffn_ref[...]) * (1.0 + scale2_ref[...]) + shift2_ref[...]
        hi, lo = _split_bf16(h2)
        lg_ref[...] = (_dot(hi, wrh_ref[...]) + _dot(lo, wrh_ref[...]) + _dot(hi, wrl_ref[...])) + br_ref[...]
        bits = lax.bitcast_convert_type(hi.astype(F32), jnp.uint32)
        half = bits.shape[-1] // 2
        h2u_ref[...] = (bits[:, :half] >> 16) | (bits[:, half:] & jnp.uint32(0xFFFF0000))

    is_ctx = pl.program_id(0) < n_ctx_tiles
    pl.when(is_ctx)(lambda: run(yhp_ref, yap_ref, xp_ref))
    pl.when(jnp.logical_not(is_ctx))(lambda: run(yhs_ref, yas_ref, xs_ref))


def _out_projection(ctx, lat, mod3, lp):
    Bp, Lp, D = ctx[2].shape
    Bs, Ls, _ = lat[2].shape
    dh, da = ctx[0].shape[-1], ctx[1].shape[-1]
    tm = min(ROW_TILE, Lp, Ls)
    nct, nlt = Bp * Lp // tm, Bs * Ls // tm
    n_tok = (nct + nlt) * tm
    lat_tiles = Ls // tm
    mrow = lambda i: jnp.where(i < nct, 0, 1 + (i - nct) // lat_tiles)
    modspec = lambda c: pl.BlockSpec((None, 1, D), lambda i: (mrow(i), 0, c))
    const = lambda shape: pl.BlockSpec(shape, lambda i: (0,) * len(shape))
    crow = lambda w: pl.BlockSpec((tm, w), lambda i: (jnp.minimum(i, nct - 1), 0))
    lrow = lambda w: pl.BlockSpec((tm, w), lambda i: (jnp.maximum(i - nct, 0), 0))
    flat = lambda t: t.reshape(-1, t.shape[-1])
    return pl.pallas_call(
        functools.partial(_outproj_body, n_ctx_tiles=nct),
        grid=(nct + nlt,),
        in_specs=[crow(dh), crow(da), crow(D), lrow(dh), lrow(da), lrow(D),
                  modspec(2), modspec(3), modspec(4),
                  const((1, dh)), const((1, da)),
                  pl.BlockSpec((dh + da, D), lambda i: (0, 0), pipeline_mode=pl.Buffered(1)),
                  const((1, D)), const((D, LANES)), const((D, LANES)), const((1, LANES))],
        out_specs=[pl.BlockSpec((tm, D), lambda i: (i, 0)), pl.BlockSpec((tm, D // 2), lambda i: (i, 0)),
                   pl.BlockSpec((tm, LANES), lambda i: (i, 0))],
        out_shape=[jax.ShapeDtypeStruct((n_tok, D), F32),
                   jax.ShapeDtypeStruct((n_tok, D // 2), jnp.uint32),
                   jax.ShapeDtypeStruct((n_tok, LANES), F32)],
        compiler_params=_params("arbitrary"),
    )(*[flat(t) for t in ctx], *[flat(t) for t in lat], mod3, mod3, mod3, lp['g_out_hy'], lp['g_out_at'],
      lp['w_out'], lp['g_ffn'], lp['wr_hi'], lp['wr_lo'], lp['b_router'])


def _expert_body(ce_ref, nv_ref, nu_ref, gidx_ref, sidx_ref, h2u_hbm, w1g_ref, w1u_ref, b1g_ref, b1u_ref,
                 w2_ref, b2_ref, yp_hbm, xu, xb, acc, w1g_b, w1u_b, w2_b, gsem, ssem):
    c = pl.program_id(0)
    f = pl.program_id(1)
    nf = pl.num_programs(1)
    nsub = (nv_ref[c] + MOE_SUB - 1) // MOE_SUB
    nrows = nsub * MOE_SUB

    def rows_copy(src, dst, sem, sb):
        r0 = pl.multiple_of(sb * MOE_SUB, MOE_SUB)
        return pltpu.make_async_copy(src.at[pl.ds(0, MOE_SUB), :], dst.at[pl.ds(r0, MOE_SUB), :], sem)

    @pl.when(c < nu_ref[0])
    def _chunk():
        @pl.when(f == 0)
        def _gather():
            def issue(i, carry):
                tok = gidx_ref[i // LANES, i % LANES]
                pltpu.make_async_copy(h2u_hbm.at[pl.ds(tok, 1), :], xu.at[pl.ds(i, 1), :], gsem).start()
                return carry

            lax.fori_loop(0, nrows, issue, 0)

            def wait(sb, carry):
                rows_copy(h2u_hbm, xu, gsem, sb).wait()
                return carry

            lax.fori_loop(0, nsub, wait, 0)

            def unpack(sb, carry):
                r0 = pl.multiple_of(sb * MOE_SUB, MOE_SUB)
                u = xu[pl.ds(r0, MOE_SUB), :]
                lo = lax.bitcast_convert_type(u << 16, F32)
                hi = lax.bitcast_convert_type(u & jnp.uint32(0xFFFF0000), F32)
                xb[pl.ds(r0, MOE_SUB), :] = jnp.concatenate([lo, hi], axis=-1).astype(BF16)
                acc[pl.ds(r0, MOE_SUB), :] = jnp.zeros((MOE_SUB, acc.shape[1]), F32)
                return carry

            lax.fori_loop(0, nsub, unpack, 0)

        w1g_b[...] = w1g_ref[...].astype(BF16)
        w1u_b[...] = w1u_ref[...].astype(BF16)
        w2_b[...] = w2_ref[...].astype(BF16)

        def sub(sb, carry):
            r0 = pl.multiple_of(sb * MOE_SUB, MOE_SUB)
            x = xb[pl.ds(r0, MOE_SUB), :]
            gate = jnp.minimum(_dot(x, w1g_b[...]) + b1g_ref[...], SWIGLU_LIMIT)
            up = jnp.clip(_dot(x, w1u_b[...]) + b1u_ref[...], -SWIGLU_LIMIT, SWIGLU_LIMIT)
            glu = gate / (1.0 + jnp.exp(-SWIGLU_ALPHA * gate))
            a = ((up + 1.0) * glu).astype(BF16)
            acc[pl.ds(r0, MOE_SUB), :] += _dot(a, w2_b[...])
            return carry

        lax.fori_loop(0, nsub, sub, 0)

        @pl.when(f == nf - 1)
        def _scatter():
            def bias(sb, carry):
                r0 = pl.multiple_of(sb * MOE_SUB, MOE_SUB)
                acc[pl.ds(r0, MOE_SUB), :] += b2_ref[...]
                return carry

            lax.fori_loop(0, nsub, bias, 0)

            nv = nv_ref[c]

            def issue(i, carry):
                dst = sidx_ref[i // LANES, i % LANES]
                pltpu.make_async_copy(acc.at[pl.ds(i, 1), :], yp_hbm.at[pl.ds(dst, 1), :], ssem).start()
                return carry

            lax.fori_loop(0, nv, issue, 0)

            def wait(sb, carry):
                rows_copy(yp_hbm, acc, ssem, sb).wait()
                return carry

            lax.fori_loop(0, nv // MOE_SUB, wait, 0)
            bit = MOE_SUB // 2
            while bit:
                def wait_piece(n=bit):
                    pltpu.make_async_copy(yp_hbm.at[pl.ds(0, n), :], acc.at[pl.ds(0, n), :], ssem).wait()

                pl.when((nv & bit) != 0)(wait_piece)
                bit //= 2


def _experts(chunk_e, chunk_nv, n_used, gidx, sidx, h2u, lp, n_out_rows):
    E, D, two_ff = lp['w_exp1'].shape
    d_ff = two_ff // 2
    nc = gidx.shape[0]
    R, tf = MOE_CHUNK, MOE_FF_TILE
    nf = d_ff // tf

    def ff(c, f, nu):
        return jnp.where(c < nu[0], f, nf - 1)

    smem = lambda: pl.BlockSpec((None, R // LANES, LANES), lambda c, f, ce, nv, nu: (c, 0, 0),
                                memory_space=pltpu.SMEM)
    grid_spec = pltpu.PrefetchScalarGridSpec(
        num_scalar_prefetch=3,
        grid=(nc, nf),
        in_specs=[smem(), smem(),
                  pl.BlockSpec(memory_space=pl.ANY),
                  pl.BlockSpec((None, D, tf), lambda c, f, ce, nv, nu: (ce[c], 0, ff(c, f, nu))),
                  pl.BlockSpec((None, D, tf), lambda c, f, ce, nv, nu: (ce[c], 0, nf + ff(c, f, nu))),
                  pl.BlockSpec((None, 1, tf), lambda c, f, ce, nv, nu: (ce[c], 0, ff(c, f, nu))),
                  pl.BlockSpec((None, 1, tf), lambda c, f, ce, nv, nu: (ce[c], 0, nf + ff(c, f, nu))),
                  pl.BlockSpec((None, tf, D), lambda c, f, ce, nv, nu: (ce[c], ff(c, f, nu), 0)),
                  pl.BlockSpec((None, 1, D), lambda c, f, ce, nv, nu: (ce[c], 0, 0))],
        out_specs=pl.BlockSpec(memory_space=pl.ANY),
        scratch_shapes=[pltpu.VMEM((R, D // 2), jnp.uint32),
                        pltpu.VMEM((R, D), BF16),
                        pltpu.VMEM((R, D), F32),
                        pltpu.VMEM((D, tf), BF16),
                        pltpu.VMEM((D, tf), BF16),
                        pltpu.VMEM((tf, D), BF16),
                        pltpu.SemaphoreType.DMA(()),
                        pltpu.SemaphoreType.DMA(())],
    )
    return pl.pallas_call(
        _expert_body,
        grid_spec=grid_spec,
        out_shape=jax.ShapeDtypeStruct((n_out_rows, D), F32),
        compiler_params=pltpu.CompilerParams(dimension_semantics=("arbitrary", "arbitrary"),
                                             vmem_limit_bytes=VMEM_LIMIT, has_side_effects=True),
    )(chunk_e, chunk_nv, n_used, gidx, sidx, h2u, lp['w_exp1'], lp['w_exp1'], lp['b_exp1'], lp['b_exp1'],
      lp['w_exp2'], lp['b_exp2'])


def _routing(logits, n_tok):
    R = MOE_CHUNK
    top_v, top_i = lax.top_k(logits, TOP_K)
    gates = jax.nn.softmax(top_v, axis=-1)
    n_pairs = n_tok * TOP_K
    flat_e = top_i.reshape(n_pairs).astype(jnp.int32)
    onehot = (flat_e[:, None] == jnp.arange(N_EXPERTS, dtype=jnp.int32)[None]).astype(jnp.int32)
    rank = jnp.take_along_axis(jnp.cumsum(onehot, axis=0), flat_e[:, None], axis=1)[:, 0] - 1
    counts = jnp.sum(onehot, axis=0)
    padded = (counts + R - 1) // R * R
    pad_end = jnp.cumsum(padded)
    pad_start = pad_end - padded
    slot = pad_start[flat_e] + rank
    nc = n_pairs // R + N_EXPERTS
    pair = jnp.arange(n_pairs, dtype=jnp.int32)
    tok = pair // TOP_K
    dest = (pair % TOP_K) * n_tok + tok
    gidx = jnp.zeros((nc * R,), jnp.int32).at[slot].set(tok)
    sidx = jnp.zeros((nc * R,), jnp.int32).at[slot].set(dest)
    n_used = (pad_end[-1] // R).astype(jnp.int32)
    cstart = jnp.arange(nc, dtype=jnp.int32) * R
    ce = jnp.minimum(jnp.searchsorted(pad_end, cstart, side='right'), N_EXPERTS - 1).astype(jnp.int32)
    ce = jnp.where(jnp.arange(nc) < n_used, ce, ce[jnp.maximum(n_used - 1, 0)])
    nv = jnp.clip(counts[ce] - (cstart - pad_start[ce]), 0, R).astype(jnp.int32)
    nv = jnp.where(jnp.arange(nc) < n_used, nv, 0)
    shape3 = (nc, R // LANES, LANES)
    return gates, ce, nv, n_used.reshape(1), gidx.reshape(shape3), sidx.reshape(shape3)


def _combine_body(x1_ref, y0_ref, y1_ref, y2_ref, y3_ref, g_ref, gate2_ref, gf_ref, o_ref):
    g = g_ref[...]
    ff = (g[:, 0:1] * y0_ref[...] + g[:, 1:2] * y1_ref[...]) + (g[:, 2:3] * y2_ref[...] + g[:, 3:4] * y3_ref[...])
    x2 = x1_ref[...] + gate2_ref[...] * ff
    o_ref[...] = _rms(x2, gf_ref[...])


def _combine(x1, yp, gates, mod3, row0, g_final, tok0, B, L):
    n_tok, D = x1.shape
    tm = min(ROW_TILE, L)
    nl = L // tm
    blk0 = tok0 // tm
    nblk_tok = n_tok // tm
    mrow = (lambda b: 0) if row0 == 0 else (lambda b: b + row0)
    row = lambda b, i: (blk0 + b * nl + i, 0)
    yspec = lambda k: pl.BlockSpec((tm, D), lambda b, i: (k * nblk_tok + blk0 + b * nl + i, 0))
    return pl.pallas_call(
        _combine_body,
        grid=(B, nl),
        in_specs=[pl.BlockSpec((tm, D), row), yspec(0), yspec(1), yspec(2), yspec(3),
                  pl.BlockSpec((tm, TOP_K), row),
                  pl.BlockSpec((None, 1, D), lambda b, i: (mrow(b), 0, 5)),
                  pl.BlockSpec((1, D), lambda b, i: (0, 0))],
        out_specs=pl.BlockSpec((None, tm, D), lambda b, i: (b, i, 0)),
        out_shape=jax.ShapeDtypeStruct((B, L, D), F32),
        compiler_params=_params("arbitrary", "arbitrary"),
    )(x1, yp, yp, yp, yp, gates, mod3, g_final)


def _filter_body(tw_ref, bands_ref, w1_ref, b1_ref, fr_ref, w2_ref, b2_ref, w3_ref, dl_ref, k_ref, ss_ref):
    hi = functools.partial(jnp.dot, precision=HIGHEST, preferred_element_type=F32)
    t = tw_ref[:, 0:1]
    w = tw_ref[:, 1:2]
    lane = lax.broadcasted_iota(jnp.int32, (t.shape[0], LANES), 1)
    fw = w * bands_ref[...]
    z = jnp.where(lane == 0, t,
                  jnp.where(lane <= FILTER_BANDS, jnp.cos(fw),
                            jnp.where(lane <= 2 * FILTER_BANDS, -jnp.sin(fw), 0.0)))
    fr = fr_ref[...]
    h = jnp.sin(fr * (hi(z, w1_ref[...]) + b1_ref[...]))
    h = jnp.sin(fr * (hi(h, w2_ref[...]) + b2_ref[...]))
    decay = jnp.exp(-t * dl_ref[...])
    C = decay.shape[1]
    ss = []
    for g in range(k_ref.shape[1] // C):
        kg = hi(h, w3_ref[:, g * C:(g + 1) * C]) * decay
        k_ref[:, g * C:(g + 1) * C] = kg
        ss.append(jnp.sum(kg * kg, axis=0, keepdims=True))
    ss = jnp.concatenate(ss, axis=-1)

    @pl.when(pl.program_id(0) == 0)
    def _first():
        ss_ref[...] = ss

    @pl.when(pl.program_id(0) > 0)
    def _rest():
        ss_ref[...] += ss


def _hyena_filters(L, lp):
    C = D_HYENA
    pos = jnp.arange(L, dtype=F32)
    tw = jnp.stack([pos / max(L - 1, 1), 2 * math.pi * pos / L], axis=-1)
    bands = jnp.linspace(1e-4, FILTER_BANDS - 1, FILTER_BANDS, dtype=F32)
    bands128 = jnp.zeros((1, LANES), F32).at[0, 1:1 + 2 * FILTER_BANDS].set(jnp.concatenate([bands, bands]))
    w1 = jnp.zeros((LANES, lp['f_w1'].shape[1]), F32).at[:lp['f_w1'].shape[0]].set(lp['f_w1'])
    deltas = jnp.abs(jnp.linspace(math.log(DECAY_TARGET) / DECAY_PCT_LONG,
                                  math.log(DECAY_TARGET) / DECAY_PCT_SHORT, C, dtype=F32))[None]
    nk = lp['f_w3'].shape[1]
    tl = min(L, 256)
    args = [tw, bands128, w1, lp['f_b1'], lp['f_freq'], lp['f_w2'], lp['f_b2'], lp['f_w3'], deltas]
    const = lambda a: pl.BlockSpec(a.shape, lambda i: (0,) * a.ndim)
    k_un, ss = pl.pallas_call(
        _filter_body,
        grid=(L // tl,),
        in_specs=[pl.BlockSpec((tl, 2), lambda i: (i, 0))] + [const(a) for a in args[1:]],
        out_specs=[pl.BlockSpec((tl, nk), lambda i: (i, 0)), pl.BlockSpec((1, nk), lambda i: (0, 0))],
        out_shape=[jax.ShapeDtypeStruct((L, nk), F32), jax.ShapeDtypeStruct((1, nk), F32)],
        compiler_params=_params("arbitrary"),
    )(*args)
    ss = ss.reshape(HYENA_ORDER, 2, C)
    scale = lax.rsqrt(ss[:, 0] + ss[:, 1] + EPS).reshape(1, HYENA_ORDER * C)
    return k_un, scale


def _short_conv_chunk(u_ref, r0, n, prev_last, next_first, w, b):
    u = u_ref[pl.ds(r0, n), :]
    row = lax.broadcasted_iota(jnp.int32, u.shape, 0)
    up = jnp.where(row == 0, prev_last, pltpu.roll(u, 1, axis=0))
    un = jnp.where(row == n - 1, next_first, pltpu.roll(u, n - 1, axis=0))
    return up * w[0:1] + u * w[1:2] + un * w[2:3] + b


def _dft_spectrum_body(a_ref, b_ref, s_ref, o_ref):
    o_ref[...] = jnp.dot(a_ref[...], b_ref[...], precision=HIGHEST, preferred_element_type=F32) * s_ref[...]


def _dft_spectrum(a, b, scale):
    M, K = a.shape
    ncol = b.shape[1]
    tn = min(ncol, 512)
    return pl.pallas_call(
        _dft_spectrum_body,
        grid=(ncol // tn,),
        in_specs=[pl.BlockSpec((M, K), lambda j: (0, 0)), pl.BlockSpec((K, tn), lambda j: (0, j)),
                  pl.BlockSpec((1, tn), lambda j: (0, j))],
        out_specs=pl.BlockSpec((M, tn), lambda j: (0, j)),
        out_shape=jax.ShapeDtypeStruct((M, ncol), F32),
        compiler_params=_params("arbitrary"),
    )(a, b, scale)


def _hyena_short_body(v_ref, x1_ref, x2_ref, wv_ref, w1_ref, w2_ref, bv_ref, b1_ref, b2_ref, fb_ref, h_ref,
                      ff_ref, if_ref, o_ref):
    L = v_ref.shape[0]
    N = ff_ref.shape[0] // 2
    conv = lambda u_ref, w_ref, b_ref: _short_conv_chunk(u_ref, 0, L, 0.0, 0.0, w_ref[...], b_ref[...])
    z = conv(v_ref, wv_ref, bv_ref)
    gates = (conv(x1_ref, w1_ref, b1_ref), conv(x2_ref, w2_ref, b2_ref))
    for o, gate in enumerate(gates):
        X = _dot(ff_ref[...], z.astype(BF16))
        xr, xi = X[:N], X[N:]
        hr, hi = h_ref[o, :N, :], h_ref[o, N:, :]
        Y = jnp.concatenate([xr * hr - xi * hi, xr * hi + xi * hr], axis=0)
        y = _dot(if_ref[...], Y.astype(BF16))
        z = gate * (y + fb_ref[o:o + 1, :] * z)
    o_ref[...] = z


def _hyena_short(hy, lp, k_un, scale):
    B, L, _ = hy.shape
    C = D_HYENA
    N = 2 * L
    ct = 256
    ncb = C // ct
    k4 = k_un.reshape(L, HYENA_ORDER, 2, C)
    taps = jnp.concatenate([k4[:, :, 0], jnp.zeros((1, HYENA_ORDER, C), F32), k4[:0:-1, :, 1]], axis=0)
    ang = 2 * np.pi * np.outer(np.arange(N), np.arange(N)) / N
    dft = jnp.asarray(np.concatenate([np.cos(ang), -np.sin(ang)], axis=0), F32)
    H = _dft_spectrum(dft, taps.reshape(N, HYENA_ORDER * C), scale)
    H = H.reshape(2 * N, HYENA_ORDER, C).transpose(1, 0, 2)
    fwd = jnp.asarray(np.concatenate([np.cos(ang[:, :L]), -np.sin(ang[:, :L])], axis=0), BF16)
    inv = jnp.asarray(np.concatenate([np.cos(ang[:L]), -np.sin(ang[:L])], axis=1) / N, BF16)
    u = lambda g: pl.BlockSpec((None, L, ct), lambda c, b: (b, 0, g * ncb + c))
    cw = lambda g: pl.BlockSpec((3, ct), lambda c, b: (0, g * ncb + c))
    cb = lambda g: pl.BlockSpec((1, ct), lambda c, b: (0, g * ncb + c))
    const = lambda a: pl.BlockSpec(a.shape, lambda c, b: (0,) * a.ndim)
    return pl.pallas_call(
        _hyena_short_body,
        grid=(ncb, B),
        in_specs=[u(0), u(1), u(2), cw(0), cw(1), cw(2), cb(0), cb(1), cb(2),
                  pl.BlockSpec((HYENA_ORDER, ct), lambda c, b: (0, c)),
                  pl.BlockSpec((HYENA_ORDER, 2 * N, ct), lambda c, b: (0, 0, c)),
                  const(fwd), const(inv)],
        out_specs=pl.BlockSpec((None, L, ct), lambda c, b: (b, 0, c)),
        out_shape=jax.ShapeDtypeStruct((B, L, C), F32),
        compiler_params=_params("arbitrary", "arbitrary"),
    )(hy, hy, hy, lp['conv_w'], lp['conv_w'], lp['conv_w'], lp['conv_b'], lp['conv_b'], lp['conv_b'],
      lp['f_bias'], H, fwd, inv)


FFT_N1 = 128


def _fft_dims(L):
    N = 2 * L
    N1 = FFT_N1
    N2 = N // N1
    assert N1 * N2 == N and N2 % SUBLANES == 0
    return N, N1, N2, N2 + SUBLANES, 2 * N1 + SUBLANES


def _fft_tables(L):
    N, N1, N2, _, _ = _fft_dims(L)
    NH = N1 // 2
    n1 = np.arange(NH)
    k1 = np.arange(N1)
    n2 = np.arange(N2)
    th = 2 * np.pi * (n1[None, None, :] * k1[None, :, None] / N1 + n2[:, None, None] * k1[None, :, None] / N)
    g = np.concatenate([np.cos(th), -np.sin(th)], axis=1).reshape(N2 * 2 * N1, NH)
    ig = np.concatenate([np.cos(th), -np.sin(th)], axis=1).transpose(0, 2, 1) / N
    ig = ig.reshape(N2 * NH, 2 * N1)
    ph = 2 * np.pi * np.outer(n2, n2) / N2
    c, s = np.cos(ph), np.sin(ph)
    f2 = np.block([[c, s], [-s, c]])
    if2 = np.block([[c, -s], [s, c]])

    def hilo(a):
        hi = jnp.asarray(a, F32).astype(BF16)
        lo = (jnp.asarray(a, F32) - hi.astype(F32)).astype(BF16)
        return hi, lo

    return {'g': hilo(g), 'ig': hilo(ig), 'f2': hilo(f2), 'if2': hilo(if2)}


def _mm(tab, r0, nrows, x, passes):
    a_hi = tab[0][pl.ds(r0, nrows), :]
    if passes == 1:
        return _dot(a_hi, x.astype(BF16))
    x_hi, x_lo = _split_bf16(x)
    a_lo = tab[1][pl.ds(r0, nrows), :]
    return _dot(a_hi, x_hi) + (_dot(a_hi, x_lo) + _dot(a_lo, x_hi))


def _fft_stage1(tbuf, sbuf, g, dims, passes):
    _, N1, N2, P, Q = dims

    def body(n2, carry):
        x = tbuf[pl.ds(n2, N1 // 2, stride=P), :]
        out = _mm(g, pl.multiple_of(n2 * 2 * N1, 2 * N1), 2 * N1, x, passes)
        sbuf[pl.ds(pl.multiple_of(n2 * Q, SUBLANES), 2 * N1), :] = out
        return carry

    lax.fori_loop(0, N2, body, 0)


def _fft_stage2_load(sbuf, k1, dims):
    _, N1, N2, _, Q = dims
    return jnp.concatenate([sbuf[pl.ds(k1, N2, stride=Q), :], sbuf[pl.ds(N1 + k1, N2, stride=Q), :]], axis=0)


def _hyena_long_body(v_ref, x1_ref, x2_ref, wv_ref, w1_ref, w2_ref, bv_ref, b1_ref, b2_ref, fb_ref, h_ref,
                     g_ref, ig_ref, f2_ref, if2_ref, o_ref, zbuf, g1buf, g2buf, sbuf, *, dims):
    N, N1, N2, P, Q = dims
    NH = N1 // 2
    zero = jnp.zeros((1, LANES), F32)

    for u_ref, w_ref, b_ref, buf in ((v_ref, wv_ref, bv_ref, zbuf), (x1_ref, w1_ref, b1_ref, g1buf),
                                     (x2_ref, w2_ref, b2_ref, g2buf)):
        w, b = w_ref[...], b_ref[...]
        for n1 in range(NH):
            r0 = n1 * N2
            prev_last = zero if n1 == 0 else u_ref[r0 - 1:r0, :]
            next_first = zero if n1 == NH - 1 else u_ref[r0 + N2:r0 + N2 + 1, :]
            buf[n1 * P:n1 * P + N2, :] = _short_conv_chunk(u_ref, r0, N2, prev_last, next_first, w, b)

    for o, gbuf in enumerate((g1buf, g2buf)):
        _fft_stage1(zbuf, sbuf, (g_ref, None), dims, 1)

        def per_k1(k1, carry):
            X = _dot(f2_ref[...], _fft_stage2_load(sbuf, k1, dims).astype(BF16))
            h0 = pl.multiple_of(k1 * 2 * N2, 2 * N2)
            hr = h_ref[o, pl.ds(h0, N2), :]
            hi = h_ref[o, pl.ds(h0 + N2, N2), :]
            xr, xi = X[:N2], X[N2:]
            Y = jnp.concatenate([xr * hr - xi * hi, xr * hi + xi * hr], axis=0)
            Cc = _dot(if2_ref[...], Y.astype(BF16))
            sbuf[pl.ds(k1, N2, stride=Q), :] = Cc[:N2]
            sbuf[pl.ds(N1 + k1, N2, stride=Q), :] = Cc[N2:]
            return carry

        lax.fori_loop(0, N1, per_k1, 0)
        fb = fb_ref[o:o + 1, :]

        def per_n2(n2, carry):
            D = sbuf[pl.ds(pl.multiple_of(n2 * Q, SUBLANES), 2 * N1), :]
            y = _dot(ig_ref[pl.ds(pl.multiple_of(n2 * NH, NH), NH), :], D.astype(BF16))
            rows = pl.ds(n2, NH, stride=P)
            zbuf[rows, :] = gbuf[rows, :] * (y + fb * zbuf[rows, :])
            return carry

        lax.fori_loop(0, N2, per_n2, 0)

    for n1 in range(NH):
        o_ref[n1 * N2:(n1 + 1) * N2, :] = zbuf[n1 * P:n1 * P + N2, :]


def _fft_spectrum_body(hf_ref, hb_ref, s_ref, g_hi, g_lo, f2_hi, f2_lo, o_ref, tbuf, sbuf, *, dims):
    N, N1, N2, P, Q = dims
    NH = N1 // 2
    scale = s_ref[...]
    for d, h_ref in enumerate((hf_ref, hb_ref)):
        for n1 in range(NH):
            h = h_ref[n1 * N2:(n1 + 1) * N2, :] * scale
            if d == 1 and n1 == 0:
                h = jnp.where(lax.broadcasted_iota(jnp.int32, h.shape, 0) == 0, 0.0, h)
            tbuf[n1 * P:n1 * P + N2, :] = h
        _fft_stage1(tbuf, sbuf, (g_hi, g_lo), dims, 3)

        def per_k1(k1, carry):
            X = _mm((f2_hi, f2_lo), 0, 2 * N2, _fft_stage2_load(sbuf, k1, dims), 3)
            h0 = pl.multiple_of(k1 * 2 * N2, 2 * N2)
            if d == 0:
                o_ref[pl.ds(h0, 2 * N2), :] = X
            else:
                o_ref[pl.ds(h0, N2), :] += X[:N2]
                o_ref[pl.ds(h0 + N2, N2), :] -= X[N2:]
            return carry

        lax.fori_loop(0, N1, per_k1, 0)


def _hyena_long(hy, lp, k_un, scale):
    B, L, _ = hy.shape
    C = D_HYENA
    dims = _fft_dims(L)
    N, N1, N2, P, Q = dims
    NH = N1 // 2
    tabs = _fft_tables(L)
    ncb = C // LANES
    nspec = HYENA_ORDER * ncb
    const1 = lambda a, n: pl.BlockSpec(a.shape, (lambda *i: (0,) * a.ndim), pipeline_mode=pl.Buffered(1))
    hcol = lambda d: pl.BlockSpec((L, LANES), lambda j: (0, (j // ncb) * 2 * ncb + d * ncb + j % ncb))
    H = pl.pallas_call(
        functools.partial(_fft_spectrum_body, dims=dims),
        grid=(nspec,),
        in_specs=[hcol(0), hcol(1), pl.BlockSpec((1, LANES), lambda j: (0, j)),
                  const1(tabs['g'][0], 1), const1(tabs['g'][1], 1),
                  const1(tabs['f2'][0], 1), const1(tabs['f2'][1], 1)],
        out_specs=pl.BlockSpec((2 * N, LANES), lambda j: (0, j)),
        out_shape=jax.ShapeDtypeStruct((2 * N, HYENA_ORDER * C), F32),
        scratch_shapes=[pltpu.VMEM((NH * P, LANES), F32), pltpu.VMEM((N2 * Q, LANES), F32)],
        compiler_params=_params("arbitrary"),
    )(k_un, k_un, scale, tabs['g'][0], tabs['g'][1], tabs['f2'][0], tabs['f2'][1])
    H = H.reshape(2 * N, HYENA_ORDER, C).transpose(1, 0, 2)

    one = pl.Buffered(1)
    u = lambda g: pl.BlockSpec((None, L, LANES), lambda c, b: (b, 0, g * ncb + c), pipeline_mode=one)
    cw = lambda g: pl.BlockSpec((3, LANES), lambda c, b: (0, g * ncb + c))
    cb = lambda g: pl.BlockSpec((1, LANES), lambda c, b: (0, g * ncb + c))
    return pl.pallas_call(
        functools.partial(_hyena_long_body, dims=dims),
        grid=(ncb, B),
        in_specs=[u(0), u(1), u(2), cw(0), cw(1), cw(2), cb(0), cb(1), cb(2),
                  pl.BlockSpec((HYENA_ORDER, LANES), lambda c, b: (0, c)),
                  pl.BlockSpec((HYENA_ORDER, 2 * N, LANES), lambda c, b: (0, 0, c), pipeline_mode=one),
                  const1(tabs['g'][0], 2), const1(tabs['ig'][0], 2),
                  const1(tabs['f2'][0], 2), const1(tabs['if2'][0], 2)],
        out_specs=pl.BlockSpec((None, L, LANES), lambda c, b: (b, 0, c)),
        out_shape=jax.ShapeDtypeStruct((B, L, C), F32),
        scratch_shapes=[pltpu.VMEM((NH * P, LANES), F32), pltpu.VMEM((NH * P, LANES), F32),
                        pltpu.VMEM((NH * P, LANES), F32), pltpu.VMEM((N2 * Q, LANES), F32)],
        compiler_params=_params("arbitrary", "arbitrary"),
    )(hy, hy, hy, lp['conv_w'], lp['conv_w'], lp['conv_w'], lp['conv_b'], lp['conv_b'], lp['conv_b'],
      lp['f_bias'], H, tabs['g'][0], tabs['ig'][0], tabs['f2'][0], tabs['if2'][0])


def _hyena(hy, lp):
    L = hy.shape[1]
    k_un, scale = _hyena_filters(L, lp)
    if 2 * L >= 2 * FFT_N1 * SUBLANES and (2 * L) % (FFT_N1 * SUBLANES) == 0:
        return _hyena_long(hy, lp, k_un, scale)
    return _hyena_short(hy, lp, k_un, scale)


def kernel(x_prompt, x_sample, cache_k, cache_v, c, c_ctx, w_mod, b_mod, g_mix, w_in, conv_w, conv_b, f_w1, f_b1,
           f_freq, f_w2, f_b2, f_w3, f_bias, rpb, g_out_hy, g_out_at, w_out, g_ffn, w_router, b_router, w_exp1,
           b_exp1, w_exp2, b_exp2, g_final):
    depth = w_mod.shape[0]
    Bp, Lp, D = x_prompt.shape
    Bs, Ls, _ = x_sample.shape
    n_ctx, n_lat = Bp * Lp, Bs * Ls
    n_tok = n_ctx + n_lat
    d_attn = N_HEADS * HEAD_DIM
    q_off = 3 * D_HYENA
    row2 = lambda a: a.reshape(1, -1)

    cond = jnp.zeros((SUBLANES, D), F32).at[0].set(c_ctx).at[1:1 + Bs].set(c)
    xp, xs = x_prompt, x_sample
    new_k, new_v = [], []
    for l in range(depth):
        wr = jnp.zeros((D, LANES), F32).at[:, :N_EXPERTS].set(w_router[l])
        wr_hi = wr.astype(BF16)
        lp = {
            'conv_w': conv_w[l], 'conv_b': row2(conv_b[l]), 'f_w1': f_w1[l], 'f_b1': row2(f_b1[l]),
            'f_freq': row2(f_freq[l]), 'f_w2': f_w2[l], 'f_b2': row2(f_b2[l]), 'f_w3': f_w3[l],
            'f_bias': f_bias[l], 'g_out_hy': row2(g_out_hy[l]), 'g_out_at': row2(g_out_at[l]),
            'w_out': w_out[l].astype(BF16), 'g_ffn': row2(g_ffn[l]),
            'wr_hi': wr_hi, 'wr_lo': (wr - wr_hi.astype(F32)).astype(BF16),
            'b_router': jnp.zeros((1, LANES), F32).at[0, :N_EXPERTS].set(b_router[l]),
            'w_exp1': w_exp1[l], 'b_exp1': b_exp1[l][:, None, :], 'w_exp2': w_exp2[l],
            'b_exp2': b_exp2[l][:, None, :],
        }
        mod = _modulation(cond, w_mod[l], row2(b_mod[l]))
        mod3 = mod.reshape(SUBLANES, 1, 6 * D)
        w_in_b = w_in[l].astype(BF16)
        g_mix_l = row2(g_mix[l])

        proj_p = _in_projection(xp, mod3, 0, g_mix_l, w_in_b)
        proj_s = _in_projection(xs, mod3, 1, g_mix_l, w_in_b)
        kv = proj_p[..., q_off + d_attn:].reshape(Bp, Lp, 2, N_HEADS, HEAD_DIM).transpose(2, 0, 3, 1, 4)
        new_k.append(kv[0])
        new_v.append(kv[1])

        hy_p = _hyena(proj_p, lp)
        hy_s = _hyena(proj_s, lp)
        at_p = _context_attention(proj_p, q_off)
        heads_last = lambda t: t.transpose(0, 2, 1, 3).reshape(Bs, t.shape[2], d_attn)
        at_s = _neighbourhood_attention(proj_s, q_off, heads_last(cache_k[:, l]), heads_last(cache_v[:, l]), rpb[l])

        x1, h2u, logits = _out_projection((hy_p, at_p, xp), (hy_s, at_s, xs), mod3, lp)

        gates, ce, nv, n_used, gidx, sidx = _routing(logits[:, :N_EXPERTS], n_tok)
        yp = _experts(ce, nv, n_used, gidx, sidx, h2u, lp, n_tok * TOP_K)
        last = l == depth - 1
        gf = row2(g_final) if last else None
        assert last, "deeper stacks need the un-normalised residual between layers"
        xp = _combine(x1, yp, gates, mod3, 0, gf, 0, Bp, Lp)
        xs = _combine(x1, yp, gates, mod3, 1, gf, n_ctx, Bs, Ls)

    return xp, xs, jnp.stack(new_k, axis=1), jnp.stack(new_v, axis=1)
```

```python
import functools
import math

import numpy as np
import jax
import jax.numpy as jnp
from jax import lax
from jax.experimental import pallas as pl
from jax.experimental.pallas import tpu as pltpu

F32 = jnp.float32
BF16 = jnp.bfloat16
HIGHEST = lax.Precision.HIGHEST

GRID_W = 64
N_HEADS = 16
HEAD_DIM = 64
D_HYENA = 1024
HYENA_ORDER = 2
FILTER_BANDS = 16
DECAY_TARGET = 1e-2
DECAY_PCT_SHORT = 0.3
DECAY_PCT_LONG = 1.5
WIN_H = 8
WIN_W = 16
N_EXPERTS = 32
TOP_K = 4
SWIGLU_ALPHA = 1.702
SWIGLU_LIMIT = 7.0
EPS = 1e-6
NEG_INF = -1e30

LANES = 128
SUBLANES = 8
VMEM_LIMIT = 56 * 1024 * 1024

ROW_TILE = 256
NA_Q_ROWS = 4
MOE_CHUNK = 1024
MOE_SUB = 256
MOE_FF_TILE = 256


def _params(*sem):
    return pltpu.CompilerParams(dimension_semantics=sem, vmem_limit_bytes=VMEM_LIMIT)


def _rms(x, g):
    return x * lax.rsqrt(jnp.mean(x * x, axis=-1, keepdims=True) + EPS) * g


def _split_bf16(x):
    hi = x.astype(BF16)
    lo = (x - hi.astype(F32)).astype(BF16)
    return hi, lo


def _dot(a, b):
    return jnp.dot(a, b, preferred_element_type=F32)


def _dot_nt(a, b):
    return lax.dot_general(a, b, (((1,), (1,)), ((), ())), preferred_element_type=F32)


def _mod_body(c_ref, w_ref, b_ref, o_ref):
    c = c_ref[...]
    s = c / (1.0 + jnp.exp(-c))
    o_ref[...] = jnp.dot(s, w_ref[...], precision=HIGHEST, preferred_element_type=F32) + b_ref[...]


def _modulation(cc, w, b):
    D, N = w.shape
    tn = min(N, 1536)
    return pl.pallas_call(
        _mod_body,
        grid=(N // tn,),
        in_specs=[pl.BlockSpec((SUBLANES, D), lambda j: (0, 0)),
                  pl.BlockSpec((D, tn), lambda j: (0, j)),
                  pl.BlockSpec((1, tn), lambda j: (0, j))],
        out_specs=pl.BlockSpec((SUBLANES, tn), lambda j: (0, j)),
        out_shape=jax.ShapeDtypeStruct((SUBLANES, N), F32),
        name="modulation",
        compiler_params=_params("arbitrary"),
    )(cc, w, b)


def _inproj_body(x_ref, shift_ref, scale_ref, g_ref, w_ref, o_ref, *, n_chunk):
    h = _rms(x_ref[...], g_ref[...]) * (1.0 + scale_ref[...]) + shift_ref[...]
    hb = h.astype(BF16)
    n_out = o_ref.shape[-1]

    def col(j, carry):
        c0 = pl.multiple_of(j * n_chunk, n_chunk)
        o_ref[:, pl.ds(c0, n_chunk)] = _dot(hb, w_ref[:, pl.ds(c0, n_chunk)])
        return carry

    lax.fori_loop(0, n_out // n_chunk, col, 0)


def _in_projection(x, mod3, row0, g, w_bf16):
    B, L, D = x.shape
    N = w_bf16.shape[1]
    tm = min(ROW_TILE, L)
    mrow = (lambda b: 0) if row0 == 0 else (lambda b: b + row0)
    return pl.pallas_call(
        functools.partial(_inproj_body, n_chunk=512),
        grid=(B, L // tm),
        in_specs=[pl.BlockSpec((None, tm, D), lambda b, i: (b, i, 0)),
                  pl.BlockSpec((None, 1, D), lambda b, i: (mrow(b), 0, 0)),
                  pl.BlockSpec((None, 1, D), lambda b, i: (mrow(b), 0, 1)),
                  pl.BlockSpec((1, D), lambda b, i: (0, 0)),
                  pl.BlockSpec((D, N), lambda b, i: (0, 0), pipeline_mode=pl.Buffered(1))],
        out_specs=pl.BlockSpec((None, tm, N), lambda b, i: (b, i, 0)),
        out_shape=jax.ShapeDtypeStruct((B, L, N), F32),
        name="in_projection",
        compiler_params=_params("arbitrary", "arbitrary"),
    )(x, mod3, mod3, g, w_bf16)


def _head_lane_masks(rows, hd):
    lane = lax.broadcasted_iota(jnp.int32, (rows, LANES), 1)
    return [(lane >= h * hd) & (lane < (h + 1) * hd) for h in range(LANES // hd)]


def _ctx_attn_body(q_ref, k_ref, v_ref, o_ref, *, scale, hd):
    q = q_ref[...]
    k = k_ref[...].astype(BF16)
    v = v_ref[...].astype(BF16)
    masks = _head_lane_masks(q.shape[0], hd)
    out = None
    for msk in masks:
        qh = jnp.where(msk, q, 0.0).astype(BF16)
        s = _dot_nt(qh, k) * scale
        p = jnp.exp(s - jnp.max(s, axis=-1, keepdims=True))
        o = _dot(p.astype(BF16), v) / jnp.sum(p, axis=-1, keepdims=True)
        out = o if out is None else jnp.where(msk, o, out)
    o_ref[...] = out


def _context_attention(proj, q_off):
    B, L, _ = proj.shape
    d_attn = N_HEADS * HEAD_DIM
    nhp = d_attn // LANES
    qb, kb, vb = (q_off // LANES, (q_off + d_attn) // LANES, (q_off + 2 * d_attn) // LANES)
    spec = lambda base: pl.BlockSpec((None, L, LANES), lambda b, h: (b, 0, base + h))
    return pl.pallas_call(
        functools.partial(_ctx_attn_body, scale=1.0 / math.sqrt(HEAD_DIM), hd=HEAD_DIM),
        grid=(B, nhp),
        in_specs=[spec(qb), spec(kb), spec(vb)],
        out_specs=pl.BlockSpec((None, L, LANES), lambda b, h: (b, 0, h)),
        out_shape=jax.ShapeDtypeStruct((B, L, d_attn), F32),
        name="context_attention",
        compiler_params=_params("arbitrary", "arbitrary"),
    )(proj, proj, proj)


def _na_tables(rows, rpb):
    W = GRID_W
    kh = min(WIN_H, rows)
    rbq = NA_Q_ROWS
    kwr = rbq + kh
    assert rows % rbq == 0 and kwr <= rows
    nblk = rows // rbq
    ws = np.clip(np.arange(nblk) * rbq - kh // 2, 0, rows - kwr)
    rq = np.arange(nblk)[:, None, None] * rbq + np.arange(rbq)[None, :, None]
    rk = ws[:, None, None] + np.arange(kwr)[None, None, :]
    rs = np.clip(rq - kh // 2, 0, rows - kh)
    vr = (rk >= rs) & (rk < rs + kh)
    dr = np.where(vr, rk - rq + WIN_H - 1, 0)
    patterns = np.concatenate([dr.reshape(nblk, -1), vr.reshape(nblk, -1)], axis=1)
    _, first, btype = np.unique(patterns, axis=0, return_index=True, return_inverse=True)
    dr_t, vr_t = dr[first], vr[first]
    cq = np.arange(W)[:, None]
    ck = np.arange(W)[None, :]
    cs = np.clip(cq - WIN_W // 2, 0, W - WIN_W)
    vc = (ck >= cs) & (ck < cs + WIN_W)
    dc = np.clip(ck - cq, -(WIN_W - 1), WIN_W - 1) + WIN_W - 1
    nt = dr_t.shape[0]
    rowsel = (dr_t[..., None] == np.arange(2 * WIN_H - 1)) & vr_t[..., None]
    colsel = (dc[..., None] == np.arange(2 * WIN_W - 1)) & vc[..., None]
    vals = jnp.einsum('tajr,hrc,qkc->htaqjk', rowsel.astype(np.float32), rpb.astype(F32),
                      colsel.astype(np.float32), precision=HIGHEST)
    valid = vr_t[:, :, None, :, None] & vc[None, None, :, None, :]
    bias = jnp.where(valid[None], vals, NEG_INF)
    bias = bias.reshape(rpb.shape[0], nt, rbq * W, kwr * W)
    return ws.astype(np.int32), btype.reshape(-1).astype(np.int32), bias


def _na_body(ws_ref, bt_ref, q_ref, k_ref, v_ref, kc_ref, vc_ref, bias_ref, o_ref, *, scale, hd, nk):
    i = pl.program_id(2)
    start = pl.multiple_of(ws_ref[i] * GRID_W, GRID_W)
    q = q_ref[...]
    kl = k_ref[pl.ds(start, nk), :].astype(BF16)
    vl = v_ref[pl.ds(start, nk), :].astype(BF16)
    kc = kc_ref[...].astype(BF16)
    vc = vc_ref[...].astype(BF16)
    masks = _head_lane_masks(q.shape[0], hd)
    out = None
    for h, msk in enumerate(masks):
        qh = jnp.where(msk, q, 0.0).astype(BF16)
        sl = _dot_nt(qh, kl) * scale + bias_ref[h]
        sc = _dot_nt(qh, kc) * scale
        m = jnp.maximum(jnp.max(sl, axis=-1, keepdims=True), jnp.max(sc, axis=-1, keepdims=True))
        p_l = jnp.exp(sl - m)
        p_c = jnp.exp(sc - m)
        den = jnp.sum(p_l, axis=-1, keepdims=True) + jnp.sum(p_c, axis=-1, keepdims=True)
        o = (_dot(p_l.astype(BF16), vl) + _dot(p_c.astype(BF16), vc)) / den
        out = o if out is None else jnp.where(msk, o, out)
    o_ref[...] = out


def _neighbourhood_attention(proj, q_off, kc, vc, rpb):
    B, L, _ = proj.shape
    Lc = kc.shape[1]
    d_attn = N_HEADS * HEAD_DIM
    hpb = LANES // HEAD_DIM
    nhp = d_attn // LANES
    rows = L // GRID_W
    ws, btype, bias = _na_tables(rows, rpb)
    nq = NA_Q_ROWS * GRID_W
    nk = (NA_Q_ROWS + min(WIN_H, rows)) * GRID_W
    nblk = rows // NA_Q_ROWS
    qb, kb, vb = (q_off // LANES, (q_off + d_attn) // LANES, (q_off + 2 * d_attn) // LANES)
    full = lambda base: pl.BlockSpec((None, L, LANES), lambda b, h, i, ws_r, bt_r: (b, 0, base + h))
    ctx = pl.BlockSpec((None, Lc, LANES), lambda b, h, i, ws_r, bt_r: (b, 0, h))
    grid_spec = pltpu.PrefetchScalarGridSpec(
        num_scalar_prefetch=2,
        grid=(B, nhp, nblk),
        in_specs=[pl.BlockSpec((None, nq, LANES), lambda b, h, i, ws_r, bt_r: (b, i, qb + h)),
                  full(kb), full(vb), ctx, ctx,
                  pl.BlockSpec((hpb, None, nq, nk), lambda b, h, i, ws_r, bt_r: (h, bt_r[i], 0, 0))],
        out_specs=pl.BlockSpec((None, nq, LANES), lambda b, h, i, ws_r, bt_r: (b, i, h)),
    )
    return pl.pallas_call(
        functools.partial(_na_body, scale=1.0 / math.sqrt(HEAD_DIM), hd=HEAD_DIM, nk=nk),
        grid_spec=grid_spec,
        out_shape=jax.ShapeDtypeStruct((B, L, d_attn), F32),
        name="neighbourhood_attention",
        compiler_params=_params("arbitrary", "arbitrary", "arbitrary"),
    )(jnp.asarray(ws), jnp.asarray(btype), proj, proj, proj, kc, vc, bias)


def _outproj_body(yhp_ref, yap_ref, xp_ref, yhs_ref, yas_ref, xs_ref, gate1_ref, shift2_ref, scale2_ref, ghy_ref,
                  gat_ref, w_ref, gffn_ref, wrh_ref, wrl_ref, br_ref, x1_ref, h2u_ref, lg_ref, *, n_ctx_tiles):
    def run(yh_ref, ya_ref, x_ref):
        cat = jnp.concatenate([_rms(yh_ref[...], ghy_ref[...]), _rms(ya_ref[...], gat_ref[...])], axis=-1)
        mix = _dot(cat.astype(BF16), w_ref[...])
        x1 = x_ref[...] + gate1_ref[...] * mix
        x1_ref[...] = x1
        h2 = _rms(x1, gffn_ref[...]) * (1.0 + scale2_ref[...]) + shift2_ref[...]
        hi, lo = _split_bf16(h2)
        lg_ref[...] = (_dot(hi, wrh_ref[...]) + _dot(lo, wrh_ref[...]) + _dot(hi, wrl_ref[...])) + br_ref[...]
        bits = lax.bitcast_convert_type(hi.astype(F32), jnp.uint32)
        half = bits.shape[-1] // 2
        words = (bits[:, :half] >> 16) | (bits[:, half:] & jnp.uint32(0xFFFF0000))
        per = half // LANES
        for s in range(per):
            h2u_ref[pl.ds(s, words.shape[0], stride=per), :] = words[:, s * LANES:(s + 1) * LANES]

    is_ctx = pl.program_id(0) < n_ctx_tiles
    pl.when(is_ctx)(lambda: run(yhp_ref, yap_ref, xp_ref))
    pl.when(jnp.logical_not(is_ctx))(lambda: run(yhs_ref, yas_ref, xs_ref))


def _out_projection(ctx, lat, mod3, lp):
    Bp, Lp, D = ctx[2].shape
    Bs, Ls, _ = lat[2].shape
    dh, da = ctx[0].shape[-1], ctx[1].shape[-1]
    tm = min(ROW_TILE, Lp, Ls)
    nct, nlt = Bp * Lp // tm, Bs * Ls // tm
    n_tok = (nct + nlt) * tm
    lat_tiles = Ls // tm
    mrow = lambda i: jnp.where(i < nct, 0, 1 + (i - nct) // lat_tiles)
    modspec = lambda c: pl.BlockSpec((None, 1, D), lambda i: (mrow(i), 0, c))
    const = lambda shape: pl.BlockSpec(shape, lambda i: (0,) * len(shape))
    crow = lambda w: pl.BlockSpec((tm, w), lambda i: (jnp.minimum(i, nct - 1), 0))
    lrow = lambda w: pl.BlockSpec((tm, w), lambda i: (jnp.maximum(i - nct, 0), 0))
    flat = lambda t: t.reshape(-1, t.shape[-1])
    return pl.pallas_call(
        functools.partial(_outproj_body, n_ctx_tiles=nct),
        grid=(nct + nlt,),
        in_specs=[crow(dh), crow(da), crow(D), lrow(dh), lrow(da), lrow(D),
                  modspec(2), modspec(3), modspec(4),
                  const((1, dh)), const((1, da)),
                  pl.BlockSpec((dh + da, D), lambda i: (0, 0), pipeline_mode=pl.Buffered(1)),
                  const((1, D)), const((D, LANES)), const((D, LANES)), const((1, LANES))],
        out_specs=[pl.BlockSpec((tm, D), lambda i: (i, 0)),
                   pl.BlockSpec((tm * (D // 2 // LANES), LANES), lambda i: (i, 0)),
                   pl.BlockSpec((tm, LANES), lambda i: (i, 0))],
        out_shape=[jax.ShapeDtypeStruct((n_tok, D), F32),
                   jax.ShapeDtypeStruct((n_tok * (D // 2 // LANES), LANES), jnp.uint32),
                   jax.ShapeDtypeStruct((n_tok, LANES), F32)],
        name="out_projection",
        compiler_params=_params("arbitrary"),
    )(*[flat(t) for t in ctx], *[flat(t) for t in lat], mod3, mod3, mod3, lp['g_out_hy'], lp['g_out_at'],
      lp['w_out'], lp['g_ffn'], lp['wr_hi'], lp['wr_lo'], lp['b_router'])


def _expert_body(ce_ref, nv_ref, nu_ref, gidx_ref, sidx_ref, h2u_hbm, w1g_ref, w1u_ref, b1g_ref, b1u_ref,
                 w2_ref, b2_ref, yp_hbm, xu, xb, acc, ystage, w1g_b, w1u_b, w2_b, gsem, ssem):
    c = pl.program_id(0)
    f = pl.program_id(1)
    nf = pl.num_programs(1)
    half = xb.shape[1] // 2
    xt = half // LANES
    yt = acc.shape[1] // LANES
    nsub = (nv_ref[c] + MOE_SUB - 1) // MOE_SUB
    nrows = nsub * MOE_SUB

    def tokens_copy(src, dst, sem, per, n, t0=0):
        return pltpu.make_async_copy(src.at[pl.ds(0, n * per), :], dst.at[pl.ds(t0 * per, n * per), :], sem)

    @pl.when(c < nu_ref[0])
    def _chunk():
        @pl.when(f == 0)
        def _gather():
            def issue(i, carry):
                tok = gidx_ref[i // LANES, i % LANES]
                pltpu.make_async_copy(h2u_hbm.at[pl.ds(pl.multiple_of(tok * xt, xt), xt), :],
                                      xu.at[pl.ds(pl.multiple_of(i * xt, xt), xt), :], gsem).start()
                return carry

            lax.fori_loop(0, nrows, issue, 0)

            def wait(sb, carry):
                tokens_copy(h2u_hbm, xu, gsem, xt, MOE_SUB, pl.multiple_of(sb * MOE_SUB, MOE_SUB)).wait()
                return carry

            lax.fori_loop(0, nsub, wait, 0)

            def unpack(sb, carry):
                r0 = pl.multiple_of(sb * MOE_SUB, MOE_SUB)
                for s in range(xt):
                    u = xu[pl.ds(r0 * xt + s, MOE_SUB, stride=xt), :]
                    lo = lax.bitcast_convert_type(u << 16, F32)
                    hi = lax.bitcast_convert_type(u & jnp.uint32(0xFFFF0000), F32)
                    xb[pl.ds(r0, MOE_SUB), s * LANES:(s + 1) * LANES] = lo.astype(BF16)
                    xb[pl.ds(r0, MOE_SUB), half + s * LANES:half + (s + 1) * LANES] = hi.astype(BF16)
                acc[pl.ds(r0, MOE_SUB), :] = jnp.zeros((MOE_SUB, acc.shape[1]), F32)
                return carry

            lax.fori_loop(0, nsub, unpack, 0)

        w1g_b[...] = w1g_ref[...].astype(BF16)
        w1u_b[...] = w1u_ref[...].astype(BF16)
        w2_b[...] = w2_ref[...].astype(BF16)

        def sub(sb, carry):
            r0 = pl.multiple_of(sb * MOE_SUB, MOE_SUB)
            x = xb[pl.ds(r0, MOE_SUB), :]
            gate = jnp.minimum(_dot(x, w1g_b[...]) + b1g_ref[...], SWIGLU_LIMIT)
            up = jnp.clip(_dot(x, w1u_b[...]) + b1u_ref[...], -SWIGLU_LIMIT, SWIGLU_LIMIT)
            glu = gate / (1.0 + jnp.exp(-SWIGLU_ALPHA * gate))
            a = ((up + 1.0) * glu).astype(BF16)
            acc[pl.ds(r0, MOE_SUB), :] += _dot(a, w2_b[...])
            return carry

        lax.fori_loop(0, nsub, sub, 0)

        @pl.when(f == nf - 1)
        def _scatter():
            def stage(sb, carry):
                r0 = pl.multiple_of(sb * MOE_SUB, MOE_SUB)
                for s in range(yt):
                    cols = slice(s * LANES, (s + 1) * LANES)
                    ystage[pl.ds(r0 * yt + s, MOE_SUB, stride=yt), :] = acc[pl.ds(r0, MOE_SUB), cols] + b2_ref[:, cols]
                return carry

            lax.fori_loop(0, nsub, stage, 0)

            nv = nv_ref[c]

            def issue(i, carry):
                dst = sidx_ref[i // LANES, i % LANES]
                pltpu.make_async_copy(ystage.at[pl.ds(pl.multiple_of(i * yt, yt), yt), :],
                                      yp_hbm.at[pl.ds(pl.multiple_of(dst * yt, yt), yt), :], ssem).start()
                return carry

            lax.fori_loop(0, nv, issue, 0)

            def wait(sb, carry):
                tokens_copy(yp_hbm, ystage, ssem, yt, MOE_SUB).wait()
                return carry

            lax.fori_loop(0, nv // MOE_SUB, wait, 0)
            bit = MOE_SUB // 2
            while bit:
                pl.when((nv & bit) != 0)(tokens_copy(yp_hbm, ystage, ssem, yt, bit).wait)
                bit //= 2


def _experts(chunk_e, chunk_nv, n_used, gidx, sidx, h2u, lp, n_out_rows):
    E, D, two_ff = lp['w_exp1'].shape
    d_ff = two_ff // 2
    nc = gidx.shape[0]
    R, tf = MOE_CHUNK, MOE_FF_TILE
    nf = d_ff // tf

    def ff(c, f, nu):
        return jnp.where(c < nu[0], f, nf - 1)

    smem = lambda: pl.BlockSpec((None, R // LANES, LANES), lambda c, f, ce, nv, nu: (c, 0, 0),
                                memory_space=pltpu.SMEM)
    grid_spec = pltpu.PrefetchScalarGridSpec(
        num_scalar_prefetch=3,
        grid=(nc, nf),
        in_specs=[smem(), smem(),
                  pl.BlockSpec(memory_space=pl.ANY),
                  pl.BlockSpec((None, D, tf), lambda c, f, ce, nv, nu: (ce[c], 0, ff(c, f, nu))),
                  pl.BlockSpec((None, D, tf), lambda c, f, ce, nv, nu: (ce[c], 0, nf + ff(c, f, nu))),
                  pl.BlockSpec((None, 1, tf), lambda c, f, ce, nv, nu: (ce[c], 0, ff(c, f, nu))),
                  pl.BlockSpec((None, 1, tf), lambda c, f, ce, nv, nu: (ce[c], 0, nf + ff(c, f, nu))),
                  pl.BlockSpec((None, tf, D), lambda c, f, ce, nv, nu: (ce[c], ff(c, f, nu), 0)),
                  pl.BlockSpec((None, 1, D), lambda c, f, ce, nv, nu: (ce[c], 0, 0))],
        out_specs=pl.BlockSpec(memory_space=pl.ANY),
        scratch_shapes=[pltpu.VMEM((R * (D // 2 // LANES), LANES), jnp.uint32),
                        pltpu.VMEM((R, D), BF16),
                        pltpu.VMEM((R, D), F32),
                        pltpu.VMEM((R * (D // LANES), LANES), F32),
                        pltpu.VMEM((D, tf), BF16),
                        pltpu.VMEM((D, tf), BF16),
                        pltpu.VMEM((tf, D), BF16),
                        pltpu.SemaphoreType.DMA(()),
                        pltpu.SemaphoreType.DMA(())],
    )
    return pl.pallas_call(
        _expert_body,
        grid_spec=grid_spec,
        out_shape=jax.ShapeDtypeStruct((n_out_rows * (D // LANES), LANES), F32),
        name="experts",
        compiler_params=pltpu.CompilerParams(dimension_semantics=("arbitrary", "arbitrary"),
                                             vmem_limit_bytes=VMEM_LIMIT, has_side_effects=True,
                                             disable_bounds_checks=True),
    )(chunk_e, chunk_nv, n_used, gidx, sidx, h2u, lp['w_exp1'], lp['w_exp1'], lp['b_exp1'], lp['b_exp1'],
      lp['w_exp2'], lp['b_exp2'])


def _routing(logits, n_tok):
    R = MOE_CHUNK
    top_v, top_i = lax.top_k(logits, TOP_K)
    gates = jax.nn.softmax(top_v, axis=-1)
    n_pairs = n_tok * TOP_K
    flat_e = top_i.reshape(n_pairs).astype(jnp.int32)
    experts = jnp.arange(N_EXPERTS, dtype=jnp.int32)
    counts = jnp.sum((flat_e[:, None] == experts[None]).astype(jnp.int32), axis=0)
    padded = (counts + R - 1) // R * R
    pad_end = jnp.cumsum(padded)
    pad_start = pad_end - padded
    nc = n_pairs // R + N_EXPERTS
    pair = jnp.arange(n_pairs, dtype=jnp.int32)
    tok = pair // TOP_K
    dest = (pair % TOP_K) * n_tok + tok
    fill_e = jnp.repeat(experts, R)
    fill_i = jnp.tile(jnp.arange(R, dtype=jnp.int32), N_EXPERTS)
    fill_key = jnp.where(fill_i < jnp.repeat(padded - counts, R), 2 * fill_e + 1, 2 * N_EXPERTS)
    zeros = jnp.zeros((N_EXPERTS * R,), jnp.int32)
    _, gidx, sidx = lax.sort((jnp.concatenate([2 * flat_e, fill_key]), jnp.concatenate([tok, zeros]),
                              jnp.concatenate([dest, zeros])), num_keys=1)
    n_used = (pad_end[-1] // R).astype(jnp.int32)
    cstart = jnp.arange(nc, dtype=jnp.int32) * R
    ce = jnp.minimum(jnp.searchsorted(pad_end, cstart, side='right'), N_EXPERTS - 1).astype(jnp.int32)
    ce = jnp.where(jnp.arange(nc) < n_used, ce, ce[jnp.maximum(n_used - 1, 0)])
    nv = jnp.clip(counts[ce] - (cstart - pad_start[ce]), 0, R).astype(jnp.int32)
    nv = jnp.where(jnp.arange(nc) < n_used, nv, 0)
    shape3 = (nc, R // LANES, LANES)
    return gates, ce, nv, n_used.reshape(1), gidx.reshape(shape3), sidx.reshape(shape3)


def _combine_body(x1_ref, y0_ref, y1_ref, y2_ref, y3_ref, g_ref, gate2_ref, gf_ref, o_ref):
    g = g_ref[...]
    tm, D = x1_ref.shape
    per = D // LANES

    def rows(y_ref):
        return jnp.concatenate([y_ref[pl.ds(s, tm, stride=per), :] for s in range(per)], axis=-1)

    ff = (g[:, 0:1] * rows(y0_ref) + g[:, 1:2] * rows(y1_ref)) + (g[:, 2:3] * rows(y2_ref) + g[:, 3:4] * rows(y3_ref))
    x2 = x1_ref[...] + gate2_ref[...] * ff
    o_ref[...] = _rms(x2, gf_ref[...])


def _combine(x1, yp, gates, mod3, row0, g_final, tok0, B, L):
    n_tok, D = x1.shape
    tm = min(ROW_TILE, L)
    nl = L // tm
    blk0 = tok0 // tm
    nblk_tok = n_tok // tm
    mrow = (lambda b: 0) if row0 == 0 else (lambda b: b + row0)
    row = lambda b, i: (blk0 + b * nl + i, 0)
    yspec = lambda k: pl.BlockSpec((tm * (D // LANES), LANES), lambda b, i: (k * nblk_tok + blk0 + b * nl + i, 0))
    return pl.pallas_call(
        _combine_body,
        grid=(B, nl),
        in_specs=[pl.BlockSpec((tm, D), row), yspec(0), yspec(1), yspec(2), yspec(3),
                  pl.BlockSpec((tm, TOP_K), row),
                  pl.BlockSpec((None, 1, D), lambda b, i: (mrow(b), 0, 5)),
                  pl.BlockSpec((1, D), lambda b, i: (0, 0))],
        out_specs=pl.BlockSpec((None, tm, D), lambda b, i: (b, i, 0)),
        out_shape=jax.ShapeDtypeStruct((B, L, D), F32),
        name="combine",
        compiler_params=_params("arbitrary", "arbitrary"),
    )(x1, yp, yp, yp, yp, gates, mod3, g_final)


def _filter_body(tw_ref, bands_ref, w1_ref, b1_ref, fr_ref, w2_ref, b2_ref, w3_ref, dl_ref, k_ref, ss_ref):
    hi = functools.partial(jnp.dot, precision=HIGHEST, preferred_element_type=F32)
    t = tw_ref[:, 0:1]
    w = tw_ref[:, 1:2]
    lane = lax.broadcasted_iota(jnp.int32, (t.shape[0], LANES), 1)
    fw = w * bands_ref[...]
    z = jnp.where(lane == 0, t,
                  jnp.where(lane <= FILTER_BANDS, jnp.cos(fw),
                            jnp.where(lane <= 2 * FILTER_BANDS, -jnp.sin(fw), 0.0)))
    fr = fr_ref[...]
    h = jnp.sin(fr * (hi(z, w1_ref[...]) + b1_ref[...]))
    h = jnp.sin(fr * (hi(h, w2_ref[...]) + b2_ref[...]))
    decay = jnp.exp(-t * dl_ref[...])
    C = decay.shape[1]
    ss = []
    for g in range(k_ref.shape[1] // C):
        kg = hi(h, w3_ref[:, g * C:(g + 1) * C]) * decay
        k_ref[:, g * C:(g + 1) * C] = kg
        ss.append(jnp.sum(kg * kg, axis=0, keepdims=True))
    ss = jnp.concatenate(ss, axis=-1)

    @pl.when(pl.program_id(0) == 0)
    def _first():
        ss_ref[...] = ss

    @pl.when(pl.program_id(0) > 0)
    def _rest():
        ss_ref[...] += ss


def _hyena_filters(L, lp):
    C = D_HYENA
    pos = jnp.arange(L, dtype=F32)
    tw = jnp.stack([pos / max(L - 1, 1), 2 * math.pi * pos / L], axis=-1)
    bands = jnp.linspace(1e-4, FILTER_BANDS - 1, FILTER_BANDS, dtype=F32)
    bands128 = jnp.zeros((1, LANES), F32).at[0, 1:1 + 2 * FILTER_BANDS].set(jnp.concatenate([bands, bands]))
    w1 = jnp.zeros((LANES, lp['f_w1'].shape[1]), F32).at[:lp['f_w1'].shape[0]].set(lp['f_w1'])
    deltas = jnp.abs(jnp.linspace(math.log(DECAY_TARGET) / DECAY_PCT_LONG,
                                  math.log(DECAY_TARGET) / DECAY_PCT_SHORT, C, dtype=F32))[None]
    nk = lp['f_w3'].shape[1]
    tl = min(L, 256)
    args = [tw, bands128, w1, lp['f_b1'], lp['f_freq'], lp['f_w2'], lp['f_b2'], lp['f_w3'], deltas]
    const = lambda a: pl.BlockSpec(a.shape, lambda i: (0,) * a.ndim)
    k_un, ss = pl.pallas_call(
        _filter_body,
        grid=(L // tl,),
        in_specs=[pl.BlockSpec((tl, 2), lambda i: (i, 0))] + [const(a) for a in args[1:]],
        out_specs=[pl.BlockSpec((tl, nk), lambda i: (i, 0)), pl.BlockSpec((1, nk), lambda i: (0, 0))],
        out_shape=[jax.ShapeDtypeStruct((L, nk), F32), jax.ShapeDtypeStruct((1, nk), F32)],
        name="hyena_filters",
        compiler_params=_params("arbitrary"),
    )(*args)
    ss = ss.reshape(HYENA_ORDER, 2, C)
    scale = lax.rsqrt(ss[:, 0] + ss[:, 1] + EPS).reshape(1, HYENA_ORDER * C)
    return k_un, scale


def _short_conv_chunk(u_ref, r0, n, prev_last, next_first, w, b):
    u = u_ref[pl.ds(r0, n), :]
    row = lax.broadcasted_iota(jnp.int32, u.shape, 0)
    up = jnp.where(row == 0, prev_last, pltpu.roll(u, 1, axis=0))
    un = jnp.where(row == n - 1, next_first, pltpu.roll(u, n - 1, axis=0))
    return up * w[0:1] + u * w[1:2] + un * w[2:3] + b


def _dft_spectrum_body(a_ref, b_ref, s_ref, o_ref):
    o_ref[...] = jnp.dot(a_ref[...], b_ref[...], precision=HIGHEST, preferred_element_type=F32) * s_ref[...]


def _dft_spectrum(a, b, scale):
    M, K = a.shape
    ncol = b.shape[1]
    tn = min(ncol, 512)
    return pl.pallas_call(
        _dft_spectrum_body,
        grid=(ncol // tn,),
        in_specs=[pl.BlockSpec((M, K), lambda j: (0, 0)), pl.BlockSpec((K, tn), lambda j: (0, j)),
                  pl.BlockSpec((1, tn), lambda j: (0, j))],
        out_specs=pl.BlockSpec((M, tn), lambda j: (0, j)),
        out_shape=jax.ShapeDtypeStruct((M, ncol), F32),
        name="dft_spectrum",
        compiler_params=_params("arbitrary"),
    )(a, b, scale)


def _hyena_short_body(v_ref, x1_ref, x2_ref, wv_ref, w1_ref, w2_ref, bv_ref, b1_ref, b2_ref, fb_ref, h0_ref, h1_ref,
                      ff_ref, if_ref, o_ref):
    L = v_ref.shape[0]
    N = ff_ref.shape[0] // 2
    conv = lambda u_ref, w_ref, b_ref: _short_conv_chunk(u_ref, 0, L, 0.0, 0.0, w_ref[...], b_ref[...])
    z = conv(v_ref, wv_ref, bv_ref)
    gates = (conv(x1_ref, w1_ref, b1_ref), conv(x2_ref, w2_ref, b2_ref))
    for o, (gate, h_ref) in enumerate(zip(gates, (h0_ref, h1_ref))):
        X = _dot(ff_ref[...], z.astype(BF16))
        xr, xi = X[:N], X[N:]
        hr, hi = h_ref[:N, :], h_ref[N:, :]
        Y = jnp.concatenate([xr * hr - xi * hi, xr * hi + xi * hr], axis=0)
        y = _dot(if_ref[...], Y.astype(BF16))
        z = gate * (y + fb_ref[o:o + 1, :] * z)
    o_ref[...] = z


def _hyena_short(hy, lp, k_un, scale):
    B, L, _ = hy.shape
    C = D_HYENA
    N = 2 * L
    ct = 256
    ncb = C // ct
    k4 = k_un.reshape(L, HYENA_ORDER, 2, C)
    taps = jnp.concatenate([k4[:, :, 0], jnp.zeros((1, HYENA_ORDER, C), F32), k4[:0:-1, :, 1]], axis=0)
    ang = 2 * np.pi * np.outer(np.arange(N), np.arange(N)) / N
    dft = jnp.asarray(np.concatenate([np.cos(ang), -np.sin(ang)], axis=0), F32)
    H = _dft_spectrum(dft, taps.reshape(N, HYENA_ORDER * C), scale)
    fwd = jnp.asarray(np.concatenate([np.cos(ang[:, :L]), -np.sin(ang[:, :L])], axis=0), BF16)
    inv = jnp.asarray(np.concatenate([np.cos(ang[:L]), -np.sin(ang[:L])], axis=1) / N, BF16)
    u = lambda g: pl.BlockSpec((None, L, ct), lambda c, b: (b, 0, g * ncb + c))
    cw = lambda g: pl.BlockSpec((3, ct), lambda c, b: (0, g * ncb + c))
    cb = lambda g: pl.BlockSpec((1, ct), lambda c, b: (0, g * ncb + c))
    const = lambda a: pl.BlockSpec(a.shape, lambda c, b: (0,) * a.ndim)
    return pl.pallas_call(
        _hyena_short_body,
        grid=(ncb, B),
        in_specs=[u(0), u(1), u(2), cw(0), cw(1), cw(2), cb(0), cb(1), cb(2),
                  pl.BlockSpec((HYENA_ORDER, ct), lambda c, b: (0, c)),
                  pl.BlockSpec((2 * N, ct), lambda c, b: (0, c)),
                  pl.BlockSpec((2 * N, ct), lambda c, b: (0, ncb + c)),
                  const(fwd), const(inv)],
        out_specs=pl.BlockSpec((None, L, ct), lambda c, b: (b, 0, c)),
        out_shape=jax.ShapeDtypeStruct((B, L, C), F32),
        name="hyena_short",
        compiler_params=_params("arbitrary", "arbitrary"),
    )(hy, hy, hy, lp['conv_w'], lp['conv_w'], lp['conv_w'], lp['conv_b'], lp['conv_b'], lp['conv_b'],
      lp['f_bias'], H, H, fwd, inv)


FFT_N1 = 128


def _fft_dims(L):
    N = 2 * L
    N1 = FFT_N1
    N2 = N // N1
    assert N1 * N2 == N and N2 % SUBLANES == 0
    return N, N1, N2, N2 + SUBLANES, 2 * N1 + SUBLANES


def _fft_tables(L):
    N, N1, N2, _, _ = _fft_dims(L)
    NH = N1 // 2
    n1 = np.arange(NH)
    k1 = np.arange(N1)
    n2 = np.arange(N2)
    th = 2 * np.pi * (n1[None, None, :] * k1[None, :, None] / N1 + n2[:, None, None] * k1[None, :, None] / N)
    g = np.concatenate([np.cos(th), -np.sin(th)], axis=1).reshape(N2 * 2 * N1, NH)
    ig = np.concatenate([np.cos(th), -np.sin(th)], axis=1).transpose(0, 2, 1) / N
    ig = ig.reshape(N2 * NH, 2 * N1)
    ph = 2 * np.pi * np.outer(n2, n2) / N2
    c, s = np.cos(ph), np.sin(ph)
    f2 = np.block([[c, s], [-s, c]])
    if2 = np.block([[c, -s], [s, c]])

    def hilo(a):
        hi = jnp.asarray(a, F32).astype(BF16)
        lo = (jnp.asarray(a, F32) - hi.astype(F32)).astype(BF16)
        return hi, lo

    return {'g': hilo(g), 'ig': hilo(ig), 'f2': hilo(f2), 'if2': hilo(if2)}


def _mm(tab, r0, nrows, x, passes):
    a_hi = tab[0][pl.ds(r0, nrows), :]
    if passes == 1:
        return _dot(a_hi, x.astype(BF16))
    x_hi, x_lo = _split_bf16(x)
    a_lo = tab[1][pl.ds(r0, nrows), :]
    return _dot(a_hi, x_hi) + (_dot(a_hi, x_lo) + _dot(a_lo, x_hi))


FFT_BATCH = 4


def _fft_stage1(tbuf, sbuf, g, dims, passes):
    _, N1, N2, P, Q = dims

    def body(i, carry):
        n2s = [i * FFT_BATCH + j for j in range(FFT_BATCH)]
        xs = [tbuf[pl.ds(n2, N1 // 2, stride=P), :] for n2 in n2s]
        outs = [_mm(g, pl.multiple_of(n2 * 2 * N1, 2 * N1), 2 * N1, x, passes) for n2, x in zip(n2s, xs)]
        for n2, out in zip(n2s, outs):
            sbuf[pl.ds(pl.multiple_of(n2 * Q, SUBLANES), 2 * N1), :] = out
        return carry

    lax.fori_loop(0, N2 // FFT_BATCH, body, 0)


def _fft_stage2_load(sbuf, k1, dims):
    _, N1, N2, _, Q = dims
    re = [sbuf[pl.ds(k1 + j, N2, stride=Q), :] for j in range(FFT_BATCH)]
    im = [sbuf[pl.ds(N1 + k1 + j, N2, stride=Q), :] for j in range(FFT_BATCH)]
    return jnp.concatenate([jnp.concatenate(re, axis=1), jnp.concatenate(im, axis=1)], axis=0)


def _hyena_long_body(v_ref, x1_ref, x2_ref, wv_ref, w1_ref, w2_ref, bv_ref, b1_ref, b2_ref, fb_ref, h0_ref, h1_ref,
                     g_ref, ig_ref, f2_ref, if2_ref, o_ref, zbuf, g1buf, g2buf, sbuf, *, dims):
    N, N1, N2, P, Q = dims
    NH = N1 // 2
    zero = jnp.zeros((1, LANES), F32)

    for u_ref, w_ref, b_ref, buf in ((v_ref, wv_ref, bv_ref, zbuf), (x1_ref, w1_ref, b1_ref, g1buf),
                                     (x2_ref, w2_ref, b2_ref, g2buf)):
        w, b = w_ref[...], b_ref[...]
        for n1 in range(NH):
            r0 = n1 * N2
            prev_last = zero if n1 == 0 else u_ref[r0 - 1:r0, :]
            next_first = zero if n1 == NH - 1 else u_ref[r0 + N2:r0 + N2 + 1, :]
            buf[n1 * P:n1 * P + N2, :] = _short_conv_chunk(u_ref, r0, N2, prev_last, next_first, w, b)

    for o, (gbuf, h_ref) in enumerate(((g1buf, h0_ref), (g2buf, h1_ref))):
        _fft_stage1(zbuf, sbuf, (g_ref, None), dims, 1)

        def per_k1(i, carry):
            k1 = i * FFT_BATCH
            X = _dot(f2_ref[...], _fft_stage2_load(sbuf, k1, dims).astype(BF16))
            h0 = [pl.multiple_of((k1 + j) * 2 * N2, 2 * N2) for j in range(FFT_BATCH)]
            hr = jnp.concatenate([h_ref[pl.ds(r, N2), :] for r in h0], axis=1)
            hi = jnp.concatenate([h_ref[pl.ds(r + N2, N2), :] for r in h0], axis=1)
            xr, xi = X[:N2], X[N2:]
            Y = jnp.concatenate([xr * hr - xi * hi, xr * hi + xi * hr], axis=0)
            Cc = _dot(if2_ref[...], Y.astype(BF16))
            for j in range(FFT_BATCH):
                lanes = slice(j * LANES, (j + 1) * LANES)
                sbuf[pl.ds(k1 + j, N2, stride=Q), :] = Cc[:N2, lanes]
                sbuf[pl.ds(N1 + k1 + j, N2, stride=Q), :] = Cc[N2:, lanes]
            return carry

        lax.fori_loop(0, N1 // FFT_BATCH, per_k1, 0)
        fb = fb_ref[o:o + 1, :]

        def per_n2(i, carry):
            n2s = [i * FFT_BATCH + j for j in range(FFT_BATCH)]
            Ds = [sbuf[pl.ds(pl.multiple_of(n2 * Q, SUBLANES), 2 * N1), :].astype(BF16) for n2 in n2s]
            ys = [_dot(ig_ref[pl.ds(pl.multiple_of(n2 * NH, NH), NH), :], D) for n2, D in zip(n2s, Ds)]
            for n2, y in zip(n2s, ys):
                rows = pl.ds(n2, NH, stride=P)
                zbuf[rows, :] = gbuf[rows, :] * (y + fb * zbuf[rows, :])
            return carry

        lax.fori_loop(0, N2 // FFT_BATCH, per_n2, 0)

    for n1 in range(NH):
        o_ref[n1 * N2:(n1 + 1) * N2, :] = zbuf[n1 * P:n1 * P + N2, :]


def _fft_spectrum_body(hf_ref, hb_ref, s_ref, g_hi, g_lo, f2_hi, f2_lo, o_ref, tbuf, sbuf, *, dims):
    N, N1, N2, P, Q = dims
    NH = N1 // 2
    scale = s_ref[...]
    for d, h_ref in enumerate((hf_ref, hb_ref)):
        for n1 in range(NH):
            h = h_ref[n1 * N2:(n1 + 1) * N2, :] * scale
            if d == 1 and n1 == 0:
                h = jnp.where(lax.broadcasted_iota(jnp.int32, h.shape, 0) == 0, 0.0, h)
            tbuf[n1 * P:n1 * P + N2, :] = h
        _fft_stage1(tbuf, sbuf, (g_hi, g_lo), dims, 3)

        def per_k1(i, carry):
            k1 = i * FFT_BATCH
            X = _mm((f2_hi, f2_lo), 0, 2 * N2, _fft_stage2_load(sbuf, k1, dims), 3)
            for j in range(FFT_BATCH):
                h0 = pl.multiple_of((k1 + j) * 2 * N2, 2 * N2)
                Xj = X[:, j * LANES:(j + 1) * LANES]
                if d == 0:
                    o_ref[pl.ds(h0, 2 * N2), :] = Xj
                else:
                    o_ref[pl.ds(h0, N2), :] += Xj[:N2]
                    o_ref[pl.ds(h0 + N2, N2), :] -= Xj[N2:]
            return carry

        lax.fori_loop(0, N1 // FFT_BATCH, per_k1, 0)


def _hyena_long(hy, lp, k_un, scale):
    B, L, _ = hy.shape
    C = D_HYENA
    dims = _fft_dims(L)
    N, N1, N2, P, Q = dims
    NH = N1 // 2
    tabs = _fft_tables(L)
    ncb = C // LANES
    nspec = HYENA_ORDER * ncb
    const1 = lambda a, n: pl.BlockSpec(a.shape, (lambda *i: (0,) * a.ndim), pipeline_mode=pl.Buffered(1))
    hcol = lambda d: pl.BlockSpec((L, LANES), lambda j: (0, (j // ncb) * 2 * ncb + d * ncb + j % ncb))
    H = pl.pallas_call(
        functools.partial(_fft_spectrum_body, dims=dims),
        grid=(nspec,),
        in_specs=[hcol(0), hcol(1), pl.BlockSpec((1, LANES), lambda j: (0, j)),
                  const1(tabs['g'][0], 1), const1(tabs['g'][1], 1),
                  const1(tabs['f2'][0], 1), const1(tabs['f2'][1], 1)],
        out_specs=pl.BlockSpec((2 * N, LANES), lambda j: (0, j)),
        out_shape=jax.ShapeDtypeStruct((2 * N, HYENA_ORDER * C), F32),
        scratch_shapes=[pltpu.VMEM((NH * P, LANES), F32), pltpu.VMEM((N2 * Q, LANES), F32)],
        name="filter_spectrum",
        compiler_params=_params("arbitrary"),
    )(k_un, k_un, scale, tabs['g'][0], tabs['g'][1], tabs['f2'][0], tabs['f2'][1])

    one = pl.Buffered(1)
    hspec = lambda o: pl.BlockSpec((2 * N, LANES), lambda c, b: (0, o * ncb + c), pipeline_mode=one)
    u = lambda g: pl.BlockSpec((None, L, LANES), lambda c, b: (b, 0, g * ncb + c), pipeline_mode=one)
    cw = lambda g: pl.BlockSpec((3, LANES), lambda c, b: (0, g * ncb + c))
    cb = lambda g: pl.BlockSpec((1, LANES), lambda c, b: (0, g * ncb + c))
    return pl.pallas_call(
        functools.partial(_hyena_long_body, dims=dims),
        grid=(ncb, B),
        in_specs=[u(0), u(1), u(2), cw(0), cw(1), cw(2), cb(0), cb(1), cb(2),
                  pl.BlockSpec((HYENA_ORDER, LANES), lambda c, b: (0, c)),
                  hspec(0), hspec(1),
                  const1(tabs['g'][0], 2), const1(tabs['ig'][0], 2),
                  const1(tabs['f2'][0], 2), const1(tabs['if2'][0], 2)],
        out_specs=pl.BlockSpec((None, L, LANES), lambda c, b: (b, 0, c)),
        out_shape=jax.ShapeDtypeStruct((B, L, C), F32),
        scratch_shapes=[pltpu.VMEM((NH * P, LANES), F32), pltpu.VMEM((NH * P, LANES), F32),
                        pltpu.VMEM((NH * P, LANES), F32), pltpu.VMEM((N2 * Q, LANES), F32)],
        name="hyena_long",
        compiler_params=_params("arbitrary", "arbitrary"),
    )(hy, hy, hy, lp['conv_w'], lp['conv_w'], lp['conv_w'], lp['conv_b'], lp['conv_b'], lp['conv_b'],
      lp['f_bias'], H, H, tabs['g'][0], tabs['ig'][0], tabs['f2'][0], tabs['if2'][0])


def _hyena(hy, lp):
    L = hy.shape[1]
    k_un, scale = _hyena_filters(L, lp)
    if 2 * L >= 2 * FFT_N1 * SUBLANES and (2 * L) % (FFT_N1 * SUBLANES) == 0:
        return _hyena_long(hy, lp, k_un, scale)
    return _hyena_short(hy, lp, k_un, scale)


def kernel(x_prompt, x_sample, cache_k, cache_v, c, c_ctx, w_mod, b_mod, g_mix, w_in, conv_w, conv_b, f_w1, f_b1,
           f_freq, f_w2, f_b2, f_w3, f_bias, rpb, g_out_hy, g_out_at, w_out, g_ffn, w_router, b_router, w_exp1,
           b_exp1, w_exp2, b_exp2, g_final):
    depth = w_mod.shape[0]
    Bp, Lp, D = x_prompt.shape
    Bs, Ls, _ = x_sample.shape
    n_ctx, n_lat = Bp * Lp, Bs * Ls
    n_tok = n_ctx + n_lat
    d_attn = N_HEADS * HEAD_DIM
    q_off = 3 * D_HYENA
    row2 = lambda a: a.reshape(1, -1)

    cond = jnp.zeros((SUBLANES, D), F32).at[0].set(c_ctx).at[1:1 + Bs].set(c)
    xp, xs = x_prompt, x_sample
    new_k, new_v = [], []
    for l in range(depth):
        wr = jnp.zeros((D, LANES), F32).at[:, :N_EXPERTS].set(w_router[l])
        wr_hi = wr.astype(BF16)
        lp = {
            'conv_w': conv_w[l], 'conv_b': row2(conv_b[l]), 'f_w1': f_w1[l], 'f_b1': row2(f_b1[l]),
            'f_freq': row2(f_freq[l]), 'f_w2': f_w2[l], 'f_b2': row2(f_b2[l]), 'f_w3': f_w3[l],
            'f_bias': f_bias[l], 'g_out_hy': row2(g_out_hy[l]), 'g_out_at': row2(g_out_at[l]),
            'w_out': w_out[l].astype(BF16), 'g_ffn': row2(g_ffn[l]),
            'wr_hi': wr_hi, 'wr_lo': (wr - wr_hi.astype(F32)).astype(BF16),
            'b_router': jnp.zeros((1, LANES), F32).at[0, :N_EXPERTS].set(b_router[l]),
            'w_exp1': w_exp1[l], 'b_exp1': b_exp1[l][:, None, :], 'w_exp2': w_exp2[l],
            'b_exp2': b_exp2[l][:, None, :],
        }
        mod = _modulation(cond, w_mod[l], row2(b_mod[l]))
        mod3 = mod.reshape(SUBLANES, 1, 6 * D)
        w_in_b = w_in[l].astype(BF16)
        g_mix_l = row2(g_mix[l])

        proj_p = _in_projection(xp, mod3, 0, g_mix_l, w_in_b)
        proj_s = _in_projection(xs, mod3, 1, g_mix_l, w_in_b)
        kv = proj_p[..., q_off + d_attn:].reshape(Bp, Lp, 2, N_HEADS, HEAD_DIM).transpose(2, 0, 3, 1, 4)
        new_k.append(kv[0])
        new_v.append(kv[1])

        hy_p = _hyena(proj_p, lp)
        hy_s = _hyena(proj_s, lp)
        at_p = _context_attention(proj_p, q_off)
        heads_last = lambda t: t.transpose(0, 2, 1, 3).reshape(Bs, t.shape[2], d_attn)
        at_s = _neighbourhood_attention(proj_s, q_off, heads_last(cache_k[:, l]), heads_last(cache_v[:, l]), rpb[l])

        x1, h2u, logits = _out_projection((hy_p, at_p, xp), (hy_s, at_s, xs), mod3, lp)

        gates, ce, nv, n_used, gidx, sidx = _routing(logits[:, :N_EXPERTS], n_tok)
        yp = _experts(ce, nv, n_used, gidx, sidx, h2u, lp, n_tok * TOP_K)
        last = l == depth - 1
        gf = row2(g_final) if last else None
        assert last, "deeper stacks need the un-normalised residual between layers"
        xp = _combine(x1, yp, gates, mod3, 0, gf, 0, Bp, Lp)
        xs = _combine(x1, yp, gates, mod3, 1, gf, n_ctx, Bs, Ls)

    return xp, xs, jnp.stack(new_k, axis=1), jnp.stack(new_v, axis=1)
```

```python
import functools
import math

import numpy as np
import jax
import jax.numpy as jnp
from jax import lax
from jax.experimental import pallas as pl
from jax.experimental.pallas import tpu as pltpu

F32 = jnp.float32
BF16 = jnp.bfloat16
HIGHEST = lax.Precision.HIGHEST

GRID_W = 64
N_HEADS = 16
HEAD_DIM = 64
D_HYENA = 1024
HYENA_ORDER = 2
FILTER_BANDS = 16
DECAY_TARGET = 1e-2
DECAY_PCT_SHORT = 0.3
DECAY_PCT_LONG = 1.5
WIN_H = 8
WIN_W = 16
N_EXPERTS = 32
TOP_K = 4
SWIGLU_ALPHA = 1.702
SWIGLU_LIMIT = 7.0
EPS = 1e-6
NEG_INF = -1e30

LANES = 128
SUBLANES = 8
VMEM_LIMIT = 56 * 1024 * 1024

ROW_TILE = 256
NA_Q_ROWS = 4
MOE_CHUNK = 1024
MOE_SUB = 256
MOE_FF_TILE = 256
MOE_ISSUE = 32


def _params(*sem):
    return pltpu.CompilerParams(dimension_semantics=sem, vmem_limit_bytes=VMEM_LIMIT)


def _rms(x, g):
    return x * lax.rsqrt(jnp.mean(x * x, axis=-1, keepdims=True) + EPS) * g


def _split_bf16(x):
    hi = x.astype(BF16)
    lo = (x - hi.astype(F32)).astype(BF16)
    return hi, lo


def _dot(a, b):
    return jnp.dot(a, b, preferred_element_type=F32)


def _dot_nt(a, b):
    return lax.dot_general(a, b, (((1,), (1,)), ((), ())), preferred_element_type=F32)


def _mod_body(c_ref, w_ref, b_ref, o_ref):
    c = c_ref[...]
    s = c / (1.0 + jnp.exp(-c))
    o_ref[...] = jnp.dot(s, w_ref[...], precision=HIGHEST, preferred_element_type=F32) + b_ref[...]


def _modulation(cc, w, b):
    D, N = w.shape
    tn = min(N, 1536)
    return pl.pallas_call(
        _mod_body,
        grid=(N // tn,),
        in_specs=[pl.BlockSpec((SUBLANES, D), lambda j: (0, 0)),
                  pl.BlockSpec((D, tn), lambda j: (0, j)),
                  pl.BlockSpec((1, tn), lambda j: (0, j))],
        out_specs=pl.BlockSpec((SUBLANES, tn), lambda j: (0, j)),
        out_shape=jax.ShapeDtypeStruct((SUBLANES, N), F32),
        name="modulation",
        compiler_params=_params("arbitrary"),
    )(cc, w, b)


def _inproj_body(x_ref, shift_ref, scale_ref, g_ref, w_ref, o_ref, *, n_chunk):
    h = _rms(x_ref[...], g_ref[...]) * (1.0 + scale_ref[...]) + shift_ref[...]
    hb = h.astype(BF16)
    n_out = o_ref.shape[-1]

    def col(j, carry):
        c0 = pl.multiple_of(j * n_chunk, n_chunk)
        o_ref[:, pl.ds(c0, n_chunk)] = _dot(hb, w_ref[:, pl.ds(c0, n_chunk)])
        return carry

    lax.fori_loop(0, n_out // n_chunk, col, 0)


def _in_projection(x, mod3, row0, g, w_bf16):
    B, L, D = x.shape
    N = w_bf16.shape[1]
    tm = min(ROW_TILE, L)
    mrow = (lambda b: 0) if row0 == 0 else (lambda b: b + row0)
    return pl.pallas_call(
        functools.partial(_inproj_body, n_chunk=512),
        grid=(B, L // tm),
        in_specs=[pl.BlockSpec((None, tm, D), lambda b, i: (b, i, 0)),
                  pl.BlockSpec((None, 1, D), lambda b, i: (mrow(b), 0, 0)),
                  pl.BlockSpec((None, 1, D), lambda b, i: (mrow(b), 0, 1)),
                  pl.BlockSpec((1, D), lambda b, i: (0, 0)),
                  pl.BlockSpec((D, N), lambda b, i: (0, 0), pipeline_mode=pl.Buffered(1))],
        out_specs=pl.BlockSpec((None, tm, N), lambda b, i: (b, i, 0)),
        out_shape=jax.ShapeDtypeStruct((B, L, N), F32),
        name="in_projection",
        compiler_params=_params("arbitrary", "arbitrary"),
    )(x, mod3, mod3, g, w_bf16)


def _head_lane_masks(rows, hd):
    lane = lax.broadcasted_iota(jnp.int32, (rows, LANES), 1)
    return [(lane >= h * hd) & (lane < (h + 1) * hd) for h in range(LANES // hd)]


def _ctx_attn_body(q_ref, k_ref, v_ref, o_ref, *, scale, hd):
    q = q_ref[...]
    k = k_ref[...].astype(BF16)
    v = v_ref[...].astype(BF16)
    masks = _head_lane_masks(q.shape[0], hd)
    out = None
    for msk in masks:
        qh = jnp.where(msk, q, 0.0).astype(BF16)
        s = _dot_nt(qh, k) * scale
        p = jnp.exp(s - jnp.max(s, axis=-1, keepdims=True))
        o = _dot(p.astype(BF16), v) / jnp.sum(p, axis=-1, keepdims=True)
        out = o if out is None else jnp.where(msk, o, out)
    o_ref[...] = out


def _context_attention(proj, q_off):
    B, L, _ = proj.shape
    d_attn = N_HEADS * HEAD_DIM
    nhp = d_attn // LANES
    qb, kb, vb = (q_off // LANES, (q_off + d_attn) // LANES, (q_off + 2 * d_attn) // LANES)
    spec = lambda base: pl.BlockSpec((None, L, LANES), lambda b, h: (b, 0, base + h))
    return pl.pallas_call(
        functools.partial(_ctx_attn_body, scale=1.0 / math.sqrt(HEAD_DIM), hd=HEAD_DIM),
        grid=(B, nhp),
        in_specs=[spec(qb), spec(kb), spec(vb)],
        out_specs=pl.BlockSpec((None, L, LANES), lambda b, h: (b, 0, h)),
        out_shape=jax.ShapeDtypeStruct((B, L, d_attn), F32),
        name="context_attention",
        compiler_params=_params("arbitrary", "arbitrary"),
    )(proj, proj, proj)


def _na_tables(rows, rpb):
    W = GRID_W
    kh = min(WIN_H, rows)
    rbq = NA_Q_ROWS
    kwr = rbq + kh
    assert rows % rbq == 0 and kwr <= rows
    nblk = rows // rbq
    ws = np.clip(np.arange(nblk) * rbq - kh // 2, 0, rows - kwr)
    rq = np.arange(nblk)[:, None, None] * rbq + np.arange(rbq)[None, :, None]
    rk = ws[:, None, None] + np.arange(kwr)[None, None, :]
    rs = np.clip(rq - kh // 2, 0, rows - kh)
    vr = (rk >= rs) & (rk < rs + kh)
    dr = np.where(vr, rk - rq + WIN_H - 1, 0)
    patterns = np.concatenate([dr.reshape(nblk, -1), vr.reshape(nblk, -1)], axis=1)
    _, first, btype = np.unique(patterns, axis=0, return_index=True, return_inverse=True)
    dr_t, vr_t = dr[first], vr[first]
    cq = np.arange(W)[:, None]
    ck = np.arange(W)[None, :]
    cs = np.clip(cq - WIN_W // 2, 0, W - WIN_W)
    vc = (ck >= cs) & (ck < cs + WIN_W)
    dc = np.clip(ck - cq, -(WIN_W - 1), WIN_W - 1) + WIN_W - 1
    nt = dr_t.shape[0]
    rowsel = (dr_t[..., None] == np.arange(2 * WIN_H - 1)) & vr_t[..., None]
    colsel = (dc[..., None] == np.arange(2 * WIN_W - 1)) & vc[..., None]
    vals = jnp.einsum('tajr,hrc,qkc->htaqjk', rowsel.astype(np.float32), rpb.astype(F32),
                      colsel.astype(np.float32), precision=HIGHEST)
    valid = vr_t[:, :, None, :, None] & vc[None, None, :, None, :]
    bias = jnp.where(valid[None], vals, NEG_INF)
    bias = bias.reshape(rpb.shape[0], nt, rbq * W, kwr * W)
    return ws.astype(np.int32), btype.reshape(-1).astype(np.int32), bias


def _na_body(ws_ref, bt_ref, q_ref, k_ref, v_ref, kc_ref, vc_ref, bias_ref, o_ref, *, scale, hd, nk):
    i = pl.program_id(2)
    start = pl.multiple_of(ws_ref[i] * GRID_W, GRID_W)
    q = q_ref[...]
    kl = k_ref[pl.ds(start, nk), :].astype(BF16)
    vl = v_ref[pl.ds(start, nk), :].astype(BF16)
    kc = kc_ref[...].astype(BF16)
    vc = vc_ref[...].astype(BF16)
    masks = _head_lane_masks(q.shape[0], hd)
    out = None
    for h, msk in enumerate(masks):
        qh = jnp.where(msk, q, 0.0).astype(BF16)
        sl = _dot_nt(qh, kl) * scale + bias_ref[h]
        sc = _dot_nt(qh, kc) * scale
        m = jnp.maximum(jnp.max(sl, axis=-1, keepdims=True), jnp.max(sc, axis=-1, keepdims=True))
        p_l = jnp.exp(sl - m)
        p_c = jnp.exp(sc - m)
        den = jnp.sum(p_l, axis=-1, keepdims=True) + jnp.sum(p_c, axis=-1, keepdims=True)
        o = (_dot(p_l.astype(BF16), vl) + _dot(p_c.astype(BF16), vc)) / den
        out = o if out is None else jnp.where(msk, o, out)
    o_ref[...] = out


def _neighbourhood_attention(proj, q_off, kc, vc, rpb):
    B, L, _ = proj.shape
    Lc = kc.shape[1]
    d_attn = N_HEADS * HEAD_DIM
    hpb = LANES // HEAD_DIM
    nhp = d_attn // LANES
    rows = L // GRID_W
    ws, btype, bias = _na_tables(rows, rpb)
    nq = NA_Q_ROWS * GRID_W
    nk = (NA_Q_ROWS + min(WIN_H, rows)) * GRID_W
    nblk = rows // NA_Q_ROWS
    qb, kb, vb = (q_off // LANES, (q_off + d_attn) // LANES, (q_off + 2 * d_attn) // LANES)
    full = lambda base: pl.BlockSpec((None, L, LANES), lambda b, h, i, ws_r, bt_r: (b, 0, base + h))
    ctx = pl.BlockSpec((None, Lc, LANES), lambda b, h, i, ws_r, bt_r: (b, 0, h))
    grid_spec = pltpu.PrefetchScalarGridSpec(
        num_scalar_prefetch=2,
        grid=(B, nhp, nblk),
        in_specs=[pl.BlockSpec((None, nq, LANES), lambda b, h, i, ws_r, bt_r: (b, i, qb + h)),
                  full(kb), full(vb), ctx, ctx,
                  pl.BlockSpec((hpb, None, nq, nk), lambda b, h, i, ws_r, bt_r: (h, bt_r[i], 0, 0))],
        out_specs=pl.BlockSpec((None, nq, LANES), lambda b, h, i, ws_r, bt_r: (b, i, h)),
    )
    return pl.pallas_call(
        functools.partial(_na_body, scale=1.0 / math.sqrt(HEAD_DIM), hd=HEAD_DIM, nk=nk),
        grid_spec=grid_spec,
        out_shape=jax.ShapeDtypeStruct((B, L, d_attn), F32),
        name="neighbourhood_attention",
        compiler_params=_params("arbitrary", "arbitrary", "arbitrary"),
    )(jnp.asarray(ws), jnp.asarray(btype), proj, proj, proj, kc, vc, bias)


def _outproj_body(yhp_ref, yap_ref, xp_ref, yhs_ref, yas_ref, xs_ref, gate1_ref, shift2_ref, scale2_ref, ghy_ref,
                  gat_ref, w_ref, gffn_ref, wrh_ref, wrl_ref, br_ref, x1_ref, h2u_ref, lg_ref, *, n_ctx_tiles):
    def run(yh_ref, ya_ref, x_ref):
        cat = jnp.concatenate([_rms(yh_ref[...], ghy_ref[...]), _rms(ya_ref[...], gat_ref[...])], axis=-1)
        mix = _dot(cat.astype(BF16), w_ref[...])
        x1 = x_ref[...] + gate1_ref[...] * mix
        x1_ref[...] = x1
        h2 = _rms(x1, gffn_ref[...]) * (1.0 + scale2_ref[...]) + shift2_ref[...]
        hi, lo = _split_bf16(h2)
        lg_ref[...] = (_dot(hi, wrh_ref[...]) + _dot(lo, wrh_ref[...]) + _dot(hi, wrl_ref[...])) + br_ref[...]
        bits = lax.bitcast_convert_type(hi.astype(F32), jnp.uint32)
        half = bits.shape[-1] // 2
        words = (bits[:, :half] >> 16) | (bits[:, half:] & jnp.uint32(0xFFFF0000))
        per = half // LANES
        for s in range(per):
            h2u_ref[pl.ds(s, words.shape[0], stride=per), :] = words[:, s * LANES:(s + 1) * LANES]

    is_ctx = pl.program_id(0) < n_ctx_tiles
    pl.when(is_ctx)(lambda: run(yhp_ref, yap_ref, xp_ref))
    pl.when(jnp.logical_not(is_ctx))(lambda: run(yhs_ref, yas_ref, xs_ref))


def _out_projection(ctx, lat, mod3, lp):
    Bp, Lp, D = ctx[2].shape
    Bs, Ls, _ = lat[2].shape
    dh, da = ctx[0].shape[-1], ctx[1].shape[-1]
    tm = min(ROW_TILE, Lp, Ls)
    nct, nlt = Bp * Lp // tm, Bs * Ls // tm
    n_tok = (nct + nlt) * tm
    lat_tiles = Ls // tm
    mrow = lambda i: jnp.where(i < nct, 0, 1 + (i - nct) // lat_tiles)
    modspec = lambda c: pl.BlockSpec((None, 1, D), lambda i: (mrow(i), 0, c))
    const = lambda shape: pl.BlockSpec(shape, lambda i: (0,) * len(shape))
    crow = lambda w: pl.BlockSpec((tm, w), lambda i: (jnp.minimum(i, nct - 1), 0))
    lrow = lambda w: pl.BlockSpec((tm, w), lambda i: (jnp.maximum(i - nct, 0), 0))
    flat = lambda t: t.reshape(-1, t.shape[-1])
    return pl.pallas_call(
        functools.partial(_outproj_body, n_ctx_tiles=nct),
        grid=(nct + nlt,),
        in_specs=[crow(dh), crow(da), crow(D), lrow(dh), lrow(da), lrow(D),
                  modspec(2), modspec(3), modspec(4),
                  const((1, dh)), const((1, da)),
                  pl.BlockSpec((dh + da, D), lambda i: (0, 0), pipeline_mode=pl.Buffered(1)),
                  const((1, D)), const((D, LANES)), const((D, LANES)), const((1, LANES))],
        out_specs=[pl.BlockSpec((tm, D), lambda i: (i, 0)),
                   pl.BlockSpec((tm * (D // 2 // LANES), LANES), lambda i: (i, 0)),
                   pl.BlockSpec((tm, LANES), lambda i: (i, 0))],
        out_shape=[jax.ShapeDtypeStruct((n_tok, D), F32),
                   jax.ShapeDtypeStruct((n_tok * (D // 2 // LANES), LANES), jnp.uint32),
                   jax.ShapeDtypeStruct((n_tok, LANES), F32)],
        name="out_projection",
        compiler_params=_params("arbitrary"),
    )(*[flat(t) for t in ctx], *[flat(t) for t in lat], mod3, mod3, mod3, lp['g_out_hy'], lp['g_out_at'],
      lp['w_out'], lp['g_ffn'], lp['wr_hi'], lp['wr_lo'], lp['b_router'])


def _expert_body(ce_ref, nv_ref, nu_ref, gcur_ref, gnext_ref, sprev_ref, scur_ref, h2u_hbm, w1g_ref, w1u_ref,
                 b1g_ref, b1u_ref, w2_ref, b2_ref, yp_hbm, xu, xb, acc, ystage, abuf, gsem, ssem):
    c = pl.program_id(0)
    f = pl.program_id(1)
    nc = pl.num_programs(0)
    nf = pl.num_programs(1)
    half = xb.shape[1] // 2
    xt = half // LANES
    yt = acc.shape[1] // LANES
    slot = c % 2
    subs = lambda n: (n + MOE_SUB - 1) // MOE_SUB
    nv = nv_ref[c]
    nsub = subs(nv)
    nrows = nsub * MOE_SUB
    last = c == nu_ref[0] - 1
    nv_prev = nv_ref[jnp.maximum(c - 1, 0)]
    nrows_prev = subs(nv_prev) * MOE_SUB
    covered_prev = MOE_ISSUE * nf * subs(nv_prev)
    nrows_next = jnp.where(c + 1 < nc, subs(nv_ref[jnp.minimum(c + 1, nc - 1)]), 0) * MOE_SUB
    covered = MOE_ISSUE * nf * nsub

    def gather_row(idx_ref, r, to_slot):
        tok = idx_ref[0, r]
        pltpu.make_async_copy(h2u_hbm.at[pl.ds(pl.multiple_of(tok * xt, xt), xt), :],
                              xu.at[to_slot, pl.ds(pl.multiple_of(r * xt, xt), xt), :], gsem.at[to_slot]).start()

    def scatter_row(dst, r):
        pltpu.make_async_copy(ystage.at[pl.ds(pl.multiple_of(r * yt, yt), yt), :],
                              yp_hbm.at[pl.ds(pl.multiple_of(dst * yt, yt), yt), :], ssem).start()

    def for_rows(lo, hi, fn):
        lax.fori_loop(lo, hi, lambda r, carry: (fn(r), carry)[1], 0)

    def drain(n, src, dst, sem, per):
        piece = lambda m: pltpu.make_async_copy(src.at[pl.ds(0, m * per), :], dst.at[pl.ds(0, m * per), :], sem)
        for_rows(0, n // MOE_SUB, lambda i: piece(MOE_SUB).wait())
        bit = MOE_SUB // 2
        while bit:
            if not isinstance(n, int):
                pl.when((n & bit) != 0)(piece(bit).wait)
            elif n & bit:
                piece(bit).wait()
            bit //= 2

    @pl.when(c < nu_ref[0])
    def _chunk():
        @pl.when(f == 0)
        def _arrive():
            @pl.when(c == 0)
            def _first():
                for_rows(0, nrows, lambda r: gather_row(gcur_ref, r, 0))

            started = jnp.where(c == 0, nrows, jnp.maximum(covered_prev, nrows))
            drain(started, h2u_hbm, xu.at[slot], gsem.at[slot], xt)

            def unpack(sb, carry):
                r0 = pl.multiple_of(sb * MOE_SUB, MOE_SUB)
                for s in range(xt):
                    u = xu[slot, pl.ds(r0 * xt + s, MOE_SUB, stride=xt), :]
                    lo = lax.bitcast_convert_type(u << 16, F32)
                    hi = lax.bitcast_convert_type(u & jnp.uint32(0xFFFF0000), F32)
                    xb[pl.ds(r0, MOE_SUB), s * LANES:(s + 1) * LANES] = lo.astype(BF16)
                    xb[pl.ds(r0, MOE_SUB), half + s * LANES:half + (s + 1) * LANES] = hi.astype(BF16)
                acc[pl.ds(r0, MOE_SUB), :] = jnp.zeros((MOE_SUB, acc.shape[1]), F32)
                return carry

            lax.fori_loop(0, nsub, unpack, 0)

        def matmul_steps(with_scatter):
            def up_proj(sb):
                base = (f * nsub + sb) * MOE_ISSUE
                for j in range(MOE_ISSUE):
                    gather_row(gnext_ref, base + j, 1 - slot)
                if with_scatter:
                    for j in range(MOE_ISSUE):
                        scatter_row(sprev_ref[0, base + j], jnp.minimum(base + j, nrows_prev - 1))
                x = xb[pl.ds(pl.multiple_of(sb * MOE_SUB, MOE_SUB), MOE_SUB), :]
                gate = jnp.minimum(_dot(x, w1g_ref[...].astype(BF16)) + b1g_ref[...], SWIGLU_LIMIT)
                up = jnp.clip(_dot(x, w1u_ref[...].astype(BF16)) + b1u_ref[...], -SWIGLU_LIMIT, SWIGLU_LIMIT)
                glu = gate / (1.0 + jnp.exp(-SWIGLU_ALPHA * gate))
                abuf[...] = ((up + 1.0) * glu).astype(BF16)

            def down_proj(sb):
                acc[pl.ds(pl.multiple_of(sb * MOE_SUB, MOE_SUB), MOE_SUB), :] += _dot(abuf[...], w2_ref[...].astype(BF16))

            def step(sb, carry):
                down_proj(sb - 1)
                up_proj(sb)
                return carry

            up_proj(0)
            lax.fori_loop(1, nsub, step, 0)
            down_proj(nsub - 1)

        pl.when(c == 0)(lambda: matmul_steps(False))
        pl.when(c > 0)(lambda: matmul_steps(True))

        @pl.when(f == nf - 1)
        def _leave():
            for_rows(covered, nrows_next, lambda r: gather_row(gnext_ref, r, 1 - slot))

            @pl.when(c > 0)
            def _prev_out():
                for_rows(covered, nv_prev, lambda r: scatter_row(sprev_ref[0, r], r))
                drain(jnp.maximum(covered, nv_prev), yp_hbm, ystage, ssem, yt)

            def stage(sb, carry):
                r0 = pl.multiple_of(sb * MOE_SUB, MOE_SUB)
                for s in range(yt):
                    cols = slice(s * LANES, (s + 1) * LANES)
                    ystage[pl.ds(r0 * yt + s, MOE_SUB, stride=yt), :] = acc[pl.ds(r0, MOE_SUB), cols] + b2_ref[:, cols]
                return carry

            lax.fori_loop(0, nsub, stage, 0)

            @pl.when(last)
            def _flush():
                for_rows(0, nv, lambda r: scatter_row(scur_ref[0, r], r))
                spare0 = yp_hbm.shape[0] // yt - xb.shape[0]
                for_rows(0, xb.shape[0], lambda r: scatter_row(spare0 + r, jnp.minimum(r, nrows - 1)))
                drain(nv, yp_hbm, ystage, ssem, yt)
                drain(xb.shape[0], yp_hbm, ystage, ssem, yt)
                drain(jnp.maximum(covered, nrows_next), h2u_hbm, xu.at[1 - slot], gsem.at[1 - slot], xt)


def _experts(chunk_e, chunk_nv, n_used, gidx, sidx, h2u, lp, n_out_rows):
    E, D, two_ff = lp['w_exp1'].shape
    d_ff = two_ff // 2
    nc = gidx.shape[0]
    R, tf = MOE_CHUNK, MOE_FF_TILE
    nf = d_ff // tf
    assert MOE_ISSUE * nf * (R // MOE_SUB) <= R, "a chunk's matmul steps must not start more rows than a chunk holds"

    def ff(c, f, nu):
        return jnp.where(c < nu[0], f, nf - 1)

    smem = lambda step: pl.BlockSpec((None, 1, R), lambda c, f, ce, nv, nu: (jnp.clip(c + step, 0, nc - 1), 0, 0),
                                     memory_space=pltpu.SMEM)
    grid_spec = pltpu.PrefetchScalarGridSpec(
        num_scalar_prefetch=3,
        grid=(nc, nf),
        in_specs=[smem(0), smem(1), smem(-1), smem(0),
                  pl.BlockSpec(memory_space=pl.ANY),
                  pl.BlockSpec((None, D, tf), lambda c, f, ce, nv, nu: (ce[c], 0, ff(c, f, nu))),
                  pl.BlockSpec((None, D, tf), lambda c, f, ce, nv, nu: (ce[c], 0, nf + ff(c, f, nu))),
                  pl.BlockSpec((None, 1, tf), lambda c, f, ce, nv, nu: (ce[c], 0, ff(c, f, nu))),
                  pl.BlockSpec((None, 1, tf), lambda c, f, ce, nv, nu: (ce[c], 0, nf + ff(c, f, nu))),
                  pl.BlockSpec((None, tf, D), lambda c, f, ce, nv, nu: (ce[c], ff(c, f, nu), 0)),
                  pl.BlockSpec((None, 1, D), lambda c, f, ce, nv, nu: (ce[c], 0, 0))],
        out_specs=pl.BlockSpec(memory_space=pl.ANY),
        scratch_shapes=[pltpu.VMEM((2, R * (D // 2 // LANES), LANES), jnp.uint32),
                        pltpu.VMEM((R, D), BF16),
                        pltpu.VMEM((R, D), F32),
                        pltpu.VMEM((R * (D // LANES), LANES), F32),
                        pltpu.VMEM((MOE_SUB, tf), BF16),
                        pltpu.SemaphoreType.DMA((2,)),
                        pltpu.SemaphoreType.DMA(())],
    )
    return pl.pallas_call(
        _expert_body,
        grid_spec=grid_spec,
        out_shape=jax.ShapeDtypeStruct((n_out_rows * (D // LANES), LANES), F32),
        name="experts",
        compiler_params=pltpu.CompilerParams(dimension_semantics=("arbitrary", "arbitrary"),
                                             vmem_limit_bytes=VMEM_LIMIT, has_side_effects=True,
                                             disable_bounds_checks=True),
    )(chunk_e, chunk_nv, n_used, gidx, gidx, sidx, sidx, h2u, lp['w_exp1'], lp['w_exp1'], lp['b_exp1'],
      lp['b_exp1'], lp['w_exp2'], lp['b_exp2'])


def _routing(logits, n_tok):
    R = MOE_CHUNK
    top_v, top_i = lax.top_k(logits, TOP_K)
    gates = jax.nn.softmax(top_v, axis=-1)
    n_pairs = n_tok * TOP_K
    flat_e = top_i.reshape(n_pairs).astype(jnp.int32)
    experts = jnp.arange(N_EXPERTS, dtype=jnp.int32)
    counts = jnp.sum((flat_e[:, None] == experts[None]).astype(jnp.int32), axis=0)
    padded = (counts + R - 1) // R * R
    pad_end = jnp.cumsum(padded)
    pad_start = pad_end - padded
    nc = n_pairs // R + N_EXPERTS
    pair = jnp.arange(n_pairs, dtype=jnp.int32)
    tok = pair // TOP_K
    dest = (pair % TOP_K) * n_tok + tok
    fill_e = jnp.repeat(experts, R)
    fill_i = jnp.tile(jnp.arange(R, dtype=jnp.int32), N_EXPERTS)
    fill_key = jnp.where(fill_i < jnp.repeat(padded - counts, R), 2 * fill_e + 1, 2 * N_EXPERTS)
    zeros = jnp.zeros((N_EXPERTS * R,), jnp.int32)
    _, gidx, sidx = lax.sort((jnp.concatenate([2 * flat_e, fill_key]), jnp.concatenate([tok, zeros]),
                              jnp.concatenate([dest, n_pairs + fill_i])), num_keys=1)
    n_used = (pad_end[-1] // R).astype(jnp.int32)
    cstart = jnp.arange(nc, dtype=jnp.int32) * R
    ce = jnp.minimum(jnp.searchsorted(pad_end, cstart, side='right'), N_EXPERTS - 1).astype(jnp.int32)
    ce = jnp.where(jnp.arange(nc) < n_used, ce, ce[jnp.maximum(n_used - 1, 0)])
    nv = jnp.clip(counts[ce] - (cstart - pad_start[ce]), 0, R).astype(jnp.int32)
    nv = jnp.where(jnp.arange(nc) < n_used, nv, 0)
    return gates, ce, nv, n_used.reshape(1), gidx.reshape(nc, 1, R), sidx.reshape(nc, 1, R)


def _combine_body(x1_ref, y0_ref, y1_ref, y2_ref, y3_ref, g_ref, gate2_ref, gf_ref, o_ref):
    g = g_ref[...]
    tm, D = x1_ref.shape
    per = D // LANES

    def rows(y_ref):
        return jnp.concatenate([y_ref[pl.ds(s, tm, stride=per), :] for s in range(per)], axis=-1)

    ff = (g[:, 0:1] * rows(y0_ref) + g[:, 1:2] * rows(y1_ref)) + (g[:, 2:3] * rows(y2_ref) + g[:, 3:4] * rows(y3_ref))
    x2 = x1_ref[...] + gate2_ref[...] * ff
    o_ref[...] = _rms(x2, gf_ref[...])


def _combine(x1, yp, gates, mod3, row0, g_final, tok0, B, L):
    n_tok, D = x1.shape
    tm = min(ROW_TILE, L)
    nl = L // tm
    blk0 = tok0 // tm
    nblk_tok = n_tok // tm
    mrow = (lambda b: 0) if row0 == 0 else (lambda b: b + row0)
    row = lambda b, i: (blk0 + b * nl + i, 0)
    yspec = lambda k: pl.BlockSpec((tm * (D // LANES), LANES), lambda b, i: (k * nblk_tok + blk0 + b * nl + i, 0))
    return pl.pallas_call(
        _combine_body,
        grid=(B, nl),
        in_specs=[pl.BlockSpec((tm, D), row), yspec(0), yspec(1), yspec(2), yspec(3),
                  pl.BlockSpec((tm, TOP_K), row),
                  pl.BlockSpec((None, 1, D), lambda b, i: (mrow(b), 0, 5)),
                  pl.BlockSpec((1, D), lambda b, i: (0, 0))],
        out_specs=pl.BlockSpec((None, tm, D), lambda b, i: (b, i, 0)),
        out_shape=jax.ShapeDtypeStruct((B, L, D), F32),
        name="combine",
        compiler_params=_params("arbitrary", "arbitrary"),
    )(x1, yp, yp, yp, yp, gates, mod3, g_final)


def _filter_body(tw_ref, bands_ref, w1_ref, b1_ref, fr_ref, w2_ref, b2_ref, w3_ref, dl_ref, k_ref, ss_ref):
    hi = functools.partial(jnp.dot, precision=HIGHEST, preferred_element_type=F32)
    t = tw_ref[:, 0:1]
    w = tw_ref[:, 1:2]
    lane = lax.broadcasted_iota(jnp.int32, (t.shape[0], LANES), 1)
    fw = w * bands_ref[...]
    z = jnp.where(lane == 0, t,
                  jnp.where(lane <= FILTER_BANDS, jnp.cos(fw),
                            jnp.where(lane <= 2 * FILTER_BANDS, -jnp.sin(fw), 0.0)))
    fr = fr_ref[...]
    h = jnp.sin(fr * (hi(z, w1_ref[...]) + b1_ref[...]))
    h = jnp.sin(fr * (hi(h, w2_ref[...]) + b2_ref[...]))
    decay = jnp.exp(-t * dl_ref[...])
    C = decay.shape[1]
    ss = []
    for g in range(k_ref.shape[1] // C):
        kg = hi(h, w3_ref[:, g * C:(g + 1) * C]) * decay
        k_ref[:, g * C:(g + 1) * C] = kg
        ss.append(jnp.sum(kg * kg, axis=0, keepdims=True))
    ss = jnp.concatenate(ss, axis=-1)

    @pl.when(pl.program_id(0) == 0)
    def _first():
        ss_ref[...] = ss

    @pl.when(pl.program_id(0) > 0)
    def _rest():
        ss_ref[...] += ss


def _hyena_filters(L, lp):
    C = D_HYENA
    pos = jnp.arange(L, dtype=F32)
    tw = jnp.stack([pos / max(L - 1, 1), 2 * math.pi * pos / L], axis=-1)
    bands = jnp.linspace(1e-4, FILTER_BANDS - 1, FILTER_BANDS, dtype=F32)
    bands128 = jnp.zeros((1, LANES), F32).at[0, 1:1 + 2 * FILTER_BANDS].set(jnp.concatenate([bands, bands]))
    w1 = jnp.zeros((LANES, lp['f_w1'].shape[1]), F32).at[:lp['f_w1'].shape[0]].set(lp['f_w1'])
    deltas = jnp.abs(jnp.linspace(math.log(DECAY_TARGET) / DECAY_PCT_LONG,
                                  math.log(DECAY_TARGET) / DECAY_PCT_SHORT, C, dtype=F32))[None]
    nk = lp['f_w3'].shape[1]
    tl = min(L, 256)
    args = [tw, bands128, w1, lp['f_b1'], lp['f_freq'], lp['f_w2'], lp['f_b2'], lp['f_w3'], deltas]
    const = lambda a: pl.BlockSpec(a.shape, lambda i: (0,) * a.ndim)
    k_un, ss = pl.pallas_call(
        _filter_body,
        grid=(L // tl,),
        in_specs=[pl.BlockSpec((tl, 2), lambda i: (i, 0))] + [const(a) for a in args[1:]],
        out_specs=[pl.BlockSpec((tl, nk), lambda i: (i, 0)), pl.BlockSpec((1, nk), lambda i: (0, 0))],
        out_shape=[jax.ShapeDtypeStruct((L, nk), F32), jax.ShapeDtypeStruct((1, nk), F32)],
        name="hyena_filters",
        compiler_params=_params("arbitrary"),
    )(*args)
    ss = ss.reshape(HYENA_ORDER, 2, C)
    scale = lax.rsqrt(ss[:, 0] + ss[:, 1] + EPS).reshape(1, HYENA_ORDER * C)
    return k_un, scale


def _short_conv_chunk(u_ref, r0, n, prev_last, next_first, w, b):
    u = u_ref[pl.ds(r0, n), :]
    row = lax.broadcasted_iota(jnp.int32, u.shape, 0)
    up = jnp.where(row == 0, prev_last, pltpu.roll(u, 1, axis=0))
    un = jnp.where(row == n - 1, next_first, pltpu.roll(u, n - 1, axis=0))
    return up * w[0:1] + u * w[1:2] + un * w[2:3] + b


def _dft_spectrum_body(a_ref, b_ref, s_ref, o_ref):
    o_ref[...] = jnp.dot(a_ref[...], b_ref[...], precision=HIGHEST, preferred_element_type=F32) * s_ref[...]


def _dft_spectrum(a, b, scale):
    M, K = a.shape
    ncol = b.shape[1]
    tn = min(ncol, 512)
    return pl.pallas_call(
        _dft_spectrum_body,
        grid=(ncol // tn,),
        in_specs=[pl.BlockSpec((M, K), lambda j: (0, 0)), pl.BlockSpec((K, tn), lambda j: (0, j)),
                  pl.BlockSpec((1, tn), lambda j: (0, j))],
        out_specs=pl.BlockSpec((M, tn), lambda j: (0, j)),
        out_shape=jax.ShapeDtypeStruct((M, ncol), F32),
        name="dft_spectrum",
        compiler_params=_params("arbitrary"),
    )(a, b, scale)


def _hyena_short_body(v_ref, x1_ref, x2_ref, wv_ref, w1_ref, w2_ref, bv_ref, b1_ref, b2_ref, fb_ref, h0_ref, h1_ref,
                      ff_ref, if_ref, o_ref):
    L = v_ref.shape[0]
    N = ff_ref.shape[0] // 2
    conv = lambda u_ref, w_ref, b_ref: _short_conv_chunk(u_ref, 0, L, 0.0, 0.0, w_ref[...], b_ref[...])
    z = conv(v_ref, wv_ref, bv_ref)
    gates = (conv(x1_ref, w1_ref, b1_ref), conv(x2_ref, w2_ref, b2_ref))
    for o, (gate, h_ref) in enumerate(zip(gates, (h0_ref, h1_ref))):
        X = _dot(ff_ref[...], z.astype(BF16))
        xr, xi = X[:N], X[N:]
        hr, hi = h_ref[:N, :], h_ref[N:, :]
        Y = jnp.concatenate([xr * hr - xi * hi, xr * hi + xi * hr], axis=0)
        y = _dot(if_ref[...], Y.astype(BF16))
        z = gate * (y + fb_ref[o:o + 1, :] * z)
    o_ref[...] = z


def _hyena_short(hy, lp, k_un, scale):
    B, L, _ = hy.shape
    C = D_HYENA
    N = 2 * L
    ct = 256
    ncb = C // ct
    k4 = k_un.reshape(L, HYENA_ORDER, 2, C)
    taps = jnp.concatenate([k4[:, :, 0], jnp.zeros((1, HYENA_ORDER, C), F32), k4[:0:-1, :, 1]], axis=0)
    ang = 2 * np.pi * np.outer(np.arange(N), np.arange(N)) / N
    dft = jnp.asarray(np.concatenate([np.cos(ang), -np.sin(ang)], axis=0), F32)
    H = _dft_spectrum(dft, taps.reshape(N, HYENA_ORDER * C), scale)
    fwd = jnp.asarray(np.concatenate([np.cos(ang[:, :L]), -np.sin(ang[:, :L])], axis=0), BF16)
    inv = jnp.asarray(np.concatenate([np.cos(ang[:L]), -np.sin(ang[:L])], axis=1) / N, BF16)
    u = lambda g: pl.BlockSpec((None, L, ct), lambda c, b: (b, 0, g * ncb + c))
    cw = lambda g: pl.BlockSpec((3, ct), lambda c, b: (0, g * ncb + c))
    cb = lambda g: pl.BlockSpec((1, ct), lambda c, b: (0, g * ncb + c))
    const = lambda a: pl.BlockSpec(a.shape, lambda c, b: (0,) * a.ndim)
    return pl.pallas_call(
        _hyena_short_body,
        grid=(ncb, B),
        in_specs=[u(0), u(1), u(2), cw(0), cw(1), cw(2), cb(0), cb(1), cb(2),
                  pl.BlockSpec((HYENA_ORDER, ct), lambda c, b: (0, c)),
                  pl.BlockSpec((2 * N, ct), lambda c, b: (0, c)),
                  pl.BlockSpec((2 * N, ct), lambda c, b: (0, ncb + c)),
                  const(fwd), const(inv)],
        out_specs=pl.BlockSpec((None, L, ct), lambda c, b: (b, 0, c)),
        out_shape=jax.ShapeDtypeStruct((B, L, C), F32),
        name="hyena_short",
        compiler_params=_params("arbitrary", "arbitrary"),
    )(hy, hy, hy, lp['conv_w'], lp['conv_w'], lp['conv_w'], lp['conv_b'], lp['conv_b'], lp['conv_b'],
      lp['f_bias'], H, H, fwd, inv)


FFT_N1 = 128


def _fft_dims(L):
    N = 2 * L
    N1 = FFT_N1
    N2 = N // N1
    assert N1 * N2 == N and N2 % SUBLANES == 0
    return N, N1, N2, N2 + SUBLANES, 2 * N1 + SUBLANES


def _fft_tables(L):
    N, N1, N2, _, _ = _fft_dims(L)
    NH = N1 // 2
    n1 = np.arange(NH)
    k1 = np.arange(N1)
    n2 = np.arange(N2)
    th = 2 * np.pi * (n1[None, None, :] * k1[None, :, None] / N1 + n2[:, None, None] * k1[None, :, None] / N)
    g = np.concatenate([np.cos(th), -np.sin(th)], axis=1).reshape(N2 * 2 * N1, NH)
    ig = np.concatenate([np.cos(th), -np.sin(th)], axis=1).transpose(0, 2, 1) / N
    ig = ig.reshape(N2 * NH, 2 * N1)
    ph = 2 * np.pi * np.outer(n2, n2) / N2
    c, s = np.cos(ph), np.sin(ph)
    f2 = np.block([[c, s], [-s, c]])
    if2 = np.block([[c, -s], [s, c]])

    def hilo(a):
        hi = jnp.asarray(a, F32).astype(BF16)
        lo = (jnp.asarray(a, F32) - hi.astype(F32)).astype(BF16)
        return hi, lo

    return {'g': hilo(g), 'ig': hilo(ig), 'f2': hilo(f2), 'if2': hilo(if2)}


def _mm(tab, r0, nrows, x, passes):
    a_hi = tab[0][pl.ds(r0, nrows), :]
    if passes == 1:
        return _dot(a_hi, x.astype(BF16))
    x_hi, x_lo = _split_bf16(x)
    a_lo = tab[1][pl.ds(r0, nrows), :]
    return _dot(a_hi, x_hi) + (_dot(a_hi, x_lo) + _dot(a_lo, x_hi))


FFT_BATCH = 4
FFT_GROUPS = 2


def _fft_stage1(tbuf, sbuf, g, dims, passes):
    _, N1, N2, P, Q = dims

    def body(i, carry):
        n2s = [i * FFT_BATCH + j for j in range(FFT_BATCH)]
        xs = [tbuf[pl.ds(n2, N1 // 2, stride=P), :] for n2 in n2s]
        outs = [_mm(g, pl.multiple_of(n2 * 2 * N1, 2 * N1), 2 * N1, x, passes) for n2, x in zip(n2s, xs)]
        for n2, out in zip(n2s, outs):
            sbuf[pl.ds(pl.multiple_of(n2 * Q, SUBLANES), 2 * N1), :] = out
        return carry

    lax.fori_loop(0, N2 // FFT_BATCH, body, 0)


def _fft_stage2_load(sbuf, k1, dims):
    _, N1, N2, _, Q = dims
    re = [sbuf[pl.ds(k1 + j, N2, stride=Q), :] for j in range(FFT_BATCH)]
    im = [sbuf[pl.ds(N1 + k1 + j, N2, stride=Q), :] for j in range(FFT_BATCH)]
    return jnp.concatenate([jnp.concatenate(re, axis=1), jnp.concatenate(im, axis=1)], axis=0)


def _hyena_long_body(v_ref, x1_ref, x2_ref, wv_ref, w1_ref, w2_ref, bv_ref, b1_ref, b2_ref, fb_ref, h0_ref, h1_ref,
                     g_ref, ig_ref, f2_ref, if2_ref, o_ref, zbuf, g1buf, g2buf, sbuf, *, dims):
    N, N1, N2, P, Q = dims
    NH = N1 // 2
    zero = jnp.zeros((1, LANES), F32)

    for u_ref, w_ref, b_ref, buf in ((v_ref, wv_ref, bv_ref, zbuf), (x1_ref, w1_ref, b1_ref, g1buf),
                                     (x2_ref, w2_ref, b2_ref, g2buf)):
        w, b = w_ref[...], b_ref[...]
        for n1 in range(NH):
            r0 = n1 * N2
            prev_last = zero if n1 == 0 else u_ref[r0 - 1:r0, :]
            next_first = zero if n1 == NH - 1 else u_ref[r0 + N2:r0 + N2 + 1, :]
            buf[n1 * P:n1 * P + N2, :] = _short_conv_chunk(u_ref, r0, N2, prev_last, next_first, w, b)

    for o, (gbuf, h_ref) in enumerate(((g1buf, h0_ref), (g2buf, h1_ref))):
        _fft_stage1(zbuf, sbuf, (g_ref, None), dims, 1)

        def per_k1(i, carry):
            k1s = [(i * FFT_GROUPS + g) * FFT_BATCH for g in range(FFT_GROUPS)]
            Bs = [_fft_stage2_load(sbuf, k1, dims).astype(BF16) for k1 in k1s]
            Ccs = []
            for k1, B in zip(k1s, Bs):
                X = _dot(f2_ref[...], B)
                h0 = [pl.multiple_of((k1 + j) * 2 * N2, 2 * N2) for j in range(FFT_BATCH)]
                hr = jnp.concatenate([h_ref[pl.ds(r, N2), :] for r in h0], axis=1)
                hi = jnp.concatenate([h_ref[pl.ds(r + N2, N2), :] for r in h0], axis=1)
                xr, xi = X[:N2], X[N2:]
                Y = jnp.concatenate([xr * hr - xi * hi, xr * hi + xi * hr], axis=0)
                Ccs.append(_dot(if2_ref[...], Y.astype(BF16)))
            for k1, Cc in zip(k1s, Ccs):
                for j in range(FFT_BATCH):
                    lanes = slice(j * LANES, (j + 1) * LANES)
                    sbuf[pl.ds(k1 + j, N2, stride=Q), :] = Cc[:N2, lanes]
                    sbuf[pl.ds(N1 + k1 + j, N2, stride=Q), :] = Cc[N2:, lanes]
            return carry

        lax.fori_loop(0, N1 // (FFT_BATCH * FFT_GROUPS), per_k1, 0)
        fb = fb_ref[o:o + 1, :]

        def per_n2(i, carry):
            n2s = [i * FFT_BATCH + j for j in range(FFT_BATCH)]
            Ds = [sbuf[pl.ds(pl.multiple_of(n2 * Q, SUBLANES), 2 * N1), :].astype(BF16) for n2 in n2s]
            ys = [_dot(ig_ref[pl.ds(pl.multiple_of(n2 * NH, NH), NH), :], D) for n2, D in zip(n2s, Ds)]
            for n2, y in zip(n2s, ys):
                rows = pl.ds(n2, NH, stride=P)
                zbuf[rows, :] = gbuf[rows, :] * (y + fb * zbuf[rows, :])
            return carry

        lax.fori_loop(0, N2 // FFT_BATCH, per_n2, 0)

    for n1 in range(NH):
        o_ref[n1 * N2:(n1 + 1) * N2, :] = zbuf[n1 * P:n1 * P + N2, :]


def _fft_spectrum_body(hf_ref, hb_ref, s_ref, g_hi, g_lo, f2_hi, f2_lo, o_ref, tbuf, sbuf, *, dims):
    N, N1, N2, P, Q = dims
    NH = N1 // 2
    scale = s_ref[...]
    for d, h_ref in enumerate((hf_ref, hb_ref)):
        for n1 in range(NH):
            h = h_ref[n1 * N2:(n1 + 1) * N2, :] * scale
            if d == 1 and n1 == 0:
                h = jnp.where(lax.broadcasted_iota(jnp.int32, h.shape, 0) == 0, 0.0, h)
            tbuf[n1 * P:n1 * P + N2, :] = h
        _fft_stage1(tbuf, sbuf, (g_hi, g_lo), dims, 3)

        def per_k1(i, carry):
            k1 = i * FFT_BATCH
            X = _mm((f2_hi, f2_lo), 0, 2 * N2, _fft_stage2_load(sbuf, k1, dims), 3)
            for j in range(FFT_BATCH):
                h0 = pl.multiple_of((k1 + j) * 2 * N2, 2 * N2)
                Xj = X[:, j * LANES:(j + 1) * LANES]
                if d == 0:
                    o_ref[pl.ds(h0, 2 * N2), :] = Xj
                else:
                    o_ref[pl.ds(h0, N2), :] += Xj[:N2]
                    o_ref[pl.ds(h0 + N2, N2), :] -= Xj[N2:]
            return carry

        lax.fori_loop(0, N1 // FFT_BATCH, per_k1, 0)


def _hyena_long(hy, lp, k_un, scale):
    B, L, _ = hy.shape
    C = D_HYENA
    dims = _fft_dims(L)
    N, N1, N2, P, Q = dims
    NH = N1 // 2
    tabs = _fft_tables(L)
    ncb = C // LANES
    nspec = HYENA_ORDER * ncb
    const1 = lambda a, n: pl.BlockSpec(a.shape, (lambda *i: (0,) * a.ndim), pipeline_mode=pl.Buffered(1))
    hcol = lambda d: pl.BlockSpec((L, LANES), lambda j: (0, (j // ncb) * 2 * ncb + d * ncb + j % ncb))
    H = pl.pallas_call(
        functools.partial(_fft_spectrum_body, dims=dims),
        grid=(nspec,),
        in_specs=[hcol(0), hcol(1), pl.BlockSpec((1, LANES), lambda j: (0, j)),
                  const1(tabs['g'][0], 1), const1(tabs['g'][1], 1),
                  const1(tabs['f2'][0], 1), const1(tabs['f2'][1], 1)],
        out_specs=pl.BlockSpec((2 * N, LANES), lambda j: (0, j)),
        out_shape=jax.ShapeDtypeStruct((2 * N, HYENA_ORDER * C), F32),
        scratch_shapes=[pltpu.VMEM((NH * P, LANES), F32), pltpu.VMEM((N2 * Q, LANES), F32)],
        name="filter_spectrum",
        compiler_params=_params("arbitrary"),
    )(k_un, k_un, scale, tabs['g'][0], tabs['g'][1], tabs['f2'][0], tabs['f2'][1])

    one = pl.Buffered(1)
    hspec = lambda o: pl.BlockSpec((2 * N, LANES), lambda c, b: (0, o * ncb + c), pipeline_mode=one)
    u = lambda g: pl.BlockSpec((None, L, LANES), lambda c, b: (b, 0, g * ncb + c), pipeline_mode=one)
    cw = lambda g: pl.BlockSpec((3, LANES), lambda c, b: (0, g * ncb + c))
    cb = lambda g: pl.BlockSpec((1, LANES), lambda c, b: (0, g * ncb + c))
    return pl.pallas_call(
        functools.partial(_hyena_long_body, dims=dims),
        grid=(ncb, B),
        in_specs=[u(0), u(1), u(2), cw(0), cw(1), cw(2), cb(0), cb(1), cb(2),
                  pl.BlockSpec((HYENA_ORDER, LANES), lambda c, b: (0, c)),
                  hspec(0), hspec(1),
                  const1(tabs['g'][0], 2), const1(tabs['ig'][0], 2),
                  const1(tabs['f2'][0], 2), const1(tabs['if2'][0], 2)],
        out_specs=pl.BlockSpec((None, L, LANES), lambda c, b: (b, 0, c)),
        out_shape=jax.ShapeDtypeStruct((B, L, C), F32),
        scratch_shapes=[pltpu.VMEM((NH * P, LANES), F32), pltpu.VMEM((NH * P, LANES), F32),
                        pltpu.VMEM((NH * P, LANES), F32), pltpu.VMEM((N2 * Q, LANES), F32)],
        name="hyena_long",
        compiler_params=_params("arbitrary", "arbitrary"),
    )(hy, hy, hy, lp['conv_w'], lp['conv_w'], lp['conv_w'], lp['conv_b'], lp['conv_b'], lp['conv_b'],
      lp['f_bias'], H, H, tabs['g'][0], tabs['ig'][0], tabs['f2'][0], tabs['if2'][0])


def _hyena(hy, lp):
    L = hy.shape[1]
    k_un, scale = _hyena_filters(L, lp)
    if 2 * L >= 2 * FFT_N1 * SUBLANES and (2 * L) % (FFT_N1 * SUBLANES) == 0:
        return _hyena_long(hy, lp, k_un, scale)
    return _hyena_short(hy, lp, k_un, scale)


def kernel(x_prompt, x_sample, cache_k, cache_v, c, c_ctx, w_mod, b_mod, g_mix, w_in, conv_w, conv_b, f_w1, f_b1,
           f_freq, f_w2, f_b2, f_w3, f_bias, rpb, g_out_hy, g_out_at, w_out, g_ffn, w_router, b_router, w_exp1,
           b_exp1, w_exp2, b_exp2, g_final):
    depth = w_mod.shape[0]
    Bp, Lp, D = x_prompt.shape
    Bs, Ls, _ = x_sample.shape
    n_ctx, n_lat = Bp * Lp, Bs * Ls
    n_tok = n_ctx + n_lat
    d_attn = N_HEADS * HEAD_DIM
    q_off = 3 * D_HYENA
    row2 = lambda a: a.reshape(1, -1)

    cond = jnp.zeros((SUBLANES, D), F32).at[0].set(c_ctx).at[1:1 + Bs].set(c)
    xp, xs = x_prompt, x_sample
    new_k, new_v = [], []
    for l in range(depth):
        wr = jnp.zeros((D, LANES), F32).at[:, :N_EXPERTS].set(w_router[l])
        wr_hi = wr.astype(BF16)
        lp = {
            'conv_w': conv_w[l], 'conv_b': row2(conv_b[l]), 'f_w1': f_w1[l], 'f_b1': row2(f_b1[l]),
            'f_freq': row2(f_freq[l]), 'f_w2': f_w2[l], 'f_b2': row2(f_b2[l]), 'f_w3': f_w3[l],
            'f_bias': f_bias[l], 'g_out_hy': row2(g_out_hy[l]), 'g_out_at': row2(g_out_at[l]),
            'w_out': w_out[l].astype(BF16), 'g_ffn': row2(g_ffn[l]),
            'wr_hi': wr_hi, 'wr_lo': (wr - wr_hi.astype(F32)).astype(BF16),
            'b_router': jnp.zeros((1, LANES), F32).at[0, :N_EXPERTS].set(b_router[l]),
            'w_exp1': w_exp1[l], 'b_exp1': b_exp1[l][:, None, :], 'w_exp2': w_exp2[l],
            'b_exp2': b_exp2[l][:, None, :],
        }
        mod = _modulation(cond, w_mod[l], row2(b_mod[l]))
        mod3 = mod.reshape(SUBLANES, 1, 6 * D)
        w_in_b = w_in[l].astype(BF16)
        g_mix_l = row2(g_mix[l])

        proj_p = _in_projection(xp, mod3, 0, g_mix_l, w_in_b)
        proj_s = _in_projection(xs, mod3, 1, g_mix_l, w_in_b)
        kv = proj_p[..., q_off + d_attn:].reshape(Bp, Lp, 2, N_HEADS, HEAD_DIM).transpose(2, 0, 3, 1, 4)
        new_k.append(kv[0])
        new_v.append(kv[1])

        hy_p = _hyena(proj_p, lp)
        hy_s = _hyena(proj_s, lp)
        at_p = _context_attention(proj_p, q_off)
        heads_last = lambda t: t.transpose(0, 2, 1, 3).reshape(Bs, t.shape[2], d_attn)
        at_s = _neighbourhood_attention(proj_s, q_off, heads_last(cache_k[:, l]), heads_last(cache_v[:, l]), rpb[l])

        x1, h2u, logits = _out_projection((hy_p, at_p, xp), (hy_s, at_s, xs), mod3, lp)

        gates, ce, nv, n_used, gidx, sidx = _routing(logits[:, :N_EXPERTS], n_tok)
        yp = _experts(ce, nv, n_used, gidx, sidx, h2u, lp, n_tok * TOP_K + MOE_CHUNK)
        last = l == depth - 1
        gf = row2(g_final) if last else None
        assert last, "deeper stacks need the un-normalised residual between layers"
        xp = _combine(x1, yp, gates, mod3, 0, gf, 0, Bp, Lp)
        xs = _combine(x1, yp, gates, mod3, 1, gf, n_ctx, Bs, Ls)

    return xp, xs, jnp.stack(new_k, axis=1), jnp.stack(new_v, axis=1)
```

```python
import functools
import math

import numpy as np
import jax
import jax.numpy as jnp
from jax import lax
from jax.experimental import pallas as pl
from jax.experimental.pallas import tpu as pltpu

F32 = jnp.float32
BF16 = jnp.bfloat16
HIGHEST = lax.Precision.HIGHEST

GRID_W = 64
N_HEADS = 16
HEAD_DIM = 64
D_HYENA = 1024
HYENA_ORDER = 2
FILTER_BANDS = 16
DECAY_TARGET = 1e-2
DECAY_PCT_SHORT = 0.3
DECAY_PCT_LONG = 1.5
WIN_H = 8
WIN_W = 16
N_EXPERTS = 32
TOP_K = 4
SWIGLU_ALPHA = 1.702
SWIGLU_LIMIT = 7.0
EPS = 1e-6
NEG_INF = -1e30

LANES = 128
SUBLANES = 8
VMEM_LIMIT = 60 * 1024 * 1024

ROW_TILE = 256
NA_Q_ROWS = 4
MOE_CHUNK = 1024
MOE_SUB = 256
MOE_FF_TILE = 512
MOE_ISSUE = 64


def _params(*sem):
    return pltpu.CompilerParams(dimension_semantics=sem, vmem_limit_bytes=VMEM_LIMIT)


def _rms(x, g):
    return x * lax.rsqrt(jnp.mean(x * x, axis=-1, keepdims=True) + EPS) * g


def _split_bf16(x):
    hi = x.astype(BF16)
    lo = (x - hi.astype(F32)).astype(BF16)
    return hi, lo


def _dot(a, b):
    return jnp.dot(a, b, preferred_element_type=F32)


def _dot_nt(a, b):
    return lax.dot_general(a, b, (((1,), (1,)), ((), ())), preferred_element_type=F32)


def _mod_body(c_ref, w_ref, b_ref, o_ref):
    c = c_ref[...]
    s = c / (1.0 + jnp.exp(-c))
    o_ref[...] = jnp.dot(s, w_ref[...], precision=HIGHEST, preferred_element_type=F32) + b_ref[...]


def _modulation(cc, w, b):
    D, N = w.shape
    tn = min(N, 1536)
    return pl.pallas_call(
        _mod_body,
        grid=(N // tn,),
        in_specs=[pl.BlockSpec((SUBLANES, D), lambda j: (0, 0)),
                  pl.BlockSpec((D, tn), lambda j: (0, j)),
                  pl.BlockSpec((1, tn), lambda j: (0, j))],
        out_specs=pl.BlockSpec((SUBLANES, tn), lambda j: (0, j)),
        out_shape=jax.ShapeDtypeStruct((SUBLANES, N), F32),
        name="modulation",
        compiler_params=_params("arbitrary"),
    )(cc, w, b)


def _inproj_body(x_ref, shift_ref, scale_ref, g_ref, w_ref, o_ref, *, n_chunk):
    h = _rms(x_ref[...], g_ref[...]) * (1.0 + scale_ref[...]) + shift_ref[...]
    hb = h.astype(BF16)
    n_out = o_ref.shape[-1]

    def col(j, carry):
        c0 = pl.multiple_of(j * n_chunk, n_chunk)
        o_ref[:, pl.ds(c0, n_chunk)] = _dot(hb, w_ref[:, pl.ds(c0, n_chunk)])
        return carry

    lax.fori_loop(0, n_out // n_chunk, col, 0)


def _in_projection(x, mod3, row0, g, w_bf16):
    B, L, D = x.shape
    N = w_bf16.shape[1]
    tm = min(ROW_TILE, L)
    mrow = (lambda b: 0) if row0 == 0 else (lambda b: b + row0)
    return pl.pallas_call(
        functools.partial(_inproj_body, n_chunk=512),
        grid=(B, L // tm),
        in_specs=[pl.BlockSpec((None, tm, D), lambda b, i: (b, i, 0)),
                  pl.BlockSpec((None, 1, D), lambda b, i: (mrow(b), 0, 0)),
                  pl.BlockSpec((None, 1, D), lambda b, i: (mrow(b), 0, 1)),
                  pl.BlockSpec((1, D), lambda b, i: (0, 0)),
                  pl.BlockSpec((D, N), lambda b, i: (0, 0), pipeline_mode=pl.Buffered(1))],
        out_specs=pl.BlockSpec((None, tm, N), lambda b, i: (b, i, 0)),
        out_shape=jax.ShapeDtypeStruct((B, L, N), F32),
        name="in_projection",
        compiler_params=_params("arbitrary", "arbitrary"),
    )(x, mod3, mod3, g, w_bf16)


def _head_lane_masks(rows, hd):
    lane = lax.broadcasted_iota(jnp.int32, (rows, LANES), 1)
    return [(lane >= h * hd) & (lane < (h + 1) * hd) for h in range(LANES // hd)]


def _ctx_attn_body(q_ref, k_ref, v_ref, o_ref, *, scale, hd):
    q = q_ref[...]
    k = k_ref[...].astype(BF16)
    v = v_ref[...].astype(BF16)
    masks = _head_lane_masks(q.shape[0], hd)
    out = None
    for msk in masks:
        qh = jnp.where(msk, q, 0.0).astype(BF16)
        s = _dot_nt(qh, k) * scale
        p = jnp.exp(s - jnp.max(s, axis=-1, keepdims=True))
        o = _dot(p.astype(BF16), v) / jnp.sum(p, axis=-1, keepdims=True)
        out = o if out is None else jnp.where(msk, o, out)
    o_ref[...] = out


def _context_attention(proj, q_off):
    B, L, _ = proj.shape
    d_attn = N_HEADS * HEAD_DIM
    nhp = d_attn // LANES
    qb, kb, vb = (q_off // LANES, (q_off + d_attn) // LANES, (q_off + 2 * d_attn) // LANES)
    spec = lambda base: pl.BlockSpec((None, L, LANES), lambda b, h: (b, 0, base + h))
    return pl.pallas_call(
        functools.partial(_ctx_attn_body, scale=1.0 / math.sqrt(HEAD_DIM), hd=HEAD_DIM),
        grid=(B, nhp),
        in_specs=[spec(qb), spec(kb), spec(vb)],
        out_specs=pl.BlockSpec((None, L, LANES), lambda b, h: (b, 0, h)),
        out_shape=jax.ShapeDtypeStruct((B, L, d_attn), F32),
        name="context_attention",
        compiler_params=_params("arbitrary", "arbitrary"),
    )(proj, proj, proj)


def _na_tables(rows, rpb):
    W = GRID_W
    kh = min(WIN_H, rows)
    rbq = NA_Q_ROWS
    kwr = rbq + kh
    assert rows % rbq == 0 and kwr <= rows
    nblk = rows // rbq
    ws = np.clip(np.arange(nblk) * rbq - kh // 2, 0, rows - kwr)
    rq = np.arange(nblk)[:, None, None] * rbq + np.arange(rbq)[None, :, None]
    rk = ws[:, None, None] + np.arange(kwr)[None, None, :]
    rs = np.clip(rq - kh // 2, 0, rows - kh)
    vr = (rk >= rs) & (rk < rs + kh)
    dr = np.where(vr, rk - rq + WIN_H - 1, 0)
    patterns = np.concatenate([dr.reshape(nblk, -1), vr.reshape(nblk, -1)], axis=1)
    _, first, btype = np.unique(patterns, axis=0, return_index=True, return_inverse=True)
    dr_t, vr_t = dr[first], vr[first]
    cq = np.arange(W)[:, None]
    ck = np.arange(W)[None, :]
    cs = np.clip(cq - WIN_W // 2, 0, W - WIN_W)
    vc = (ck >= cs) & (ck < cs + WIN_W)
    dc = np.clip(ck - cq, -(WIN_W - 1), WIN_W - 1) + WIN_W - 1
    nt = dr_t.shape[0]
    rowsel = (dr_t[..., None] == np.arange(2 * WIN_H - 1)) & vr_t[..., None]
    colsel = (dc[..., None] == np.arange(2 * WIN_W - 1)) & vc[..., None]
    vals = jnp.einsum('tajr,hrc,qkc->htaqjk', rowsel.astype(np.float32), rpb.astype(F32),
                      colsel.astype(np.float32), precision=HIGHEST)
    valid = vr_t[:, :, None, :, None] & vc[None, None, :, None, :]
    bias = jnp.where(valid[None], vals, NEG_INF)
    bias = bias.reshape(rpb.shape[0], nt, rbq * W, kwr * W)
    return ws.astype(np.int32), btype.reshape(-1).astype(np.int32), bias


def _na_body(ws_ref, bt_ref, q_ref, k_ref, v_ref, kc_ref, vc_ref, bias_ref, o_ref, *, scale, hd, nk):
    i = pl.program_id(2)
    start = pl.multiple_of(ws_ref[i] * GRID_W, GRID_W)
    q = q_ref[...] * scale
    kl = k_ref[pl.ds(start, nk), :].astype(BF16)
    vl = v_ref[pl.ds(start, nk), :]
    kc = kc_ref[...].astype(BF16)
    vc = vc_ref[...]
    masks = _head_lane_masks(q.shape[0], hd)
    vl_masks = _head_lane_masks(nk, hd)
    vc_masks = _head_lane_masks(vc.shape[0], hd)
    assert len(masks) == 2, "the value lanes of the other head carry the softmax denominator"
    out = None
    for h, msk in enumerate(masks):
        qh = jnp.where(msk, q, 0.0).astype(BF16)
        sl = _dot_nt(qh, kl) + bias_ref[h]
        sc = _dot_nt(qh, kc)
        m = jnp.maximum(jnp.max(sl, axis=-1, keepdims=True), jnp.max(sc, axis=-1, keepdims=True))
        p_l = jnp.exp(sl - m).astype(BF16)
        p_c = jnp.exp(sc - m).astype(BF16)
        o = (_dot(p_l, jnp.where(vl_masks[h], vl, 1.0).astype(BF16))
             + _dot(p_c, jnp.where(vc_masks[h], vc, 1.0).astype(BF16)))
        o = o / pltpu.roll(o, hd, axis=1)
        out = o if out is None else jnp.where(msk, o, out)
    o_ref[...] = out


def _neighbourhood_attention(proj, q_off, kc, vc, rpb):
    B, L, _ = proj.shape
    Lc = kc.shape[1]
    d_attn = N_HEADS * HEAD_DIM
    hpb = LANES // HEAD_DIM
    nhp = d_attn // LANES
    rows = L // GRID_W
    ws, btype, bias = _na_tables(rows, rpb)
    nq = NA_Q_ROWS * GRID_W
    nk = (NA_Q_ROWS + min(WIN_H, rows)) * GRID_W
    nblk = rows // NA_Q_ROWS
    qb, kb, vb = (q_off // LANES, (q_off + d_attn) // LANES, (q_off + 2 * d_attn) // LANES)
    full = lambda base: pl.BlockSpec((None, L, LANES), lambda b, h, i, ws_r, bt_r: (b, 0, base + h))
    ctx = pl.BlockSpec((None, Lc, LANES), lambda b, h, i, ws_r, bt_r: (b, 0, h))
    grid_spec = pltpu.PrefetchScalarGridSpec(
        num_scalar_prefetch=2,
        grid=(B, nhp, nblk),
        in_specs=[pl.BlockSpec((None, nq, LANES), lambda b, h, i, ws_r, bt_r: (b, i, qb + h)),
                  full(kb), full(vb), ctx, ctx,
                  pl.BlockSpec((hpb, None, nq, nk), lambda b, h, i, ws_r, bt_r: (h, bt_r[i], 0, 0))],
        out_specs=pl.BlockSpec((None, nq, LANES), lambda b, h, i, ws_r, bt_r: (b, i, h)),
    )
    return pl.pallas_call(
        functools.partial(_na_body, scale=1.0 / math.sqrt(HEAD_DIM), hd=HEAD_DIM, nk=nk),
        grid_spec=grid_spec,
        out_shape=jax.ShapeDtypeStruct((B, L, d_attn), F32),
        name="neighbourhood_attention",
        compiler_params=_params("arbitrary", "arbitrary", "arbitrary"),
    )(jnp.asarray(ws), jnp.asarray(btype), proj, proj, proj, kc, vc, bias)


def _outproj_body(yhp_ref, yap_ref, xp_ref, yhs_ref, yas_ref, xs_ref, gate1_ref, shift2_ref, scale2_ref, ghy_ref,
                  gat_ref, w_ref, gffn_ref, wrh_ref, wrl_ref, br_ref, x1_ref, h2u_ref, lg_ref, *, n_ctx_tiles):
    def run(yh_ref, ya_ref, x_ref):
        cat = jnp.concatenate([_rms(yh_ref[...], ghy_ref[...]), _rms(ya_ref[...], gat_ref[...])], axis=-1)
        mix = _dot(cat.astype(BF16), w_ref[...])
        x1 = x_ref[...] + gate1_ref[...] * mix
        x1_ref[...] = x1
        h2 = _rms(x1, gffn_ref[...]) * (1.0 + scale2_ref[...]) + shift2_ref[...]
        hi, lo = _split_bf16(h2)
        lg_ref[...] = (_dot(hi, wrh_ref[...]) + _dot(lo, wrh_ref[...]) + _dot(hi, wrl_ref[...])) + br_ref[...]
        bits = lax.bitcast_convert_type(hi.astype(F32), jnp.uint32)
        half = bits.shape[-1] // 2
        words = (bits[:, :half] >> 16) | (bits[:, half:] & jnp.uint32(0xFFFF0000))
        per = half // LANES
        for s in range(per):
            h2u_ref[pl.ds(s, words.shape[0], stride=per), :] = words[:, s * LANES:(s + 1) * LANES]

    is_ctx = pl.program_id(0) < n_ctx_tiles
    pl.when(is_ctx)(lambda: run(yhp_ref, yap_ref, xp_ref))
    pl.when(jnp.logical_not(is_ctx))(lambda: run(yhs_ref, yas_ref, xs_ref))


def _out_projection(ctx, lat, mod3, lp):
    Bp, Lp, D = ctx[2].shape
    Bs, Ls, _ = lat[2].shape
    dh, da = ctx[0].shape[-1], ctx[1].shape[-1]
    tm = min(ROW_TILE, Lp, Ls)
    nct, nlt = Bp * Lp // tm, Bs * Ls // tm
    n_tok = (nct + nlt) * tm
    lat_tiles = Ls // tm
    mrow = lambda i: jnp.where(i < nct, 0, 1 + (i - nct) // lat_tiles)
    modspec = lambda c: pl.BlockSpec((None, 1, D), lambda i: (mrow(i), 0, c))
    const = lambda shape: pl.BlockSpec(shape, lambda i: (0,) * len(shape))
    crow = lambda w: pl.BlockSpec((tm, w), lambda i: (jnp.minimum(i, nct - 1), 0))
    lrow = lambda w: pl.BlockSpec((tm, w), lambda i: (jnp.maximum(i - nct, 0), 0))
    flat = lambda t: t.reshape(-1, t.shape[-1])
    return pl.pallas_call(
        functools.partial(_outproj_body, n_ctx_tiles=nct),
        grid=(nct + nlt,),
        in_specs=[crow(dh), crow(da), crow(D), lrow(dh), lrow(da), lrow(D),
                  modspec(2), modspec(3), modspec(4),
                  const((1, dh)), const((1, da)),
                  pl.BlockSpec((dh + da, D), lambda i: (0, 0), pipeline_mode=pl.Buffered(1)),
                  const((1, D)), const((D, LANES)), const((D, LANES)), const((1, LANES))],
        out_specs=[pl.BlockSpec((tm, D), lambda i: (i, 0)),
                   pl.BlockSpec((tm * (D // 2 // LANES), LANES), lambda i: (i, 0)),
                   pl.BlockSpec((tm, LANES), lambda i: (i, 0))],
        out_shape=[jax.ShapeDtypeStruct((n_tok, D), F32),
                   jax.ShapeDtypeStruct((n_tok * (D // 2 // LANES), LANES), jnp.uint32),
                   jax.ShapeDtypeStruct((n_tok, LANES), F32)],
        name="out_projection",
        compiler_params=_params("arbitrary"),
    )(*[flat(t) for t in ctx], *[flat(t) for t in lat], mod3, mod3, mod3, lp['g_out_hy'], lp['g_out_at'],
      lp['w_out'], lp['g_ffn'], lp['wr_hi'], lp['wr_lo'], lp['b_router'])


def _expert_body(ce_ref, nv_ref, nu_ref, gcur_ref, gnext_ref, sprev_ref, scur_ref, h2u_hbm, w1g_ref, w1u_ref,
                 b1g_ref, b1u_ref, w2_ref, b2_ref, yp_hbm, xu, xb, acc, ystage, abuf, gsem, ssem):
    c = pl.program_id(0)
    f = pl.program_id(1)
    nc = pl.num_programs(0)
    nf = pl.num_programs(1)
    half = xb.shape[1] // 2
    xt = half // LANES
    yt = acc.shape[1] // LANES
    slot = c % 2
    subs = lambda n: (n + MOE_SUB - 1) // MOE_SUB
    nv = nv_ref[c]
    nsub = subs(nv)
    nrows = nsub * MOE_SUB
    last = c == nu_ref[0] - 1
    nv_prev = nv_ref[jnp.maximum(c - 1, 0)]
    nrows_prev = subs(nv_prev) * MOE_SUB
    covered_prev = MOE_ISSUE * nf * subs(nv_prev)
    nrows_next = jnp.where(c + 1 < nc, subs(nv_ref[jnp.minimum(c + 1, nc - 1)]), 0) * MOE_SUB
    covered = MOE_ISSUE * nf * nsub

    def gather_row(idx_ref, r, to_slot):
        tok = idx_ref[0, r]
        pltpu.make_async_copy(h2u_hbm.at[pl.ds(pl.multiple_of(tok * xt, xt), xt), :],
                              xu.at[to_slot, pl.ds(pl.multiple_of(r * xt, xt), xt), :], gsem.at[to_slot]).start()

    def scatter_row(dst, r):
        pltpu.make_async_copy(ystage.at[pl.ds(pl.multiple_of(r * yt, yt), yt), :],
                              yp_hbm.at[pl.ds(pl.multiple_of(dst * yt, yt), yt), :], ssem).start()

    def for_rows(lo, hi, fn):
        lax.fori_loop(lo, hi, lambda r, carry: (fn(r), carry)[1], 0)

    def drain(n, src, dst, sem, per):
        piece = lambda m: pltpu.make_async_copy(src.at[pl.ds(0, m * per), :], dst.at[pl.ds(0, m * per), :], sem)
        for_rows(0, n // MOE_SUB, lambda i: piece(MOE_SUB).wait())
        bit = MOE_SUB // 2
        while bit:
            if not isinstance(n, int):
                pl.when((n & bit) != 0)(piece(bit).wait)
            elif n & bit:
                piece(bit).wait()
            bit //= 2

    @pl.when(c < nu_ref[0])
    def _chunk():
        @pl.when(f == 0)
        def _arrive():
            @pl.when(c == 0)
            def _first():
                for_rows(0, nrows, lambda r: gather_row(gcur_ref, r, 0))

            started = jnp.where(c == 0, nrows, jnp.maximum(covered_prev, nrows))
            drain(started, h2u_hbm, xu.at[slot], gsem.at[slot], xt)

            def unpack(sb, carry):
                r0 = pl.multiple_of(sb * MOE_SUB, MOE_SUB)
                for s in range(xt):
                    u = xu[slot, pl.ds(r0 * xt + s, MOE_SUB, stride=xt), :]
                    lo = lax.bitcast_convert_type(u << 16, F32)
                    hi = lax.bitcast_convert_type(u & jnp.uint32(0xFFFF0000), F32)
                    xb[pl.ds(r0, MOE_SUB), s * LANES:(s + 1) * LANES] = lo.astype(BF16)
                    xb[pl.ds(r0, MOE_SUB), half + s * LANES:half + (s + 1) * LANES] = hi.astype(BF16)
                acc[pl.ds(r0, MOE_SUB), :] = jnp.zeros((MOE_SUB, acc.shape[1]), F32)
                return carry

            lax.fori_loop(0, nsub, unpack, 0)

        def matmul_steps(with_scatter):
            def up_proj(sb):
                base = (f * nsub + sb) * MOE_ISSUE
                for j in range(MOE_ISSUE):
                    gather_row(gnext_ref, base + j, 1 - slot)
                if with_scatter:
                    for j in range(MOE_ISSUE):
                        scatter_row(sprev_ref[0, base + j], jnp.minimum(base + j, nrows_prev - 1))
                x = xb[pl.ds(pl.multiple_of(sb * MOE_SUB, MOE_SUB), MOE_SUB), :]
                gate = jnp.minimum(_dot(x, w1g_ref[...].astype(BF16)) + b1g_ref[...], SWIGLU_LIMIT)
                up = jnp.clip(_dot(x, w1u_ref[...].astype(BF16)) + b1u_ref[...], -SWIGLU_LIMIT, SWIGLU_LIMIT)
                glu = gate / (1.0 + jnp.exp(-SWIGLU_ALPHA * gate))
                abuf[...] = ((up + 1.0) * glu).astype(BF16)

            def down_proj(sb):
                acc[pl.ds(pl.multiple_of(sb * MOE_SUB, MOE_SUB), MOE_SUB), :] += _dot(abuf[...], w2_ref[...].astype(BF16))

            def step(sb, carry):
                down_proj(sb - 1)
                up_proj(sb)
                return carry

            up_proj(0)
            lax.fori_loop(1, nsub, step, 0)
            down_proj(nsub - 1)

        pl.when(c == 0)(lambda: matmul_steps(False))
        pl.when(c > 0)(lambda: matmul_steps(True))

        @pl.when(f == nf - 1)
        def _leave():
            for_rows(covered, nrows_next, lambda r: gather_row(gnext_ref, r, 1 - slot))

            @pl.when(c > 0)
            def _prev_out():
                for_rows(covered, nv_prev, lambda r: scatter_row(sprev_ref[0, r], r))
                drain(jnp.maximum(covered, nv_prev), yp_hbm, ystage, ssem, yt)

            def stage(sb, carry):
                r0 = pl.multiple_of(sb * MOE_SUB, MOE_SUB)
                for s in range(yt):
                    cols = slice(s * LANES, (s + 1) * LANES)
                    ystage[pl.ds(r0 * yt + s, MOE_SUB, stride=yt), :] = acc[pl.ds(r0, MOE_SUB), cols] + b2_ref[:, cols]
                return carry

            lax.fori_loop(0, nsub, stage, 0)

            @pl.when(last)
            def _flush():
                for_rows(0, nv, lambda r: scatter_row(scur_ref[0, r], r))
                spare0 = yp_hbm.shape[0] // yt - xb.shape[0]
                for_rows(0, xb.shape[0], lambda r: scatter_row(spare0 + r, jnp.minimum(r, nrows - 1)))
                drain(nv, yp_hbm, ystage, ssem, yt)
                drain(xb.shape[0], yp_hbm, ystage, ssem, yt)
                drain(jnp.maximum(covered, nrows_next), h2u_hbm, xu.at[1 - slot], gsem.at[1 - slot], xt)


def _experts(chunk_e, chunk_nv, n_used, gidx, sidx, h2u, lp, n_out_rows):
    E, D, two_ff = lp['w_exp1'].shape
    d_ff = two_ff // 2
    nc = gidx.shape[0]
    R, tf = MOE_CHUNK, MOE_FF_TILE
    nf = d_ff // tf
    assert MOE_ISSUE * nf * (R // MOE_SUB) <= R, "a chunk's matmul steps must not start more rows than a chunk holds"

    def ff(c, f, nu):
        return jnp.where(c < nu[0], f, nf - 1)

    smem = lambda step: pl.BlockSpec((None, 1, R), lambda c, f, ce, nv, nu: (jnp.clip(c + step, 0, nc - 1), 0, 0),
                                     memory_space=pltpu.SMEM)
    grid_spec = pltpu.PrefetchScalarGridSpec(
        num_scalar_prefetch=3,
        grid=(nc, nf),
        in_specs=[smem(0), smem(1), smem(-1), smem(0),
                  pl.BlockSpec(memory_space=pl.ANY),
                  pl.BlockSpec((None, D, tf), lambda c, f, ce, nv, nu: (ce[c], 0, ff(c, f, nu))),
                  pl.BlockSpec((None, D, tf), lambda c, f, ce, nv, nu: (ce[c], 0, nf + ff(c, f, nu))),
                  pl.BlockSpec((None, 1, tf), lambda c, f, ce, nv, nu: (ce[c], 0, ff(c, f, nu))),
                  pl.BlockSpec((None, 1, tf), lambda c, f, ce, nv, nu: (ce[c], 0, nf + ff(c, f, nu))),
                  pl.BlockSpec((None, tf, D), lambda c, f, ce, nv, nu: (ce[c], ff(c, f, nu), 0)),
                  pl.BlockSpec((None, 1, D), lambda c, f, ce, nv, nu: (ce[c], 0, 0))],
        out_specs=pl.BlockSpec(memory_space=pl.ANY),
        scratch_shapes=[pltpu.VMEM((2, R * (D // 2 // LANES), LANES), jnp.uint32),
                        pltpu.VMEM((R, D), BF16),
                        pltpu.VMEM((R, D), F32),
                        pltpu.VMEM((R * (D // LANES), LANES), F32),
                        pltpu.VMEM((MOE_SUB, tf), BF16),
                        pltpu.SemaphoreType.DMA((2,)),
                        pltpu.SemaphoreType.DMA(())],
    )
    return pl.pallas_call(
        _expert_body,
        grid_spec=grid_spec,
        out_shape=jax.ShapeDtypeStruct((n_out_rows * (D // LANES), LANES), F32),
        name="experts",
        compiler_params=pltpu.CompilerParams(dimension_semantics=("arbitrary", "arbitrary"),
                                             vmem_limit_bytes=VMEM_LIMIT, has_side_effects=True,
                                             disable_bounds_checks=True),
    )(chunk_e, chunk_nv, n_used, gidx, gidx, sidx, sidx, h2u, lp['w_exp1'], lp['w_exp1'], lp['b_exp1'],
      lp['b_exp1'], lp['w_exp2'], lp['b_exp2'])


def _routing(logits, n_tok):
    R = MOE_CHUNK
    top_v, top_i = lax.top_k(logits, TOP_K)
    gates = jax.nn.softmax(top_v, axis=-1)
    n_pairs = n_tok * TOP_K
    flat_e = top_i.reshape(n_pairs).astype(jnp.int32)
    experts = jnp.arange(N_EXPERTS, dtype=jnp.int32)
    counts = jnp.sum((flat_e[:, None] == experts[None]).astype(jnp.int32), axis=0)
    padded = (counts + R - 1) // R * R
    pad_end = jnp.cumsum(padded)
    pad_start = pad_end - padded
    nc = n_pairs // R + N_EXPERTS
    pair = jnp.arange(n_pairs, dtype=jnp.int32)
    tok = pair // TOP_K
    dest = (pair % TOP_K) * n_tok + tok
    fill_e = jnp.repeat(experts, R)
    fill_i = jnp.tile(jnp.arange(R, dtype=jnp.int32), N_EXPERTS)
    fill_key = jnp.where(fill_i < jnp.repeat(padded - counts, R), 2 * fill_e + 1, 2 * N_EXPERTS)
    zeros = jnp.zeros((N_EXPERTS * R,), jnp.int32)
    _, gidx, sidx = lax.sort((jnp.concatenate([2 * flat_e, fill_key]), jnp.concatenate([tok, zeros]),
                              jnp.concatenate([dest, n_pairs + fill_i])), num_keys=1)
    n_used = (pad_end[-1] // R).astype(jnp.int32)
    cstart = jnp.arange(nc, dtype=jnp.int32) * R
    ce = jnp.minimum(jnp.searchsorted(pad_end, cstart, side='right'), N_EXPERTS - 1).astype(jnp.int32)
    ce = jnp.where(jnp.arange(nc) < n_used, ce, ce[jnp.maximum(n_used - 1, 0)])
    nv = jnp.clip(counts[ce] - (cstart - pad_start[ce]), 0, R).astype(jnp.int32)
    nv = jnp.where(jnp.arange(nc) < n_used, nv, 0)
    return gates, ce, nv, n_used.reshape(1), gidx.reshape(nc, 1, R), sidx.reshape(nc, 1, R)


def _combine_body(x1_ref, y0_ref, y1_ref, y2_ref, y3_ref, g_ref, gate2_ref, gf_ref, o_ref):
    g = g_ref[...]
    tm, D = x1_ref.shape
    per = D // LANES

    def rows(y_ref):
        return jnp.concatenate([y_ref[pl.ds(s, tm, stride=per), :] for s in range(per)], axis=-1)

    ff = (g[:, 0:1] * rows(y0_ref) + g[:, 1:2] * rows(y1_ref)) + (g[:, 2:3] * rows(y2_ref) + g[:, 3:4] * rows(y3_ref))
    x2 = x1_ref[...] + gate2_ref[...] * ff
    o_ref[...] = _rms(x2, gf_ref[...])


def _combine(x1, yp, gates, mod3, row0, g_final, tok0, B, L):
    n_tok, D = x1.shape
    tm = min(ROW_TILE, L)
    nl = L // tm
    blk0 = tok0 // tm
    nblk_tok = n_tok // tm
    mrow = (lambda b: 0) if row0 == 0 else (lambda b: b + row0)
    row = lambda b, i: (blk0 + b * nl + i, 0)
    yspec = lambda k: pl.BlockSpec((tm * (D // LANES), LANES), lambda b, i: (k * nblk_tok + blk0 + b * nl + i, 0))
    return pl.pallas_call(
        _combine_body,
        grid=(B, nl),
        in_specs=[pl.BlockSpec((tm, D), row), yspec(0), yspec(1), yspec(2), yspec(3),
                  pl.BlockSpec((tm, TOP_K), row),
                  pl.BlockSpec((None, 1, D), lambda b, i: (mrow(b), 0, 5)),
                  pl.BlockSpec((1, D), lambda b, i: (0, 0))],
        out_specs=pl.BlockSpec((None, tm, D), lambda b, i: (b, i, 0)),
        out_shape=jax.ShapeDtypeStruct((B, L, D), F32),
        name="combine",
        compiler_params=_params("arbitrary", "arbitrary"),
    )(x1, yp, yp, yp, yp, gates, mod3, g_final)


def _filter_body(tw_ref, bands_ref, w1_ref, b1_ref, fr_ref, w2_ref, b2_ref, w3_ref, dl_ref, k_ref, ss_ref):
    hi = functools.partial(jnp.dot, precision=HIGHEST, preferred_element_type=F32)
    t = tw_ref[:, 0:1]
    w = tw_ref[:, 1:2]
    lane = lax.broadcasted_iota(jnp.int32, (t.shape[0], LANES), 1)
    fw = w * bands_ref[...]
    z = jnp.where(lane == 0, t,
                  jnp.where(lane <= FILTER_BANDS, jnp.cos(fw),
                            jnp.where(lane <= 2 * FILTER_BANDS, -jnp.sin(fw), 0.0)))
    fr = fr_ref[...]
    h = jnp.sin(fr * (hi(z, w1_ref[...]) + b1_ref[...]))
    h = jnp.sin(fr * (hi(h, w2_ref[...]) + b2_ref[...]))
    decay = jnp.exp(-t * dl_ref[...])
    C = decay.shape[1]
    ss = []
    for g in range(k_ref.shape[1] // C):
        kg = hi(h, w3_ref[:, g * C:(g + 1) * C]) * decay
        k_ref[:, g * C:(g + 1) * C] = kg
        ss.append(jnp.sum(kg * kg, axis=0, keepdims=True))
    ss = jnp.concatenate(ss, axis=-1)

    @pl.when(pl.program_id(0) == 0)
    def _first():
        ss_ref[...] = ss

    @pl.when(pl.program_id(0) > 0)
    def _rest():
        ss_ref[...] += ss


def _hyena_filters(L, lp):
    C = D_HYENA
    pos = jnp.arange(L, dtype=F32)
    tw = jnp.stack([pos / max(L - 1, 1), 2 * math.pi * pos / L], axis=-1)
    bands = jnp.linspace(1e-4, FILTER_BANDS - 1, FILTER_BANDS, dtype=F32)
    bands128 = jnp.zeros((1, LANES), F32).at[0, 1:1 + 2 * FILTER_BANDS].set(jnp.concatenate([bands, bands]))
    w1 = jnp.zeros((LANES, lp['f_w1'].shape[1]), F32).at[:lp['f_w1'].shape[0]].set(lp['f_w1'])
    deltas = jnp.abs(jnp.linspace(math.log(DECAY_TARGET) / DECAY_PCT_LONG,
                                  math.log(DECAY_TARGET) / DECAY_PCT_SHORT, C, dtype=F32))[None]
    nk = lp['f_w3'].shape[1]
    tl = min(L, 256)
    args = [tw, bands128, w1, lp['f_b1'], lp['f_freq'], lp['f_w2'], lp['f_b2'], lp['f_w3'], deltas]
    const = lambda a: pl.BlockSpec(a.shape, lambda i: (0,) * a.ndim)
    k_un, ss = pl.pallas_call(
        _filter_body,
        grid=(L // tl,),
        in_specs=[pl.BlockSpec((tl, 2), lambda i: (i, 0))] + [const(a) for a in args[1:]],
        out_specs=[pl.BlockSpec((tl, nk), lambda i: (i, 0)), pl.BlockSpec((1, nk), lambda i: (0, 0))],
        out_shape=[jax.ShapeDtypeStruct((L, nk), F32), jax.ShapeDtypeStruct((1, nk), F32)],
        name="hyena_filters",
        compiler_params=_params("arbitrary"),
    )(*args)
    ss = ss.reshape(HYENA_ORDER, 2, C)
    scale = lax.rsqrt(ss[:, 0] + ss[:, 1] + EPS).reshape(1, HYENA_ORDER * C)
    return k_un, scale


def _short_conv_chunk(u_ref, r0, n, prev_last, next_first, w, b):
    u = u_ref[pl.ds(r0, n), :]
    row = lax.broadcasted_iota(jnp.int32, u.shape, 0)
    up = jnp.where(row == 0, prev_last, pltpu.roll(u, 1, axis=0))
    un = jnp.where(row == n - 1, next_first, pltpu.roll(u, n - 1, axis=0))
    return up * w[0:1] + u * w[1:2] + un * w[2:3] + b


def _dft_spectrum_body(a_ref, b_ref, s_ref, o_ref):
    o_ref[...] = jnp.dot(a_ref[...], b_ref[...], precision=HIGHEST, preferred_element_type=F32) * s_ref[...]


def _dft_spectrum(a, b, scale):
    M, K = a.shape
    ncol = b.shape[1]
    tn = min(ncol, 512)
    return pl.pallas_call(
        _dft_spectrum_body,
        grid=(ncol // tn,),
        in_specs=[pl.BlockSpec((M, K), lambda j: (0, 0)), pl.BlockSpec((K, tn), lambda j: (0, j)),
                  pl.BlockSpec((1, tn), lambda j: (0, j))],
        out_specs=pl.BlockSpec((M, tn), lambda j: (0, j)),
        out_shape=jax.ShapeDtypeStruct((M, ncol), F32),
        name="dft_spectrum",
        compiler_params=_params("arbitrary"),
    )(a, b, scale)


def _hyena_short_body(v_ref, x1_ref, x2_ref, wv_ref, w1_ref, w2_ref, bv_ref, b1_ref, b2_ref, fb_ref, h0_ref, h1_ref,
                      ff_ref, if_ref, o_ref):
    L = v_ref.shape[0]
    N = ff_ref.shape[0] // 2
    conv = lambda u_ref, w_ref, b_ref: _short_conv_chunk(u_ref, 0, L, 0.0, 0.0, w_ref[...], b_ref[...])
    z = conv(v_ref, wv_ref, bv_ref)
    gates = (conv(x1_ref, w1_ref, b1_ref), conv(x2_ref, w2_ref, b2_ref))
    for o, (gate, h_ref) in enumerate(zip(gates, (h0_ref, h1_ref))):
        X = _dot(ff_ref[...], z.astype(BF16))
        xr, xi = X[:N], X[N:]
        hr, hi = h_ref[:N, :], h_ref[N:, :]
        Y = jnp.concatenate([xr * hr - xi * hi, xr * hi + xi * hr], axis=0)
        y = _dot(if_ref[...], Y.astype(BF16))
        z = gate * (y + fb_ref[o:o + 1, :] * z)
    o_ref[...] = z


def _hyena_short(hy, lp, k_un, scale):
    B, L, _ = hy.shape
    C = D_HYENA
    N = 2 * L
    ct = 256
    ncb = C // ct
    k4 = k_un.reshape(L, HYENA_ORDER, 2, C)
    taps = jnp.concatenate([k4[:, :, 0], jnp.zeros((1, HYENA_ORDER, C), F32), k4[:0:-1, :, 1]], axis=0)
    ang = 2 * np.pi * np.outer(np.arange(N), np.arange(N)) / N
    dft = jnp.asarray(np.concatenate([np.cos(ang), -np.sin(ang)], axis=0), F32)
    H = _dft_spectrum(dft, taps.reshape(N, HYENA_ORDER * C), scale)
    fwd = jnp.asarray(np.concatenate([np.cos(ang[:, :L]), -np.sin(ang[:, :L])], axis=0), BF16)
    inv = jnp.asarray(np.concatenate([np.cos(ang[:L]), -np.sin(ang[:L])], axis=1) / N, BF16)
    u = lambda g: pl.BlockSpec((None, L, ct), lambda c, b: (b, 0, g * ncb + c))
    cw = lambda g: pl.BlockSpec((3, ct), lambda c, b: (0, g * ncb + c))
    cb = lambda g: pl.BlockSpec((1, ct), lambda c, b: (0, g * ncb + c))
    const = lambda a: pl.BlockSpec(a.shape, lambda c, b: (0,) * a.ndim)
    return pl.pallas_call(
        _hyena_short_body,
        grid=(ncb, B),
        in_specs=[u(0), u(1), u(2), cw(0), cw(1), cw(2), cb(0), cb(1), cb(2),
                  pl.BlockSpec((HYENA_ORDER, ct), lambda c, b: (0, c)),
                  pl.BlockSpec((2 * N, ct), lambda c, b: (0, c)),
                  pl.BlockSpec((2 * N, ct), lambda c, b: (0, ncb + c)),
                  const(fwd), const(inv)],
        out_specs=pl.BlockSpec((None, L, ct), lambda c, b: (b, 0, c)),
        out_shape=jax.ShapeDtypeStruct((B, L, C), F32),
        name="hyena_short",
        compiler_params=_params("arbitrary", "arbitrary"),
    )(hy, hy, hy, lp['conv_w'], lp['conv_w'], lp['conv_w'], lp['conv_b'], lp['conv_b'], lp['conv_b'],
      lp['f_bias'], H, H, fwd, inv)


FFT_N1 = 128


def _fft_dims(L):
    N = 2 * L
    N1 = FFT_N1
    N2 = N // N1
    assert N1 * N2 == N and N2 % SUBLANES == 0
    return N, N1, N2, N2 + SUBLANES, 2 * N1 + SUBLANES


def _fft_tables(L):
    N, N1, N2, _, _ = _fft_dims(L)
    NH = N1 // 2
    n1 = np.arange(NH)
    k1 = np.arange(N1)
    n2 = np.arange(N2)
    th = 2 * np.pi * (n1[None, None, :] * k1[None, :, None] / N1 + n2[:, None, None] * k1[None, :, None] / N)
    g = np.concatenate([np.cos(th), -np.sin(th)], axis=1).reshape(N2 * 2 * N1, NH)
    ig = np.concatenate([np.cos(th), -np.sin(th)], axis=1).transpose(0, 2, 1) / N
    ig = ig.reshape(N2 * NH, 2 * N1)
    ph = 2 * np.pi * np.outer(n2, n2) / N2
    c, s = np.cos(ph), np.sin(ph)
    f2 = np.block([[c, s], [-s, c]])
    if2 = np.block([[c, -s], [s, c]])

    def hilo(a):
        hi = jnp.asarray(a, F32).astype(BF16)
        lo = (jnp.asarray(a, F32) - hi.astype(F32)).astype(BF16)
        return hi, lo

    return {'g': hilo(g), 'ig': hilo(ig), 'f2': hilo(f2), 'if2': hilo(if2)}


def _mm(tab, r0, nrows, x, passes):
    a_hi = tab[0][pl.ds(r0, nrows), :]
    if passes == 1:
        return _dot(a_hi, x.astype(BF16))
    x_hi, x_lo = _split_bf16(x)
    a_lo = tab[1][pl.ds(r0, nrows), :]
    return _dot(a_hi, x_hi) + (_dot(a_hi, x_lo) + _dot(a_lo, x_hi))


FFT_BATCH = 4
FFT_GROUPS = 2
SPECTRUM_PASSES = 1


def _fft_stage1(tbuf, sbuf, g, dims, passes):
    _, N1, N2, P, Q = dims

    def body(i, carry):
        n2s = [i * FFT_BATCH + j for j in range(FFT_BATCH)]
        xs = [tbuf[pl.ds(n2, N1 // 2, stride=P), :] for n2 in n2s]
        outs = [_mm(g, pl.multiple_of(n2 * 2 * N1, 2 * N1), 2 * N1, x, passes) for n2, x in zip(n2s, xs)]
        for n2, out in zip(n2s, outs):
            sbuf[pl.ds(pl.multiple_of(n2 * Q, SUBLANES), 2 * N1), :] = out
        return carry

    lax.fori_loop(0, N2 // FFT_BATCH, body, 0)


def _fft_stage2_load(sbuf, k1, dims):
    _, N1, N2, _, Q = dims
    re = [sbuf[pl.ds(k1 + j, N2, stride=Q), :] for j in range(FFT_BATCH)]
    im = [sbuf[pl.ds(N1 + k1 + j, N2, stride=Q), :] for j in range(FFT_BATCH)]
    return jnp.concatenate([jnp.concatenate(re, axis=1), jnp.concatenate(im, axis=1)], axis=0)


def _hyena_long_body(v_ref, x1_ref, x2_ref, wv_ref, w1_ref, w2_ref, bv_ref, b1_ref, b2_ref, fb_ref, h0_ref, h1_ref,
                     g_ref, ig_ref, f2_ref, if2_ref, o_ref, zbuf, g1buf, g2buf, sbuf, *, dims):
    N, N1, N2, P, Q = dims
    NH = N1 // 2
    zero = jnp.zeros((1, LANES), F32)

    for u_ref, w_ref, b_ref, buf in ((v_ref, wv_ref, bv_ref, zbuf), (x1_ref, w1_ref, b1_ref, g1buf),
                                     (x2_ref, w2_ref, b2_ref, g2buf)):
        w, b = w_ref[...], b_ref[...]
        for n1 in range(NH):
            r0 = n1 * N2
            prev_last = zero if n1 == 0 else u_ref[r0 - 1:r0, :]
            next_first = zero if n1 == NH - 1 else u_ref[r0 + N2:r0 + N2 + 1, :]
            buf[n1 * P:n1 * P + N2, :] = _short_conv_chunk(u_ref, r0, N2, prev_last, next_first, w, b)

    for o, (gbuf, h_ref) in enumerate(((g1buf, h0_ref), (g2buf, h1_ref))):
        _fft_stage1(zbuf, sbuf, (g_ref, None), dims, 1)

        def per_k1(i, carry):
            k1s = [(i * FFT_GROUPS + g) * FFT_BATCH for g in range(FFT_GROUPS)]
            Bs = [_fft_stage2_load(sbuf, k1, dims).astype(BF16) for k1 in k1s]
            Ccs = []
            for k1, B in zip(k1s, Bs):
                X = _dot(f2_ref[...], B)
                h0 = [pl.multiple_of((k1 + j) * 2 * N2, 2 * N2) for j in range(FFT_BATCH)]
                hr = jnp.concatenate([h_ref[pl.ds(r, N2), :] for r in h0], axis=1)
                hi = jnp.concatenate([h_ref[pl.ds(r + N2, N2), :] for r in h0], axis=1)
                xr, xi = X[:N2], X[N2:]
                Y = jnp.concatenate([xr * hr - xi * hi, xr * hi + xi * hr], axis=0)
                Ccs.append(_dot(if2_ref[...], Y.astype(BF16)))
            for k1, Cc in zip(k1s, Ccs):
                for j in range(FFT_BATCH):
                    lanes = slice(j * LANES, (j + 1) * LANES)
                    sbuf[pl.ds(k1 + j, N2, stride=Q), :] = Cc[:N2, lanes]
                    sbuf[pl.ds(N1 + k1 + j, N2, stride=Q), :] = Cc[N2:, lanes]
            return carry

        lax.fori_loop(0, N1 // (FFT_BATCH * FFT_GROUPS), per_k1, 0)
        fb = fb_ref[o:o + 1, :]

        def per_n2(i, carry):
            n2s = [i * FFT_BATCH + j for j in range(FFT_BATCH)]
            Ds = [sbuf[pl.ds(pl.multiple_of(n2 * Q, SUBLANES), 2 * N1), :].astype(BF16) for n2 in n2s]
            ys = [_dot(ig_ref[pl.ds(pl.multiple_of(n2 * NH, NH), NH), :], D) for n2, D in zip(n2s, Ds)]
            for n2, y in zip(n2s, ys):
                rows = pl.ds(n2, NH, stride=P)
                zbuf[rows, :] = gbuf[rows, :] * (y + fb * zbuf[rows, :])
            return carry

        lax.fori_loop(0, N2 // FFT_BATCH, per_n2, 0)

    for n1 in range(NH):
        o_ref[n1 * N2:(n1 + 1) * N2, :] = zbuf[n1 * P:n1 * P + N2, :]


def _fft_spectrum_body(hf_ref, hb_ref, s_ref, g_hi, g_lo, f2_hi, f2_lo, o_ref, tbuf, sbuf, *, dims):
    N, N1, N2, P, Q = dims
    NH = N1 // 2
    scale = s_ref[...]
    for d, h_ref in enumerate((hf_ref, hb_ref)):
        for n1 in range(NH):
            h = h_ref[n1 * N2:(n1 + 1) * N2, :] * scale
            if d == 1 and n1 == 0:
                h = jnp.where(lax.broadcasted_iota(jnp.int32, h.shape, 0) == 0, 0.0, h)
            tbuf[n1 * P:n1 * P + N2, :] = h
        _fft_stage1(tbuf, sbuf, (g_hi, g_lo), dims, SPECTRUM_PASSES)

        def per_k1(i, carry):
            k1 = i * FFT_BATCH
            X = _mm((f2_hi, f2_lo), 0, 2 * N2, _fft_stage2_load(sbuf, k1, dims), SPECTRUM_PASSES)
            for j in range(FFT_BATCH):
                h0 = pl.multiple_of((k1 + j) * 2 * N2, 2 * N2)
                Xj = X[:, j * LANES:(j + 1) * LANES]
                if d == 0:
                    o_ref[pl.ds(h0, 2 * N2), :] = Xj
                else:
                    o_ref[pl.ds(h0, N2), :] += Xj[:N2]
                    o_ref[pl.ds(h0 + N2, N2), :] -= Xj[N2:]
            return carry

        lax.fori_loop(0, N1 // FFT_BATCH, per_k1, 0)


def _hyena_long(hy, lp, k_un, scale):
    B, L, _ = hy.shape
    C = D_HYENA
    dims = _fft_dims(L)
    N, N1, N2, P, Q = dims
    NH = N1 // 2
    tabs = _fft_tables(L)
    ncb = C // LANES
    nspec = HYENA_ORDER * ncb
    const1 = lambda a, n: pl.BlockSpec(a.shape, (lambda *i: (0,) * a.ndim), pipeline_mode=pl.Buffered(1))
    hcol = lambda d: pl.BlockSpec((L, LANES), lambda j: (0, (j // ncb) * 2 * ncb + d * ncb + j % ncb))
    H = pl.pallas_call(
        functools.partial(_fft_spectrum_body, dims=dims),
        grid=(nspec,),
        in_specs=[hcol(0), hcol(1), pl.BlockSpec((1, LANES), lambda j: (0, j)),
                  const1(tabs['g'][0], 1), const1(tabs['g'][1], 1),
                  const1(tabs['f2'][0], 1), const1(tabs['f2'][1], 1)],
        out_specs=pl.BlockSpec((2 * N, LANES), lambda j: (0, j)),
        out_shape=jax.ShapeDtypeStruct((2 * N, HYENA_ORDER * C), F32),
        scratch_shapes=[pltpu.VMEM((NH * P, LANES), F32), pltpu.VMEM((N2 * Q, LANES), F32)],
        name="filter_spectrum",
        compiler_params=_params("arbitrary"),
    )(k_un, k_un, scale, tabs['g'][0], tabs['g'][1], tabs['f2'][0], tabs['f2'][1])

    one = pl.Buffered(1)
    hspec = lambda o: pl.BlockSpec((2 * N, LANES), lambda c, b: (0, o * ncb + c), pipeline_mode=one)
    u = lambda g: pl.BlockSpec((None, L, LANES), lambda c, b: (b, 0, g * ncb + c), pipeline_mode=one)
    cw = lambda g: pl.BlockSpec((3, LANES), lambda c, b: (0, g * ncb + c))
    cb = lambda g: pl.BlockSpec((1, LANES), lambda c, b: (0, g * ncb + c))
    return pl.pallas_call(
        functools.partial(_hyena_long_body, dims=dims),
        grid=(ncb, B),
        in_specs=[u(0), u(1), u(2), cw(0), cw(1), cw(2), cb(0), cb(1), cb(2),
                  pl.BlockSpec((HYENA_ORDER, LANES), lambda c, b: (0, c)),
                  hspec(0), hspec(1),
                  const1(tabs['g'][0], 2), const1(tabs['ig'][0], 2),
                  const1(tabs['f2'][0], 2), const1(tabs['if2'][0], 2)],
        out_specs=pl.BlockSpec((None, L, LANES), lambda c, b: (b, 0, c)),
        out_shape=jax.ShapeDtypeStruct((B, L, C), F32),
        scratch_shapes=[pltpu.VMEM((NH * P, LANES), F32), pltpu.VMEM((NH * P, LANES), F32),
                        pltpu.VMEM((NH * P, LANES), F32), pltpu.VMEM((N2 * Q, LANES), F32)],
        name="hyena_long",
        compiler_params=_params("arbitrary", "arbitrary"),
    )(hy, hy, hy, lp['conv_w'], lp['conv_w'], lp['conv_w'], lp['conv_b'], lp['conv_b'], lp['conv_b'],
      lp['f_bias'], H, H, tabs['g'][0], tabs['ig'][0], tabs['f2'][0], tabs['if2'][0])


def _hyena(hy, lp):
    L = hy.shape[1]
    k_un, scale = _hyena_filters(L, lp)
    if 2 * L >= 2 * FFT_N1 * SUBLANES and (2 * L) % (FFT_N1 * SUBLANES) == 0:
        return _hyena_long(hy, lp, k_un, scale)
    return _hyena_short(hy, lp, k_un, scale)


def kernel(x_prompt, x_sample, cache_k, cache_v, c, c_ctx, w_mod, b_mod, g_mix, w_in, conv_w, conv_b, f_w1, f_b1,
           f_freq, f_w2, f_b2, f_w3, f_bias, rpb, g_out_hy, g_out_at, w_out, g_ffn, w_router, b_router, w_exp1,
           b_exp1, w_exp2, b_exp2, g_final):
    depth = w_mod.shape[0]
    Bp, Lp, D = x_prompt.shape
    Bs, Ls, _ = x_sample.shape
    n_ctx, n_lat = Bp * Lp, Bs * Ls
    n_tok = n_ctx + n_lat
    d_attn = N_HEADS * HEAD_DIM
    q_off = 3 * D_HYENA
    row2 = lambda a: a.reshape(1, -1)

    cond = jnp.zeros((SUBLANES, D), F32).at[0].set(c_ctx).at[1:1 + Bs].set(c)
    xp, xs = x_prompt, x_sample
    new_k, new_v = [], []
    for l in range(depth):
        wr = jnp.zeros((D, LANES), F32).at[:, :N_EXPERTS].set(w_router[l])
        wr_hi = wr.astype(BF16)
        lp = {
            'conv_w': conv_w[l], 'conv_b': row2(conv_b[l]), 'f_w1': f_w1[l], 'f_b1': row2(f_b1[l]),
            'f_freq': row2(f_freq[l]), 'f_w2': f_w2[l], 'f_b2': row2(f_b2[l]), 'f_w3': f_w3[l],
            'f_bias': f_bias[l], 'g_out_hy': row2(g_out_hy[l]), 'g_out_at': row2(g_out_at[l]),
            'w_out': w_out[l].astype(BF16), 'g_ffn': row2(g_ffn[l]),
            'wr_hi': wr_hi, 'wr_lo': (wr - wr_hi.astype(F32)).astype(BF16),
            'b_router': jnp.zeros((1, LANES), F32).at[0, :N_EXPERTS].set(b_router[l]),
            'w_exp1': w_exp1[l], 'b_exp1': b_exp1[l][:, None, :], 'w_exp2': w_exp2[l],
            'b_exp2': b_exp2[l][:, None, :],
        }
        mod = _modulation(cond, w_mod[l], row2(b_mod[l]))
        mod3 = mod.reshape(SUBLANES, 1, 6 * D)
        w_in_b = w_in[l].astype(BF16)
        g_mix_l = row2(g_mix[l])

        proj_p = _in_projection(xp, mod3, 0, g_mix_l, w_in_b)
        proj_s = _in_projection(xs, mod3, 1, g_mix_l, w_in_b)
        kv = proj_p[..., q_off + d_attn:].reshape(Bp, Lp, 2, N_HEADS, HEAD_DIM).transpose(2, 0, 3, 1, 4)
        new_k.append(kv[0])
        new_v.append(kv[1])

        hy_p = _hyena(proj_p, lp)
        hy_s = _hyena(proj_s, lp)
        at_p = _context_attention(proj_p, q_off)
        heads_last = lambda t: t.transpose(0, 2, 1, 3).reshape(Bs, t.shape[2], d_attn)
        at_s = _neighbourhood_attention(proj_s, q_off, heads_last(cache_k[:, l]), heads_last(cache_v[:, l]), rpb[l])

        x1, h2u, logits = _out_projection((hy_p, at_p, xp), (hy_s, at_s, xs), mod3, lp)

        gates, ce, nv, n_used, gidx, sidx = _routing(logits[:, :N_EXPERTS], n_tok)
        yp = _experts(ce, nv, n_used, gidx, sidx, h2u, lp, n_tok * TOP_K + MOE_CHUNK)
        last = l == depth - 1
        gf = row2(g_final) if last else None
        assert last, "deeper stacks need the un-normalised residual between layers"
        xp = _combine(x1, yp, gates, mod3, 0, gf, 0, Bp, Lp)
        xs = _combine(x1, yp, gates, mod3, 1, gf, n_ctx, Bs, Ls)

    return xp, xs, jnp.stack(new_k, axis=1), jnp.stack(new_v, axis=1)
```

```python
import functools
import math

import numpy as np
import jax
import jax.numpy as jnp
from jax import lax
from jax.experimental import pallas as pl
from jax.experimental.pallas import tpu as pltpu

F32 = jnp.float32
BF16 = jnp.bfloat16
HIGHEST = lax.Precision.HIGHEST

GRID_W = 64
N_HEADS = 16
HEAD_DIM = 64
D_HYENA = 1024
HYENA_ORDER = 2
FILTER_BANDS = 16
DECAY_TARGET = 1e-2
DECAY_PCT_SHORT = 0.3
DECAY_PCT_LONG = 1.5
WIN_H = 8
WIN_W = 16
N_EXPERTS = 32
TOP_K = 4
SWIGLU_ALPHA = 1.702
SWIGLU_LIMIT = 7.0
EPS = 1e-6
NEG_INF = -1e30

LANES = 128
SUBLANES = 8
VMEM_LIMIT = 60 * 1024 * 1024

ROW_TILE = 256
NA_Q_ROWS = 4
MOE_CHUNK = 1024
MOE_SUB = 256
MOE_FF_TILE = 512
MOE_ISSUE = 64


def _params(*sem):
    return pltpu.CompilerParams(dimension_semantics=sem, vmem_limit_bytes=VMEM_LIMIT)


def _rms(x, g):
    return x * lax.rsqrt(jnp.mean(x * x, axis=-1, keepdims=True) + EPS) * g


def _split_bf16(x):
    hi = x.astype(BF16)
    lo = (x - hi.astype(F32)).astype(BF16)
    return hi, lo


def _dot(a, b):
    return jnp.dot(a, b, preferred_element_type=F32)


def _dot_nt(a, b):
    return lax.dot_general(a, b, (((1,), (1,)), ((), ())), preferred_element_type=F32)


def _mod_body(c_ref, w_ref, b_ref, o_ref):
    c = c_ref[...]
    s = c / (1.0 + jnp.exp(-c))
    o_ref[...] = jnp.dot(s, w_ref[...], precision=HIGHEST, preferred_element_type=F32) + b_ref[...]


def _modulation(cc, w, b):
    D, N = w.shape
    tn = min(N, 1536)
    return pl.pallas_call(
        _mod_body,
        grid=(N // tn,),
        in_specs=[pl.BlockSpec((SUBLANES, D), lambda j: (0, 0)),
                  pl.BlockSpec((D, tn), lambda j: (0, j)),
                  pl.BlockSpec((1, tn), lambda j: (0, j))],
        out_specs=pl.BlockSpec((SUBLANES, tn), lambda j: (0, j)),
        out_shape=jax.ShapeDtypeStruct((SUBLANES, N), F32),
        name="modulation",
        compiler_params=_params("arbitrary"),
    )(cc, w, b)


def _inproj_body(x_ref, shift_ref, scale_ref, g_ref, w_ref, o_ref, *, n_chunk):
    h = _rms(x_ref[...], g_ref[...]) * (1.0 + scale_ref[...]) + shift_ref[...]
    hb = h.astype(BF16)
    n_out = o_ref.shape[-1]

    def col(j, carry):
        c0 = pl.multiple_of(j * n_chunk, n_chunk)
        o_ref[:, pl.ds(c0, n_chunk)] = _dot(hb, w_ref[:, pl.ds(c0, n_chunk)])
        return carry

    lax.fori_loop(0, n_out // n_chunk, col, 0)


def _in_projection(x, mod3, row0, g, w_bf16):
    B, L, D = x.shape
    N = w_bf16.shape[1]
    tm = min(ROW_TILE, L)
    mrow = (lambda b: 0) if row0 == 0 else (lambda b: b + row0)
    return pl.pallas_call(
        functools.partial(_inproj_body, n_chunk=512),
        grid=(B, L // tm),
        in_specs=[pl.BlockSpec((None, tm, D), lambda b, i: (b, i, 0)),
                  pl.BlockSpec((None, 1, D), lambda b, i: (mrow(b), 0, 0)),
                  pl.BlockSpec((None, 1, D), lambda b, i: (mrow(b), 0, 1)),
                  pl.BlockSpec((1, D), lambda b, i: (0, 0)),
                  pl.BlockSpec((D, N), lambda b, i: (0, 0), pipeline_mode=pl.Buffered(1))],
        out_specs=pl.BlockSpec((None, tm, N), lambda b, i: (b, i, 0)),
        out_shape=jax.ShapeDtypeStruct((B, L, N), F32),
        name="in_projection",
        compiler_params=_params("arbitrary", "arbitrary"),
    )(x, mod3, mod3, g, w_bf16)


def _head_lane_masks(rows, hd):
    lane = lax.broadcasted_iota(jnp.int32, (rows, LANES), 1)
    return [(lane >= h * hd) & (lane < (h + 1) * hd) for h in range(LANES // hd)]


def _ctx_attn_body(q_ref, k_ref, v_ref, o_ref, *, scale, hd):
    q = q_ref[...]
    k = k_ref[...].astype(BF16)
    v = v_ref[...].astype(BF16)
    masks = _head_lane_masks(q.shape[0], hd)
    out = None
    for msk in masks:
        qh = jnp.where(msk, q, 0.0).astype(BF16)
        s = _dot_nt(qh, k) * scale
        p = jnp.exp(s - jnp.max(s, axis=-1, keepdims=True))
        o = _dot(p.astype(BF16), v) / jnp.sum(p, axis=-1, keepdims=True)
        out = o if out is None else jnp.where(msk, o, out)
    o_ref[...] = out


def _context_attention(proj, q_off):
    B, L, _ = proj.shape
    d_attn = N_HEADS * HEAD_DIM
    nhp = d_attn // LANES
    qb, kb, vb = (q_off // LANES, (q_off + d_attn) // LANES, (q_off + 2 * d_attn) // LANES)
    spec = lambda base: pl.BlockSpec((None, L, LANES), lambda b, h: (b, 0, base + h))
    return pl.pallas_call(
        functools.partial(_ctx_attn_body, scale=1.0 / math.sqrt(HEAD_DIM), hd=HEAD_DIM),
        grid=(B, nhp),
        in_specs=[spec(qb), spec(kb), spec(vb)],
        out_specs=pl.BlockSpec((None, L, LANES), lambda b, h: (b, 0, h)),
        out_shape=jax.ShapeDtypeStruct((B, L, d_attn), F32),
        name="context_attention",
        compiler_params=_params("arbitrary", "arbitrary"),
    )(proj, proj, proj)


def _na_tables(rows, rpb):
    W = GRID_W
    kh = min(WIN_H, rows)
    rbq = NA_Q_ROWS
    kwr = rbq + kh
    assert rows % rbq == 0 and kwr <= rows
    nblk = rows // rbq
    ws = np.clip(np.arange(nblk) * rbq - kh // 2, 0, rows - kwr)
    rq = np.arange(nblk)[:, None, None] * rbq + np.arange(rbq)[None, :, None]
    rk = ws[:, None, None] + np.arange(kwr)[None, None, :]
    rs = np.clip(rq - kh // 2, 0, rows - kh)
    vr = (rk >= rs) & (rk < rs + kh)
    dr = np.where(vr, rk - rq + WIN_H - 1, 0)
    patterns = np.concatenate([dr.reshape(nblk, -1), vr.reshape(nblk, -1)], axis=1)
    _, first, btype = np.unique(patterns, axis=0, return_index=True, return_inverse=True)
    dr_t, vr_t = dr[first], vr[first]
    cq = np.arange(W)[:, None]
    ck = np.arange(W)[None, :]
    cs = np.clip(cq - WIN_W // 2, 0, W - WIN_W)
    vc = (ck >= cs) & (ck < cs + WIN_W)
    dc = np.clip(ck - cq, -(WIN_W - 1), WIN_W - 1) + WIN_W - 1
    nt = dr_t.shape[0]
    rowsel = (dr_t[..., None] == np.arange(2 * WIN_H - 1)) & vr_t[..., None]
    colsel = (dc[..., None] == np.arange(2 * WIN_W - 1)) & vc[..., None]
    vals = jnp.einsum('tajr,hrc,qkc->htaqjk', rowsel.astype(np.float32), rpb.astype(F32),
                      colsel.astype(np.float32), precision=HIGHEST)
    valid = vr_t[:, :, None, :, None] & vc[None, None, :, None, :]
    bias = jnp.where(valid[None], vals, NEG_INF)
    bias = bias.reshape(rpb.shape[0], nt, rbq * W, kwr * W)
    return ws.astype(np.int32), btype.reshape(-1).astype(np.int32), bias


def _na_body(ws_ref, bt_ref, q_ref, k_ref, v_ref, kc_ref, vc_ref, bias_ref, o_ref, *, scale, hd, nk):
    i = pl.program_id(2)
    start = pl.multiple_of(ws_ref[i] * GRID_W, GRID_W)
    q = q_ref[...] * scale
    kl = k_ref[pl.ds(start, nk), :].astype(BF16)
    vl = v_ref[pl.ds(start, nk), :]
    kc = kc_ref[...].astype(BF16)
    vc = vc_ref[...]
    masks = _head_lane_masks(q.shape[0], hd)
    vl_masks = _head_lane_masks(nk, hd)
    vc_masks = _head_lane_masks(vc.shape[0], hd)
    assert len(masks) == 2, "the value lanes of the other head carry the softmax denominator"
    out = None
    for h, msk in enumerate(masks):
        qh = jnp.where(msk, q, 0.0).astype(BF16)
        sl = _dot_nt(qh, kl) + bias_ref[h]
        sc = _dot_nt(qh, kc)
        m = jnp.maximum(jnp.max(sl, axis=-1, keepdims=True), jnp.max(sc, axis=-1, keepdims=True))
        p_l = jnp.exp(sl - m).astype(BF16)
        p_c = jnp.exp(sc - m).astype(BF16)
        o = (_dot(p_l, jnp.where(vl_masks[h], vl, 1.0).astype(BF16))
             + _dot(p_c, jnp.where(vc_masks[h], vc, 1.0).astype(BF16)))
        o = o / pltpu.roll(o, hd, axis=1)
        out = o if out is None else jnp.where(msk, o, out)
    o_ref[...] = out


def _neighbourhood_attention(proj, q_off, kc, vc, rpb):
    B, L, _ = proj.shape
    Lc = kc.shape[1]
    d_attn = N_HEADS * HEAD_DIM
    hpb = LANES // HEAD_DIM
    nhp = d_attn // LANES
    rows = L // GRID_W
    ws, btype, bias = _na_tables(rows, rpb)
    nq = NA_Q_ROWS * GRID_W
    nk = (NA_Q_ROWS + min(WIN_H, rows)) * GRID_W
    nblk = rows // NA_Q_ROWS
    qb, kb, vb = (q_off // LANES, (q_off + d_attn) // LANES, (q_off + 2 * d_attn) // LANES)
    full = lambda base: pl.BlockSpec((None, L, LANES), lambda b, h, i, ws_r, bt_r: (b, 0, base + h))
    ctx = pl.BlockSpec((None, Lc, LANES), lambda b, h, i, ws_r, bt_r: (b, 0, h))
    grid_spec = pltpu.PrefetchScalarGridSpec(
        num_scalar_prefetch=2,
        grid=(B, nhp, nblk),
        in_specs=[pl.BlockSpec((None, nq, LANES), lambda b, h, i, ws_r, bt_r: (b, i, qb + h)),
                  full(kb), full(vb), ctx, ctx,
                  pl.BlockSpec((hpb, None, nq, nk), lambda b, h, i, ws_r, bt_r: (h, bt_r[i], 0, 0))],
        out_specs=pl.BlockSpec((None, nq, LANES), lambda b, h, i, ws_r, bt_r: (b, i, h)),
    )
    return pl.pallas_call(
        functools.partial(_na_body, scale=1.0 / math.sqrt(HEAD_DIM), hd=HEAD_DIM, nk=nk),
        grid_spec=grid_spec,
        out_shape=jax.ShapeDtypeStruct((B, L, d_attn), F32),
        name="neighbourhood_attention",
        compiler_params=_params("arbitrary", "arbitrary", "arbitrary"),
    )(jnp.asarray(ws), jnp.asarray(btype), proj, proj, proj, kc, vc, bias)


def _outproj_body(yhp_ref, yap_ref, xp_ref, yhs_ref, yas_ref, xs_ref, gate1_ref, shift2_ref, scale2_ref, ghy_ref,
                  gat_ref, w_ref, gffn_ref, wrh_ref, wrl_ref, br_ref, x1_ref, h2u_ref, lg_ref, *, n_ctx_tiles):
    def run(yh_ref, ya_ref, x_ref):
        cat = jnp.concatenate([_rms(yh_ref[...], ghy_ref[...]), _rms(ya_ref[...], gat_ref[...])], axis=-1)
        mix = _dot(cat.astype(BF16), w_ref[...])
        x1 = x_ref[...] + gate1_ref[...] * mix
        x1_ref[...] = x1
        h2 = _rms(x1, gffn_ref[...]) * (1.0 + scale2_ref[...]) + shift2_ref[...]
        hi, lo = _split_bf16(h2)
        lg_ref[...] = (_dot(hi, wrh_ref[...]) + _dot(lo, wrh_ref[...]) + _dot(hi, wrl_ref[...])) + br_ref[...]
        bits = lax.bitcast_convert_type(hi.astype(F32), jnp.uint32)
        half = bits.shape[-1] // 2
        words = (bits[:, :half] >> 16) | (bits[:, half:] & jnp.uint32(0xFFFF0000))
        per = half // LANES
        for s in range(per):
            h2u_ref[pl.ds(s, words.shape[0], stride=per), :] = words[:, s * LANES:(s + 1) * LANES]

    is_ctx = pl.program_id(0) < n_ctx_tiles
    pl.when(is_ctx)(lambda: run(yhp_ref, yap_ref, xp_ref))
    pl.when(jnp.logical_not(is_ctx))(lambda: run(yhs_ref, yas_ref, xs_ref))


def _out_projection(ctx, lat, mod3, lp):
    Bp, Lp, D = ctx[2].shape
    Bs, Ls, _ = lat[2].shape
    dh, da = ctx[0].shape[-1], ctx[1].shape[-1]
    tm = min(ROW_TILE, Lp, Ls)
    nct, nlt = Bp * Lp // tm, Bs * Ls // tm
    n_tok = (nct + nlt) * tm
    lat_tiles = Ls // tm
    mrow = lambda i: jnp.where(i < nct, 0, 1 + (i - nct) // lat_tiles)
    modspec = lambda c: pl.BlockSpec((None, 1, D), lambda i: (mrow(i), 0, c))
    const = lambda shape: pl.BlockSpec(shape, lambda i: (0,) * len(shape))
    crow = lambda w: pl.BlockSpec((tm, w), lambda i: (jnp.minimum(i, nct - 1), 0))
    lrow = lambda w: pl.BlockSpec((tm, w), lambda i: (jnp.maximum(i - nct, 0), 0))
    flat = lambda t: t.reshape(-1, t.shape[-1])
    return pl.pallas_call(
        functools.partial(_outproj_body, n_ctx_tiles=nct),
        grid=(nct + nlt,),
        in_specs=[crow(dh), crow(da), crow(D), lrow(dh), lrow(da), lrow(D),
                  modspec(2), modspec(3), modspec(4),
                  const((1, dh)), const((1, da)),
                  pl.BlockSpec((dh + da, D), lambda i: (0, 0), pipeline_mode=pl.Buffered(1)),
                  const((1, D)), const((D, LANES)), const((D, LANES)), const((1, LANES))],
        out_specs=[pl.BlockSpec((tm, D), lambda i: (i, 0)),
                   pl.BlockSpec((tm * (D // 2 // LANES), LANES), lambda i: (i, 0)),
                   pl.BlockSpec((tm, LANES), lambda i: (i, 0))],
        out_shape=[jax.ShapeDtypeStruct((n_tok, D), F32),
                   jax.ShapeDtypeStruct((n_tok * (D // 2 // LANES), LANES), jnp.uint32),
                   jax.ShapeDtypeStruct((n_tok, LANES), F32)],
        name="out_projection",
        compiler_params=_params("arbitrary"),
    )(*[flat(t) for t in ctx], *[flat(t) for t in lat], mod3, mod3, mod3, lp['g_out_hy'], lp['g_out_at'],
      lp['w_out'], lp['g_ffn'], lp['wr_hi'], lp['wr_lo'], lp['b_router'])


def _expert_body(ce_ref, nv_ref, nu_ref, gcur_ref, gnext_ref, sprev_ref, scur_ref, h2u_hbm, w1g_ref, w1u_ref,
                 b1g_ref, b1u_ref, w2_ref, b2_ref, yp_hbm, xu, xb, acc, ystage, abuf, gsem, ssem):
    c = pl.program_id(0)
    f = pl.program_id(1)
    nc = pl.num_programs(0)
    nf = pl.num_programs(1)
    half = xb.shape[1] // 2
    xt = half // LANES
    yt = acc.shape[1] // LANES
    slot = c % 2
    subs = lambda n: (n + MOE_SUB - 1) // MOE_SUB
    nv = nv_ref[c]
    nsub = subs(nv)
    nrows = nsub * MOE_SUB
    last = c == nu_ref[0] - 1
    nv_prev = nv_ref[jnp.maximum(c - 1, 0)]
    nrows_prev = subs(nv_prev) * MOE_SUB
    covered_prev = MOE_ISSUE * nf * subs(nv_prev)
    nrows_next = jnp.where(c + 1 < nc, subs(nv_ref[jnp.minimum(c + 1, nc - 1)]), 0) * MOE_SUB
    covered = MOE_ISSUE * nf * nsub

    def gather_row(idx_ref, r, to_slot):
        tok = idx_ref[0, r]
        pltpu.make_async_copy(h2u_hbm.at[pl.ds(pl.multiple_of(tok * xt, xt), xt), :],
                              xu.at[to_slot, pl.ds(pl.multiple_of(r * xt, xt), xt), :], gsem.at[to_slot]).start()

    def scatter_row(dst, r):
        pltpu.make_async_copy(ystage.at[pl.ds(pl.multiple_of(r * yt, yt), yt), :],
                              yp_hbm.at[pl.ds(pl.multiple_of(dst * yt, yt), yt), :], ssem).start()

    def for_rows(lo, hi, fn):
        lax.fori_loop(lo, hi, lambda r, carry: (fn(r), carry)[1], 0)

    def drain(n, src, dst, sem, per):
        piece = lambda m: pltpu.make_async_copy(src.at[pl.ds(0, m * per), :], dst.at[pl.ds(0, m * per), :], sem)
        for_rows(0, n // MOE_SUB, lambda i: piece(MOE_SUB).wait())
        bit = MOE_SUB // 2
        while bit:
            if not isinstance(n, int):
                pl.when((n & bit) != 0)(piece(bit).wait)
            elif n & bit:
                piece(bit).wait()
            bit //= 2

    @pl.when(c < nu_ref[0])
    def _chunk():
        @pl.when(f == 0)
        def _arrive():
            @pl.when(c == 0)
            def _first():
                for_rows(0, nrows, lambda r: gather_row(gcur_ref, r, 0))

            started = jnp.where(c == 0, nrows, jnp.maximum(covered_prev, nrows))
            drain(started, h2u_hbm, xu.at[slot], gsem.at[slot], xt)

            def unpack(sb, carry):
                r0 = pl.multiple_of(sb * MOE_SUB, MOE_SUB)
                for s in range(xt):
                    u = xu[slot, pl.ds(r0 * xt + s, MOE_SUB, stride=xt), :]
                    lo = lax.bitcast_convert_type(u << 16, F32)
                    hi = lax.bitcast_convert_type(u & jnp.uint32(0xFFFF0000), F32)
                    xb[pl.ds(r0, MOE_SUB), s * LANES:(s + 1) * LANES] = lo.astype(BF16)
                    xb[pl.ds(r0, MOE_SUB), half + s * LANES:half + (s + 1) * LANES] = hi.astype(BF16)
                acc[pl.ds(r0, MOE_SUB), :] = jnp.zeros((MOE_SUB, acc.shape[1]), F32)
                return carry

            lax.fori_loop(0, nsub, unpack, 0)

        def matmul_steps(with_scatter):
            def up_proj(sb):
                base = (f * nsub + sb) * MOE_ISSUE
                for j in range(MOE_ISSUE):
                    gather_row(gnext_ref, base + j, 1 - slot)
                if with_scatter:
                    for j in range(MOE_ISSUE):
                        scatter_row(sprev_ref[0, base + j], jnp.minimum(base + j, nrows_prev - 1))
                x = xb[pl.ds(pl.multiple_of(sb * MOE_SUB, MOE_SUB), MOE_SUB), :]
                gate = jnp.minimum(_dot(x, w1g_ref[...].astype(BF16)) + b1g_ref[...], SWIGLU_LIMIT)
                up = jnp.clip(_dot(x, w1u_ref[...].astype(BF16)) + b1u_ref[...], -SWIGLU_LIMIT, SWIGLU_LIMIT)
                glu = gate / (1.0 + jnp.exp(-SWIGLU_ALPHA * gate))
                abuf[...] = ((up + 1.0) * glu).astype(BF16)

            def down_proj(sb):
                acc[pl.ds(pl.multiple_of(sb * MOE_SUB, MOE_SUB), MOE_SUB), :] += _dot(abuf[...], w2_ref[...].astype(BF16))

            def step(sb, carry):
                down_proj(sb - 1)
                up_proj(sb)
                return carry

            up_proj(0)
            lax.fori_loop(1, nsub, step, 0)
            down_proj(nsub - 1)

        pl.when(c == 0)(lambda: matmul_steps(False))
        pl.when(c > 0)(lambda: matmul_steps(True))

        @pl.when(f == nf - 1)
        def _leave():
            for_rows(covered, nrows_next, lambda r: gather_row(gnext_ref, r, 1 - slot))

            @pl.when(c > 0)
            def _prev_out():
                for_rows(covered, nv_prev, lambda r: scatter_row(sprev_ref[0, r], r))
                drain(jnp.maximum(covered, nv_prev), yp_hbm, ystage, ssem, yt)

            def stage(sb, carry):
                r0 = pl.multiple_of(sb * MOE_SUB, MOE_SUB)
                for s in range(yt):
                    cols = slice(s * LANES, (s + 1) * LANES)
                    ystage[pl.ds(r0 * yt + s, MOE_SUB, stride=yt), :] = acc[pl.ds(r0, MOE_SUB), cols] + b2_ref[:, cols]
                return carry

            lax.fori_loop(0, nsub, stage, 0)

            @pl.when(last)
            def _flush():
                for_rows(0, nv, lambda r: scatter_row(scur_ref[0, r], r))
                spare0 = yp_hbm.shape[0] // yt - xb.shape[0]
                for_rows(0, xb.shape[0], lambda r: scatter_row(spare0 + r, jnp.minimum(r, nrows - 1)))
                drain(nv, yp_hbm, ystage, ssem, yt)
                drain(xb.shape[0], yp_hbm, ystage, ssem, yt)
                drain(jnp.maximum(covered, nrows_next), h2u_hbm, xu.at[1 - slot], gsem.at[1 - slot], xt)


def _experts(chunk_e, chunk_nv, n_used, gidx, sidx, h2u, lp, n_out_rows):
    E, D, two_ff = lp['w_exp1'].shape
    d_ff = two_ff // 2
    nc = gidx.shape[0]
    R, tf = MOE_CHUNK, MOE_FF_TILE
    nf = d_ff // tf
    assert MOE_ISSUE * nf * (R // MOE_SUB) <= R, "a chunk's matmul steps must not start more rows than a chunk holds"

    def ff(c, f, nu):
        return jnp.where(c < nu[0], f, nf - 1)

    smem = lambda step: pl.BlockSpec((None, 1, R), lambda c, f, ce, nv, nu: (jnp.clip(c + step, 0, nc - 1), 0, 0),
                                     memory_space=pltpu.SMEM)
    grid_spec = pltpu.PrefetchScalarGridSpec(
        num_scalar_prefetch=3,
        grid=(nc, nf),
        in_specs=[smem(0), smem(1), smem(-1), smem(0),
                  pl.BlockSpec(memory_space=pl.ANY),
                  pl.BlockSpec((None, D, tf), lambda c, f, ce, nv, nu: (ce[c], 0, ff(c, f, nu))),
                  pl.BlockSpec((None, D, tf), lambda c, f, ce, nv, nu: (ce[c], 0, nf + ff(c, f, nu))),
                  pl.BlockSpec((None, 1, tf), lambda c, f, ce, nv, nu: (ce[c], 0, ff(c, f, nu))),
                  pl.BlockSpec((None, 1, tf), lambda c, f, ce, nv, nu: (ce[c], 0, nf + ff(c, f, nu))),
                  pl.BlockSpec((None, tf, D), lambda c, f, ce, nv, nu: (ce[c], ff(c, f, nu), 0)),
                  pl.BlockSpec((None, 1, D), lambda c, f, ce, nv, nu: (ce[c], 0, 0))],
        out_specs=pl.BlockSpec(memory_space=pl.ANY),
        scratch_shapes=[pltpu.VMEM((2, R * (D // 2 // LANES), LANES), jnp.uint32),
                        pltpu.VMEM((R, D), BF16),
                        pltpu.VMEM((R, D), F32),
                        pltpu.VMEM((R * (D // LANES), LANES), F32),
                        pltpu.VMEM((MOE_SUB, tf), BF16),
                        pltpu.SemaphoreType.DMA((2,)),
                        pltpu.SemaphoreType.DMA(())],
    )
    return pl.pallas_call(
        _expert_body,
        grid_spec=grid_spec,
        out_shape=jax.ShapeDtypeStruct((n_out_rows * (D // LANES), LANES), F32),
        name="experts",
        compiler_params=pltpu.CompilerParams(dimension_semantics=("arbitrary", "arbitrary"),
                                             vmem_limit_bytes=VMEM_LIMIT, has_side_effects=True,
                                             disable_bounds_checks=True),
    )(chunk_e, chunk_nv, n_used, gidx, gidx, sidx, sidx, h2u, lp['w_exp1'], lp['w_exp1'], lp['b_exp1'],
      lp['b_exp1'], lp['w_exp2'], lp['b_exp2'])


def _routing(logits, n_tok):
    R = MOE_CHUNK
    top_v, top_i = lax.top_k(logits, TOP_K)
    gates = jax.nn.softmax(top_v, axis=-1)
    n_pairs = n_tok * TOP_K
    flat_e = top_i.reshape(n_pairs).astype(jnp.int32)
    experts = jnp.arange(N_EXPERTS, dtype=jnp.int32)
    counts = jnp.sum((flat_e[:, None] == experts[None]).astype(jnp.int32), axis=0)
    padded = (counts + R - 1) // R * R
    pad_end = jnp.cumsum(padded)
    pad_start = pad_end - padded
    nc = n_pairs // R + N_EXPERTS
    pair = jnp.arange(n_pairs, dtype=jnp.int32)
    fill_e = jnp.repeat(experts, R)
    fill_i = jnp.tile(jnp.arange(R, dtype=jnp.int32), N_EXPERTS)
    fill_key = jnp.where(fill_i < jnp.repeat(padded - counts, R), 2 * fill_e + 1, 2 * N_EXPERTS)
    _, pid = lax.sort((jnp.concatenate([2 * flat_e, fill_key]), jnp.concatenate([pair, n_pairs + fill_i])),
                      num_keys=1)
    real = pid < n_pairs
    gidx = jnp.where(real, pid // TOP_K, 0)
    sidx = jnp.where(real, (pid % TOP_K) * n_tok + pid // TOP_K, pid)
    n_used = (pad_end[-1] // R).astype(jnp.int32)
    cstart = jnp.arange(nc, dtype=jnp.int32) * R
    ce = jnp.minimum(jnp.searchsorted(pad_end, cstart, side='right'), N_EXPERTS - 1).astype(jnp.int32)
    ce = jnp.where(jnp.arange(nc) < n_used, ce, ce[jnp.maximum(n_used - 1, 0)])
    nv = jnp.clip(counts[ce] - (cstart - pad_start[ce]), 0, R).astype(jnp.int32)
    nv = jnp.where(jnp.arange(nc) < n_used, nv, 0)
    return gates, ce, nv, n_used.reshape(1), gidx.reshape(nc, 1, R), sidx.reshape(nc, 1, R)


def _combine_body(x1_ref, y0_ref, y1_ref, y2_ref, y3_ref, g_ref, gate2_ref, gf_ref, o_ref):
    g = g_ref[...]
    tm, D = x1_ref.shape
    per = D // LANES

    def rows(y_ref):
        return jnp.concatenate([y_ref[pl.ds(s, tm, stride=per), :] for s in range(per)], axis=-1)

    ff = (g[:, 0:1] * rows(y0_ref) + g[:, 1:2] * rows(y1_ref)) + (g[:, 2:3] * rows(y2_ref) + g[:, 3:4] * rows(y3_ref))
    x2 = x1_ref[...] + gate2_ref[...] * ff
    o_ref[...] = _rms(x2, gf_ref[...])


def _combine(x1, yp, gates, mod3, row0, g_final, tok0, B, L):
    n_tok, D = x1.shape
    tm = min(ROW_TILE, L)
    nl = L // tm
    blk0 = tok0 // tm
    nblk_tok = n_tok // tm
    mrow = (lambda b: 0) if row0 == 0 else (lambda b: b + row0)
    row = lambda b, i: (blk0 + b * nl + i, 0)
    yspec = lambda k: pl.BlockSpec((tm * (D // LANES), LANES), lambda b, i: (k * nblk_tok + blk0 + b * nl + i, 0))
    return pl.pallas_call(
        _combine_body,
        grid=(B, nl),
        in_specs=[pl.BlockSpec((tm, D), row), yspec(0), yspec(1), yspec(2), yspec(3),
                  pl.BlockSpec((tm, TOP_K), row),
                  pl.BlockSpec((None, 1, D), lambda b, i: (mrow(b), 0, 5)),
                  pl.BlockSpec((1, D), lambda b, i: (0, 0))],
        out_specs=pl.BlockSpec((None, tm, D), lambda b, i: (b, i, 0)),
        out_shape=jax.ShapeDtypeStruct((B, L, D), F32),
        name="combine",
        compiler_params=_params("arbitrary", "arbitrary"),
    )(x1, yp, yp, yp, yp, gates, mod3, g_final)


def _filter_body(tw_ref, bands_ref, w1_ref, b1_ref, fr_ref, w2_ref, b2_ref, w3_ref, dl_ref, k_ref, ss_ref):
    hi = functools.partial(jnp.dot, precision=HIGHEST, preferred_element_type=F32)
    t = tw_ref[:, 0:1]
    w = tw_ref[:, 1:2]
    lane = lax.broadcasted_iota(jnp.int32, (t.shape[0], LANES), 1)
    fw = w * bands_ref[...]
    z = jnp.where(lane == 0, t,
                  jnp.where(lane <= FILTER_BANDS, jnp.cos(fw),
                            jnp.where(lane <= 2 * FILTER_BANDS, -jnp.sin(fw), 0.0)))
    fr = fr_ref[...]
    h = jnp.sin(fr * (hi(z, w1_ref[...]) + b1_ref[...]))
    h = jnp.sin(fr * (hi(h, w2_ref[...]) + b2_ref[...]))
    decay = jnp.exp(-t * dl_ref[...])
    C = decay.shape[1]
    ss = []
    for g in range(k_ref.shape[1] // C):
        kg = hi(h, w3_ref[:, g * C:(g + 1) * C]) * decay
        k_ref[:, g * C:(g + 1) * C] = kg
        ss.append(jnp.sum(kg * kg, axis=0, keepdims=True))
    ss = jnp.concatenate(ss, axis=-1)

    @pl.when(pl.program_id(0) == 0)
    def _first():
        ss_ref[...] = ss

    @pl.when(pl.program_id(0) > 0)
    def _rest():
        ss_ref[...] += ss


def _hyena_filters(L, lp):
    C = D_HYENA
    pos = jnp.arange(L, dtype=F32)
    tw = jnp.stack([pos / max(L - 1, 1), 2 * math.pi * pos / L], axis=-1)
    bands = jnp.linspace(1e-4, FILTER_BANDS - 1, FILTER_BANDS, dtype=F32)
    bands128 = jnp.zeros((1, LANES), F32).at[0, 1:1 + 2 * FILTER_BANDS].set(jnp.concatenate([bands, bands]))
    w1 = jnp.zeros((LANES, lp['f_w1'].shape[1]), F32).at[:lp['f_w1'].shape[0]].set(lp['f_w1'])
    deltas = jnp.abs(jnp.linspace(math.log(DECAY_TARGET) / DECAY_PCT_LONG,
                                  math.log(DECAY_TARGET) / DECAY_PCT_SHORT, C, dtype=F32))[None]
    nk = lp['f_w3'].shape[1]
    tl = min(L, 256)
    args = [tw, bands128, w1, lp['f_b1'], lp['f_freq'], lp['f_w2'], lp['f_b2'], lp['f_w3'], deltas]
    const = lambda a: pl.BlockSpec(a.shape, lambda i: (0,) * a.ndim)
    k_un, ss = pl.pallas_call(
        _filter_body,
        grid=(L // tl,),
        in_specs=[pl.BlockSpec((tl, 2), lambda i: (i, 0))] + [const(a) for a in args[1:]],
        out_specs=[pl.BlockSpec((tl, nk), lambda i: (i, 0)), pl.BlockSpec((1, nk), lambda i: (0, 0))],
        out_shape=[jax.ShapeDtypeStruct((L, nk), F32), jax.ShapeDtypeStruct((1, nk), F32)],
        name="hyena_filters",
        compiler_params=_params("arbitrary"),
    )(*args)
    ss = ss.reshape(HYENA_ORDER, 2, C)
    scale = lax.rsqrt(ss[:, 0] + ss[:, 1] + EPS).reshape(1, HYENA_ORDER * C)
    return k_un, scale


def _short_conv_chunk(u_ref, r0, n, prev_last, next_first, w, b):
    u = u_ref[pl.ds(r0, n), :]
    row = lax.broadcasted_iota(jnp.int32, u.shape, 0)
    up = jnp.where(row == 0, prev_last, pltpu.roll(u, 1, axis=0))
    un = jnp.where(row == n - 1, next_first, pltpu.roll(u, n - 1, axis=0))
    return up * w[0:1] + u * w[1:2] + un * w[2:3] + b


def _dft_spectrum_body(a_ref, b_ref, s_ref, o_ref):
    o_ref[...] = jnp.dot(a_ref[...], b_ref[...], precision=HIGHEST, preferred_element_type=F32) * s_ref[...]


def _dft_spectrum(a, b, scale):
    M, K = a.shape
    ncol = b.shape[1]
    tn = min(ncol, 512)
    return pl.pallas_call(
        _dft_spectrum_body,
        grid=(ncol // tn,),
        in_specs=[pl.BlockSpec((M, K), lambda j: (0, 0)), pl.BlockSpec((K, tn), lambda j: (0, j)),
                  pl.BlockSpec((1, tn), lambda j: (0, j))],
        out_specs=pl.BlockSpec((M, tn), lambda j: (0, j)),
        out_shape=jax.ShapeDtypeStruct((M, ncol), F32),
        name="dft_spectrum",
        compiler_params=_params("arbitrary"),
    )(a, b, scale)


def _hyena_short_body(v_ref, x1_ref, x2_ref, wv_ref, w1_ref, w2_ref, bv_ref, b1_ref, b2_ref, fb_ref, h0_ref, h1_ref,
                      ff_ref, if_ref, o_ref):
    L = v_ref.shape[0]
    N = ff_ref.shape[0] // 2
    conv = lambda u_ref, w_ref, b_ref: _short_conv_chunk(u_ref, 0, L, 0.0, 0.0, w_ref[...], b_ref[...])
    z = conv(v_ref, wv_ref, bv_ref)
    gates = (conv(x1_ref, w1_ref, b1_ref), conv(x2_ref, w2_ref, b2_ref))
    for o, (gate, h_ref) in enumerate(zip(gates, (h0_ref, h1_ref))):
        X = _dot(ff_ref[...], z.astype(BF16))
        xr, xi = X[:N], X[N:]
        hr, hi = h_ref[:N, :], h_ref[N:, :]
        Y = jnp.concatenate([xr * hr - xi * hi, xr * hi + xi * hr], axis=0)
        y = _dot(if_ref[...], Y.astype(BF16))
        z = gate * (y + fb_ref[o:o + 1, :] * z)
    o_ref[...] = z


def _hyena_short(hy, lp, k_un, scale):
    B, L, _ = hy.shape
    C = D_HYENA
    N = 2 * L
    ct = 256
    ncb = C // ct
    k4 = k_un.reshape(L, HYENA_ORDER, 2, C)
    taps = jnp.concatenate([k4[:, :, 0], jnp.zeros((1, HYENA_ORDER, C), F32), k4[:0:-1, :, 1]], axis=0)
    ang = 2 * np.pi * np.outer(np.arange(N), np.arange(N)) / N
    dft = jnp.asarray(np.concatenate([np.cos(ang), -np.sin(ang)], axis=0), F32)
    H = _dft_spectrum(dft, taps.reshape(N, HYENA_ORDER * C), scale)
    fwd = jnp.asarray(np.concatenate([np.cos(ang[:, :L]), -np.sin(ang[:, :L])], axis=0), BF16)
    inv = jnp.asarray(np.concatenate([np.cos(ang[:L]), -np.sin(ang[:L])], axis=1) / N, BF16)
    u = lambda g: pl.BlockSpec((None, L, ct), lambda c, b: (b, 0, g * ncb + c))
    cw = lambda g: pl.BlockSpec((3, ct), lambda c, b: (0, g * ncb + c))
    cb = lambda g: pl.BlockSpec((1, ct), lambda c, b: (0, g * ncb + c))
    const = lambda a: pl.BlockSpec(a.shape, lambda c, b: (0,) * a.ndim)
    return pl.pallas_call(
        _hyena_short_body,
        grid=(ncb, B),
        in_specs=[u(0), u(1), u(2), cw(0), cw(1), cw(2), cb(0), cb(1), cb(2),
                  pl.BlockSpec((HYENA_ORDER, ct), lambda c, b: (0, c)),
                  pl.BlockSpec((2 * N, ct), lambda c, b: (0, c)),
                  pl.BlockSpec((2 * N, ct), lambda c, b: (0, ncb + c)),
                  const(fwd), const(inv)],
        out_specs=pl.BlockSpec((None, L, ct), lambda c, b: (b, 0, c)),
        out_shape=jax.ShapeDtypeStruct((B, L, C), F32),
        name="hyena_short",
        compiler_params=_params("arbitrary", "arbitrary"),
    )(hy, hy, hy, lp['conv_w'], lp['conv_w'], lp['conv_w'], lp['conv_b'], lp['conv_b'], lp['conv_b'],
      lp['f_bias'], H, H, fwd, inv)


FFT_N1 = 128
FFT_BATCH = 4
FFT_GROUPS = 2
SPECTRUM_PASSES = 1


def _fft_dims(L):
    N = 2 * L
    N1 = FFT_N1
    N2 = N // N1
    step = FFT_BATCH * FFT_GROUPS
    K1 = (N1 // 2 + 1 + step - 1) // step * step
    assert N1 * N2 == N and N2 % SUBLANES == 0 and K1 % SUBLANES == 0
    return N, N1, N2, N2 + SUBLANES, 2 * K1 + SUBLANES, K1


def _fft_tables(L):
    N, N1, N2, _, _, K1 = _fft_dims(L)
    NH = N1 // 2
    n1 = np.arange(NH)
    k1 = np.arange(K1)
    n2 = np.arange(N2)
    kept = (k1 <= N1 // 2)[None, :, None]
    weight = np.where((k1 == 0) | (k1 == N1 // 2), 1.0, 2.0)[None, :, None] * kept / N
    th = 2 * np.pi * (n1[None, None, :] * k1[None, :, None] / N1 + n2[:, None, None] * k1[None, :, None] / N)
    g = np.concatenate([np.cos(th) * kept, -np.sin(th) * kept], axis=1).reshape(N2 * 2 * K1, NH)
    ig = np.concatenate([np.cos(th) * weight, -np.sin(th) * weight], axis=1).transpose(0, 2, 1)
    ig = ig.reshape(N2 * NH, 2 * K1)
    ph = 2 * np.pi * np.outer(n2, n2) / N2
    c, s = np.cos(ph), np.sin(ph)
    f2 = np.block([[c, s], [-s, c]])
    if2 = np.block([[c, -s], [s, c]])

    def hilo(a):
        hi = jnp.asarray(a, F32).astype(BF16)
        lo = (jnp.asarray(a, F32) - hi.astype(F32)).astype(BF16)
        return hi, lo

    return {'g': hilo(g), 'ig': hilo(ig), 'f2': hilo(f2), 'if2': hilo(if2)}


def _mm(tab, r0, nrows, x, passes):
    a_hi = tab[0][pl.ds(r0, nrows), :]
    if passes == 1:
        return _dot(a_hi, x.astype(BF16))
    x_hi, x_lo = _split_bf16(x)
    a_lo = tab[1][pl.ds(r0, nrows), :]
    return _dot(a_hi, x_hi) + (_dot(a_hi, x_lo) + _dot(a_lo, x_hi))


def _fft_stage1(tbuf, sbuf, g, dims, passes):
    _, N1, N2, P, Q, K1 = dims

    nb = FFT_BATCH * FFT_GROUPS

    def body(i, carry):
        n2s = [i * nb + j for j in range(nb)]
        xs = [tbuf[pl.ds(n2, N1 // 2, stride=P), :] for n2 in n2s]
        outs = [_mm(g, pl.multiple_of(n2 * 2 * K1, 2 * SUBLANES), 2 * K1, x, passes) for n2, x in zip(n2s, xs)]
        for n2, out in zip(n2s, outs):
            sbuf[pl.ds(pl.multiple_of(n2 * Q, SUBLANES), 2 * K1), :] = out
        return carry

    lax.fori_loop(0, N2 // nb, body, 0)


def _fft_stage2_load(sbuf, k1, dims):
    _, N1, N2, _, Q, K1 = dims
    re = [sbuf[pl.ds(k1 + j, N2, stride=Q), :] for j in range(FFT_BATCH)]
    im = [sbuf[pl.ds(K1 + k1 + j, N2, stride=Q), :] for j in range(FFT_BATCH)]
    return jnp.concatenate([jnp.concatenate(re, axis=1), jnp.concatenate(im, axis=1)], axis=0)


def _hyena_long_body(v_ref, x1_ref, x2_ref, wv_ref, w1_ref, w2_ref, bv_ref, b1_ref, b2_ref, fb_ref, h0_ref, h1_ref,
                     g_ref, ig_ref, f2_ref, if2_ref, o_ref, zbuf, g1buf, g2buf, sbuf, *, dims):
    N, N1, N2, P, Q, K1 = dims
    NH = N1 // 2
    zero = jnp.zeros((1, LANES), F32)

    for u_ref, w_ref, b_ref, buf in ((v_ref, wv_ref, bv_ref, zbuf), (x1_ref, w1_ref, b1_ref, g1buf),
                                     (x2_ref, w2_ref, b2_ref, g2buf)):
        w, b = w_ref[...], b_ref[...]
        for n1 in range(NH):
            r0 = n1 * N2
            prev_last = zero if n1 == 0 else u_ref[r0 - 1:r0, :]
            next_first = zero if n1 == NH - 1 else u_ref[r0 + N2:r0 + N2 + 1, :]
            buf[n1 * P:n1 * P + N2, :] = _short_conv_chunk(u_ref, r0, N2, prev_last, next_first, w, b)

    for o, (gbuf, h_ref) in enumerate(((g1buf, h0_ref), (g2buf, h1_ref))):
        _fft_stage1(zbuf, sbuf, (g_ref, None), dims, 1)

        def per_k1(i, carry):
            k1s = [(i * FFT_GROUPS + g) * FFT_BATCH for g in range(FFT_GROUPS)]
            Bs = [_fft_stage2_load(sbuf, k1, dims).astype(BF16) for k1 in k1s]
            Ccs = []
            for k1, B in zip(k1s, Bs):
                X = _dot(f2_ref[...], B)
                h0 = [pl.multiple_of((k1 + j) * 2 * N2, 2 * N2) for j in range(FFT_BATCH)]
                hr = jnp.concatenate([h_ref[pl.ds(r, N2), :] for r in h0], axis=1)
                hi = jnp.concatenate([h_ref[pl.ds(r + N2, N2), :] for r in h0], axis=1)
                xr, xi = X[:N2], X[N2:]
                Y = jnp.concatenate([xr * hr - xi * hi, xr * hi + xi * hr], axis=0)
                Ccs.append(_dot(if2_ref[...], Y.astype(BF16)))
            for k1, Cc in zip(k1s, Ccs):
                for j in range(FFT_BATCH):
                    lanes = slice(j * LANES, (j + 1) * LANES)
                    sbuf[pl.ds(k1 + j, N2, stride=Q), :] = Cc[:N2, lanes]
                    sbuf[pl.ds(K1 + k1 + j, N2, stride=Q), :] = Cc[N2:, lanes]
            return carry

        lax.fori_loop(0, K1 // (FFT_BATCH * FFT_GROUPS), per_k1, 0)
        fb = fb_ref[o:o + 1, :]

        nb = FFT_BATCH * FFT_GROUPS

        def per_n2(i, carry):
            n2s = [i * nb + j for j in range(nb)]
            Ds = [sbuf[pl.ds(pl.multiple_of(n2 * Q, SUBLANES), 2 * K1), :].astype(BF16) for n2 in n2s]
            ys = [_dot(ig_ref[pl.ds(pl.multiple_of(n2 * NH, NH), NH), :], D) for n2, D in zip(n2s, Ds)]
            for n2, y in zip(n2s, ys):
                rows = pl.ds(n2, NH, stride=P)
                zbuf[rows, :] = gbuf[rows, :] * (y + fb * zbuf[rows, :])
            return carry

        lax.fori_loop(0, N2 // nb, per_n2, 0)

    for n1 in range(NH):
        o_ref[n1 * N2:(n1 + 1) * N2, :] = zbuf[n1 * P:n1 * P + N2, :]


def _fft_spectrum_body(hf_ref, hb_ref, s_ref, g_hi, g_lo, f2_hi, f2_lo, o_ref, tbuf, sbuf, *, dims):
    N, N1, N2, P, Q, K1 = dims
    NH = N1 // 2
    scale = s_ref[...]
    for d, h_ref in enumerate((hf_ref, hb_ref)):
        for n1 in range(NH):
            h = h_ref[n1 * N2:(n1 + 1) * N2, :] * scale
            if d == 1 and n1 == 0:
                h = jnp.where(lax.broadcasted_iota(jnp.int32, h.shape, 0) == 0, 0.0, h)
            tbuf[n1 * P:n1 * P + N2, :] = h
        _fft_stage1(tbuf, sbuf, (g_hi, g_lo), dims, SPECTRUM_PASSES)

        def per_k1(i, carry):
            k1 = i * FFT_BATCH
            X = _mm((f2_hi, f2_lo), 0, 2 * N2, _fft_stage2_load(sbuf, k1, dims), SPECTRUM_PASSES)
            for j in range(FFT_BATCH):
                h0 = pl.multiple_of((k1 + j) * 2 * N2, 2 * N2)
                Xj = X[:, j * LANES:(j + 1) * LANES]
                if d == 0:
                    o_ref[pl.ds(h0, 2 * N2), :] = Xj
                else:
                    o_ref[pl.ds(h0, N2), :] += Xj[:N2]
                    o_ref[pl.ds(h0 + N2, N2), :] -= Xj[N2:]
            return carry

        lax.fori_loop(0, K1 // FFT_BATCH, per_k1, 0)


def _hyena_long(hy, lp, k_un, scale):
    B, L, _ = hy.shape
    C = D_HYENA
    dims = _fft_dims(L)
    N, N1, N2, P, Q, K1 = dims
    NH = N1 // 2
    tabs = _fft_tables(L)
    ncb = C // LANES
    nspec = HYENA_ORDER * ncb
    const1 = lambda a, n: pl.BlockSpec(a.shape, (lambda *i: (0,) * a.ndim), pipeline_mode=pl.Buffered(1))
    hcol = lambda d: pl.BlockSpec((L, LANES), lambda j: (0, (j // ncb) * 2 * ncb + d * ncb + j % ncb))
    H = pl.pallas_call(
        functools.partial(_fft_spectrum_body, dims=dims),
        grid=(nspec,),
        in_specs=[hcol(0), hcol(1), pl.BlockSpec((1, LANES), lambda j: (0, j)),
                  const1(tabs['g'][0], 1), const1(tabs['g'][1], 1),
                  const1(tabs['f2'][0], 1), const1(tabs['f2'][1], 1)],
        out_specs=pl.BlockSpec((K1 * 2 * N2, LANES), lambda j: (0, j)),
        out_shape=jax.ShapeDtypeStruct((K1 * 2 * N2, HYENA_ORDER * C), F32),
        scratch_shapes=[pltpu.VMEM((NH * P, LANES), F32), pltpu.VMEM((N2 * Q, LANES), F32)],
        name="filter_spectrum",
        compiler_params=_params("arbitrary"),
    )(k_un, k_un, scale, tabs['g'][0], tabs['g'][1], tabs['f2'][0], tabs['f2'][1])

    one = pl.Buffered(1)
    hspec = lambda o: pl.BlockSpec((K1 * 2 * N2, LANES), lambda c, b: (0, o * ncb + c), pipeline_mode=one)
    u = lambda g: pl.BlockSpec((None, L, LANES), lambda c, b: (b, 0, g * ncb + c), pipeline_mode=one)
    cw = lambda g: pl.BlockSpec((3, LANES), lambda c, b: (0, g * ncb + c))
    cb = lambda g: pl.BlockSpec((1, LANES), lambda c, b: (0, g * ncb + c))
    return pl.pallas_call(
        functools.partial(_hyena_long_body, dims=dims),
        grid=(ncb, B),
        in_specs=[u(0), u(1), u(2), cw(0), cw(1), cw(2), cb(0), cb(1), cb(2),
                  pl.BlockSpec((HYENA_ORDER, LANES), lambda c, b: (0, c)),
                  hspec(0), hspec(1),
                  const1(tabs['g'][0], 2), const1(tabs['ig'][0], 2),
                  const1(tabs['f2'][0], 2), const1(tabs['if2'][0], 2)],
        out_specs=pl.BlockSpec((None, L, LANES), lambda c, b: (b, 0, c)),
        out_shape=jax.ShapeDtypeStruct((B, L, C), F32),
        scratch_shapes=[pltpu.VMEM((NH * P, LANES), F32), pltpu.VMEM((NH * P, LANES), F32),
                        pltpu.VMEM((NH * P, LANES), F32), pltpu.VMEM((N2 * Q, LANES), F32)],
        name="hyena_long",
        compiler_params=_params("arbitrary", "arbitrary"),
    )(hy, hy, hy, lp['conv_w'], lp['conv_w'], lp['conv_w'], lp['conv_b'], lp['conv_b'], lp['conv_b'],
      lp['f_bias'], H, H, tabs['g'][0], tabs['ig'][0], tabs['f2'][0], tabs['if2'][0])


def _hyena(hy, lp):
    L = hy.shape[1]
    k_un, scale = _hyena_filters(L, lp)
    if 2 * L >= 2 * FFT_N1 * SUBLANES and (2 * L) % (FFT_N1 * SUBLANES) == 0:
        return _hyena_long(hy, lp, k_un, scale)
    return _hyena_short(hy, lp, k_un, scale)


def kernel(x_prompt, x_sample, cache_k, cache_v, c, c_ctx, w_mod, b_mod, g_mix, w_in, conv_w, conv_b, f_w1, f_b1,
           f_freq, f_w2, f_b2, f_w3, f_bias, rpb, g_out_hy, g_out_at, w_out, g_ffn, w_router, b_router, w_exp1,
           b_exp1, w_exp2, b_exp2, g_final):
    depth = w_mod.shape[0]
    Bp, Lp, D = x_prompt.shape
    Bs, Ls, _ = x_sample.shape
    n_ctx, n_lat = Bp * Lp, Bs * Ls
    n_tok = n_ctx + n_lat
    d_attn = N_HEADS * HEAD_DIM
    q_off = 3 * D_HYENA
    row2 = lambda a: a.reshape(1, -1)

    cond = jnp.zeros((SUBLANES, D), F32).at[0].set(c_ctx).at[1:1 + Bs].set(c)
    xp, xs = x_prompt, x_sample
    new_k, new_v = [], []
    for l in range(depth):
        wr = jnp.zeros((D, LANES), F32).at[:, :N_EXPERTS].set(w_router[l])
        wr_hi = wr.astype(BF16)
        lp = {
            'conv_w': conv_w[l], 'conv_b': row2(conv_b[l]), 'f_w1': f_w1[l], 'f_b1': row2(f_b1[l]),
            'f_freq': row2(f_freq[l]), 'f_w2': f_w2[l], 'f_b2': row2(f_b2[l]), 'f_w3': f_w3[l],
            'f_bias': f_bias[l], 'g_out_hy': row2(g_out_hy[l]), 'g_out_at': row2(g_out_at[l]),
            'w_out': w_out[l].astype(BF16), 'g_ffn': row2(g_ffn[l]),
            'wr_hi': wr_hi, 'wr_lo': (wr - wr_hi.astype(F32)).astype(BF16),
            'b_router': jnp.zeros((1, LANES), F32).at[0, :N_EXPERTS].set(b_router[l]),
            'w_exp1': w_exp1[l], 'b_exp1': b_exp1[l][:, None, :], 'w_exp2': w_exp2[l],
            'b_exp2': b_exp2[l][:, None, :],
        }
        mod = _modulation(cond, w_mod[l], row2(b_mod[l]))
        mod3 = mod.reshape(SUBLANES, 1, 6 * D)
        w_in_b = w_in[l].astype(BF16)
        g_mix_l = row2(g_mix[l])

        proj_p = _in_projection(xp, mod3, 0, g_mix_l, w_in_b)
        proj_s = _in_projection(xs, mod3, 1, g_mix_l, w_in_b)
        kv = proj_p[..., q_off + d_attn:].reshape(Bp, Lp, 2, N_HEADS, HEAD_DIM).transpose(2, 0, 3, 1, 4)
        new_k.append(kv[0])
        new_v.append(kv[1])

        hy_p = _hyena(proj_p, lp)
        hy_s = _hyena(proj_s, lp)
        at_p = _context_attention(proj_p, q_off)
        heads_last = lambda t: t.transpose(0, 2, 1, 3).reshape(Bs, t.shape[2], d_attn)
        at_s = _neighbourhood_attention(proj_s, q_off, heads_last(cache_k[:, l]), heads_last(cache_v[:, l]), rpb[l])

        x1, h2u, logits = _out_projection((hy_p, at_p, xp), (hy_s, at_s, xs), mod3, lp)

        gates, ce, nv, n_used, gidx, sidx = _routing(logits[:, :N_EXPERTS], n_tok)
        yp = _experts(ce, nv, n_used, gidx, sidx, h2u, lp, n_tok * TOP_K + MOE_CHUNK)
        last = l == depth - 1
        gf = row2(g_final) if last else None
        assert last, "deeper stacks need the un-normalised residual between layers"
        xp = _combine(x1, yp, gates, mod3, 0, gf, 0, Bp, Lp)
        xs = _combine(x1, yp, gates, mod3, 1, gf, n_ctx, Bs, Ls)

    return xp, xs, jnp.stack(new_k, axis=1), jnp.stack(new_v, axis=1)
```

```python
import functools
import math

import numpy as np
import jax
import jax.numpy as jnp
from jax import lax
from jax.experimental import pallas as pl
from jax.experimental.pallas import tpu as pltpu

F32 = jnp.float32
BF16 = jnp.bfloat16
HIGHEST = lax.Precision.HIGHEST

GRID_W = 64
N_HEADS = 16
HEAD_DIM = 64
D_HYENA = 1024
HYENA_ORDER = 2
FILTER_BANDS = 16
DECAY_TARGET = 1e-2
DECAY_PCT_SHORT = 0.3
DECAY_PCT_LONG = 1.5
WIN_H = 8
WIN_W = 16
N_EXPERTS = 32
TOP_K = 4
SWIGLU_ALPHA = 1.702
SWIGLU_LIMIT = 7.0
EPS = 1e-6
NEG_INF = -1e30

LANES = 128
SUBLANES = 8
VMEM_LIMIT = 60 * 1024 * 1024

ROW_TILE = 256
NA_Q_ROWS = 4
MOE_CHUNK = 1024
MOE_SUB = 256
MOE_FF_TILE = 512
MOE_ISSUE = 64


def _params(*sem):
    return pltpu.CompilerParams(dimension_semantics=sem, vmem_limit_bytes=VMEM_LIMIT)


def _rms(x, g):
    return x * lax.rsqrt(jnp.mean(x * x, axis=-1, keepdims=True) + EPS) * g


def _split_bf16(x):
    hi = x.astype(BF16)
    lo = (x - hi.astype(F32)).astype(BF16)
    return hi, lo


def _pack_bf16_pair(lo, hi):
    bl = lax.bitcast_convert_type(lo.astype(BF16).astype(F32), jnp.uint32)
    bh = lax.bitcast_convert_type(hi.astype(BF16).astype(F32), jnp.uint32)
    return (bl >> 16) | (bh & jnp.uint32(0xFFFF0000))


def _unpack_bf16_pair(words):
    lo = lax.bitcast_convert_type(words << 16, F32)
    hi = lax.bitcast_convert_type(words & jnp.uint32(0xFFFF0000), F32)
    return lo, hi


def _dot(a, b):
    return jnp.dot(a, b, preferred_element_type=F32)


def _dot_nt(a, b):
    return lax.dot_general(a, b, (((1,), (1,)), ((), ())), preferred_element_type=F32)


def _mod_body(c_ref, w_ref, b_ref, o_ref):
    c = c_ref[...]
    s = c / (1.0 + jnp.exp(-c))
    o_ref[...] = jnp.dot(s, w_ref[...], precision=HIGHEST, preferred_element_type=F32) + b_ref[...]


def _modulation(cc, w, b):
    D, N = w.shape
    tn = min(N, 1536)
    return pl.pallas_call(
        _mod_body,
        grid=(N // tn,),
        in_specs=[pl.BlockSpec((SUBLANES, D), lambda j: (0, 0)),
                  pl.BlockSpec((D, tn), lambda j: (0, j)),
                  pl.BlockSpec((1, tn), lambda j: (0, j))],
        out_specs=pl.BlockSpec((SUBLANES, tn), lambda j: (0, j)),
        out_shape=jax.ShapeDtypeStruct((SUBLANES, N), F32),
        name="modulation",
        compiler_params=_params("arbitrary"),
    )(cc, w, b)


def _inproj_body(x_ref, shift_ref, scale_ref, g_ref, w_ref, o_ref, *, n_chunk):
    h = _rms(x_ref[...], g_ref[...]) * (1.0 + scale_ref[...]) + shift_ref[...]
    hb = h.astype(BF16)
    n_out = o_ref.shape[-1]

    def col(j, carry):
        c0 = pl.multiple_of(j * n_chunk, n_chunk)
        o_ref[:, pl.ds(c0, n_chunk)] = _dot(hb, w_ref[:, pl.ds(c0, n_chunk)])
        return carry

    lax.fori_loop(0, n_out // n_chunk, col, 0)


def _in_projection(x, mod3, row0, g, w_bf16):
    B, L, D = x.shape
    N = w_bf16.shape[1]
    tm = min(ROW_TILE, L)
    mrow = (lambda b: 0) if row0 == 0 else (lambda b: b + row0)
    return pl.pallas_call(
        functools.partial(_inproj_body, n_chunk=512),
        grid=(B, L // tm),
        in_specs=[pl.BlockSpec((None, tm, D), lambda b, i: (b, i, 0)),
                  pl.BlockSpec((None, 1, D), lambda b, i: (mrow(b), 0, 0)),
                  pl.BlockSpec((None, 1, D), lambda b, i: (mrow(b), 0, 1)),
                  pl.BlockSpec((1, D), lambda b, i: (0, 0)),
                  pl.BlockSpec((D, N), lambda b, i: (0, 0), pipeline_mode=pl.Buffered(1))],
        out_specs=pl.BlockSpec((None, tm, N), lambda b, i: (b, i, 0)),
        out_shape=jax.ShapeDtypeStruct((B, L, N), F32),
        name="in_projection",
        compiler_params=_params("arbitrary", "arbitrary"),
    )(x, mod3, mod3, g, w_bf16)


def _head_lane_masks(rows, hd):
    lane = lax.broadcasted_iota(jnp.int32, (rows, LANES), 1)
    return [(lane >= h * hd) & (lane < (h + 1) * hd) for h in range(LANES // hd)]


def _ctx_attn_body(q_ref, k_ref, v_ref, o_ref, *, scale, hd):
    q = q_ref[...]
    k = k_ref[...].astype(BF16)
    v = v_ref[...].astype(BF16)
    masks = _head_lane_masks(q.shape[0], hd)
    out = None
    for msk in masks:
        qh = jnp.where(msk, q, 0.0).astype(BF16)
        s = _dot_nt(qh, k) * scale
        p = jnp.exp(s - jnp.max(s, axis=-1, keepdims=True))
        o = _dot(p.astype(BF16), v) / jnp.sum(p, axis=-1, keepdims=True)
        out = o if out is None else jnp.where(msk, o, out)
    o_ref[...] = out


def _context_attention(proj, q_off):
    B, L, _ = proj.shape
    d_attn = N_HEADS * HEAD_DIM
    nhp = d_attn // LANES
    qb, kb, vb = (q_off // LANES, (q_off + d_attn) // LANES, (q_off + 2 * d_attn) // LANES)
    spec = lambda base: pl.BlockSpec((None, L, LANES), lambda b, h: (b, 0, base + h))
    return pl.pallas_call(
        functools.partial(_ctx_attn_body, scale=1.0 / math.sqrt(HEAD_DIM), hd=HEAD_DIM),
        grid=(B, nhp),
        in_specs=[spec(qb), spec(kb), spec(vb)],
        out_specs=pl.BlockSpec((None, L, LANES), lambda b, h: (b, 0, h)),
        out_shape=jax.ShapeDtypeStruct((B, L, d_attn), F32),
        name="context_attention",
        compiler_params=_params("arbitrary", "arbitrary"),
    )(proj, proj, proj)


def _na_tables(rows, rpb):
    W = GRID_W
    kh = min(WIN_H, rows)
    rbq = NA_Q_ROWS
    kwr = rbq + kh
    assert rows % rbq == 0 and kwr <= rows
    nblk = rows // rbq
    ws = np.clip(np.arange(nblk) * rbq - kh // 2, 0, rows - kwr)
    rq = np.arange(nblk)[:, None, None] * rbq + np.arange(rbq)[None, :, None]
    rk = ws[:, None, None] + np.arange(kwr)[None, None, :]
    rs = np.clip(rq - kh // 2, 0, rows - kh)
    vr = (rk >= rs) & (rk < rs + kh)
    dr = np.where(vr, rk - rq + WIN_H - 1, 0)
    patterns = np.concatenate([dr.reshape(nblk, -1), vr.reshape(nblk, -1)], axis=1)
    _, first, btype = np.unique(patterns, axis=0, return_index=True, return_inverse=True)
    dr_t, vr_t = dr[first], vr[first]
    cq = np.arange(W)[:, None]
    ck = np.arange(W)[None, :]
    cs = np.clip(cq - WIN_W // 2, 0, W - WIN_W)
    vc = (ck >= cs) & (ck < cs + WIN_W)
    dc = np.clip(ck - cq, -(WIN_W - 1), WIN_W - 1) + WIN_W - 1
    nt = dr_t.shape[0]
    rowsel = (dr_t[..., None] == np.arange(2 * WIN_H - 1)) & vr_t[..., None]
    colsel = (dc[..., None] == np.arange(2 * WIN_W - 1)) & vc[..., None]
    vals = jnp.einsum('tajr,hrc,qkc->htaqjk', rowsel.astype(np.float32), rpb.astype(F32),
                      colsel.astype(np.float32), precision=HIGHEST)
    valid = vr_t[:, :, None, :, None] & vc[None, None, :, None, :]
    bias = jnp.where(valid[None], vals, NEG_INF)
    bias = bias.reshape(rpb.shape[0], nt, rbq * W, kwr * W)
    return ws.astype(np.int32), btype.reshape(-1).astype(np.int32), bias


def _na_body(ws_ref, bt_ref, q_ref, k_ref, v_ref, kc_ref, vc_ref, bias_ref, o_ref, *, scale, hd, nk):
    i = pl.program_id(2)
    start = pl.multiple_of(ws_ref[i] * GRID_W, GRID_W)
    q = q_ref[...] * scale
    kl = k_ref[pl.ds(start, nk), :].astype(BF16)
    vl = v_ref[pl.ds(start, nk), :]
    kc = kc_ref[...].astype(BF16)
    vc = vc_ref[...]
    masks = _head_lane_masks(q.shape[0], hd)
    vl_masks = _head_lane_masks(nk, hd)
    vc_masks = _head_lane_masks(vc.shape[0], hd)
    assert len(masks) == 2, "the value lanes of the other head carry the softmax denominator"
    out = None
    for h, msk in enumerate(masks):
        qh = jnp.where(msk, q, 0.0).astype(BF16)
        sl = _dot_nt(qh, kl) + bias_ref[h]
        sc = _dot_nt(qh, kc)
        m = jnp.maximum(jnp.max(sl, axis=-1, keepdims=True), jnp.max(sc, axis=-1, keepdims=True))
        p_l = jnp.exp(sl - m).astype(BF16)
        p_c = jnp.exp(sc - m).astype(BF16)
        o = (_dot(p_l, jnp.where(vl_masks[h], vl, 1.0).astype(BF16))
             + _dot(p_c, jnp.where(vc_masks[h], vc, 1.0).astype(BF16)))
        o = o / pltpu.roll(o, hd, axis=1)
        out = o if out is None else jnp.where(msk, o, out)
    o_ref[...] = out


def _neighbourhood_attention(proj, q_off, kc, vc, rpb):
    B, L, _ = proj.shape
    Lc = kc.shape[1]
    d_attn = N_HEADS * HEAD_DIM
    hpb = LANES // HEAD_DIM
    nhp = d_attn // LANES
    rows = L // GRID_W
    ws, btype, bias = _na_tables(rows, rpb)
    nq = NA_Q_ROWS * GRID_W
    nk = (NA_Q_ROWS + min(WIN_H, rows)) * GRID_W
    nblk = rows // NA_Q_ROWS
    qb, kb, vb = (q_off // LANES, (q_off + d_attn) // LANES, (q_off + 2 * d_attn) // LANES)
    full = lambda base: pl.BlockSpec((None, L, LANES), lambda b, h, i, ws_r, bt_r: (b, 0, base + h))
    ctx = pl.BlockSpec((None, Lc, LANES), lambda b, h, i, ws_r, bt_r: (b, 0, h))
    grid_spec = pltpu.PrefetchScalarGridSpec(
        num_scalar_prefetch=2,
        grid=(B, nhp, nblk),
        in_specs=[pl.BlockSpec((None, nq, LANES), lambda b, h, i, ws_r, bt_r: (b, i, qb + h)),
                  full(kb), full(vb), ctx, ctx,
                  pl.BlockSpec((hpb, None, nq, nk), lambda b, h, i, ws_r, bt_r: (h, bt_r[i], 0, 0))],
        out_specs=pl.BlockSpec((None, nq, LANES), lambda b, h, i, ws_r, bt_r: (b, i, h)),
    )
    return pl.pallas_call(
        functools.partial(_na_body, scale=1.0 / math.sqrt(HEAD_DIM), hd=HEAD_DIM, nk=nk),
        grid_spec=grid_spec,
        out_shape=jax.ShapeDtypeStruct((B, L, d_attn), F32),
        name="neighbourhood_attention",
        compiler_params=_params("arbitrary", "arbitrary", "arbitrary"),
    )(jnp.asarray(ws), jnp.asarray(btype), proj, proj, proj, kc, vc, bias)


def _outproj_body(yhp_ref, yap_ref, xp_ref, yhs_ref, yas_ref, xs_ref, gate1_ref, shift2_ref, scale2_ref, ghy_ref,
                  gat_ref, w_ref, gffn_ref, wrh_ref, wrl_ref, br_ref, x1_ref, h2u_ref, lg_ref, *, n_ctx_tiles):
    def run(yh_ref, ya_ref, x_ref):
        cat = jnp.concatenate([_rms(yh_ref[...], ghy_ref[...]), _rms(ya_ref[...], gat_ref[...])], axis=-1)
        mix = _dot(cat.astype(BF16), w_ref[...])
        x1 = x_ref[...] + gate1_ref[...] * mix
        x1_ref[...] = x1
        h2 = _rms(x1, gffn_ref[...]) * (1.0 + scale2_ref[...]) + shift2_ref[...]
        hi, lo = _split_bf16(h2)
        lg_ref[...] = (_dot(hi, wrh_ref[...]) + _dot(lo, wrh_ref[...]) + _dot(hi, wrl_ref[...])) + br_ref[...]
        half = h2.shape[-1] // 2
        words = _pack_bf16_pair(h2[:, :half], h2[:, half:])
        per = half // LANES
        for s in range(per):
            h2u_ref[pl.ds(s, words.shape[0], stride=per), :] = words[:, s * LANES:(s + 1) * LANES]

    is_ctx = pl.program_id(0) < n_ctx_tiles
    pl.when(is_ctx)(lambda: run(yhp_ref, yap_ref, xp_ref))
    pl.when(jnp.logical_not(is_ctx))(lambda: run(yhs_ref, yas_ref, xs_ref))


def _out_projection(ctx, lat, mod3, lp):
    Bp, Lp, D = ctx[2].shape
    Bs, Ls, _ = lat[2].shape
    dh, da = ctx[0].shape[-1], ctx[1].shape[-1]
    tm = min(ROW_TILE, Lp, Ls)
    nct, nlt = Bp * Lp // tm, Bs * Ls // tm
    n_tok = (nct + nlt) * tm
    lat_tiles = Ls // tm
    mrow = lambda i: jnp.where(i < nct, 0, 1 + (i - nct) // lat_tiles)
    modspec = lambda c: pl.BlockSpec((None, 1, D), lambda i: (mrow(i), 0, c))
    const = lambda shape: pl.BlockSpec(shape, lambda i: (0,) * len(shape))
    crow = lambda w: pl.BlockSpec((tm, w), lambda i: (jnp.minimum(i, nct - 1), 0))
    lrow = lambda w: pl.BlockSpec((tm, w), lambda i: (jnp.maximum(i - nct, 0), 0))
    flat = lambda t: t.reshape(-1, t.shape[-1])
    return pl.pallas_call(
        functools.partial(_outproj_body, n_ctx_tiles=nct),
        grid=(nct + nlt,),
        in_specs=[crow(dh), crow(da), crow(D), lrow(dh), lrow(da), lrow(D),
                  modspec(2), modspec(3), modspec(4),
                  const((1, dh)), const((1, da)),
                  pl.BlockSpec((dh + da, D), lambda i: (0, 0), pipeline_mode=pl.Buffered(1)),
                  const((1, D)), const((D, LANES)), const((D, LANES)), const((1, LANES))],
        out_specs=[pl.BlockSpec((tm, D), lambda i: (i, 0)),
                   pl.BlockSpec((tm * (D // 2 // LANES), LANES), lambda i: (i, 0)),
                   pl.BlockSpec((tm, LANES), lambda i: (i, 0))],
        out_shape=[jax.ShapeDtypeStruct((n_tok, D), F32),
                   jax.ShapeDtypeStruct((n_tok * (D // 2 // LANES), LANES), jnp.uint32),
                   jax.ShapeDtypeStruct((n_tok, LANES), F32)],
        name="out_projection",
        compiler_params=_params("arbitrary"),
    )(*[flat(t) for t in ctx], *[flat(t) for t in lat], mod3, mod3, mod3, lp['g_out_hy'], lp['g_out_at'],
      lp['w_out'], lp['g_ffn'], lp['wr_hi'], lp['wr_lo'], lp['b_router'])


def _expert_body(ce_ref, nv_ref, nu_ref, gcur_ref, gnext_ref, sprev_ref, scur_ref, h2u_hbm, w1g_ref, w1u_ref,
                 b1g_ref, b1u_ref, w2_ref, b2_ref, yp_hbm, xu, xb, acc, ystage, abuf, gsem, ssem):
    c = pl.program_id(0)
    f = pl.program_id(1)
    nc = pl.num_programs(0)
    nf = pl.num_programs(1)
    half = xb.shape[1] // 2
    xt = half // LANES
    yt = xt
    slot = c % 2
    subs = lambda n: (n + MOE_SUB - 1) // MOE_SUB
    nv = nv_ref[c]
    nsub = subs(nv)
    nrows = nsub * MOE_SUB
    last = c == nu_ref[0] - 1
    nv_prev = nv_ref[jnp.maximum(c - 1, 0)]
    nrows_prev = subs(nv_prev) * MOE_SUB
    covered_prev = MOE_ISSUE * nf * subs(nv_prev)
    nrows_next = jnp.where(c + 1 < nc, subs(nv_ref[jnp.minimum(c + 1, nc - 1)]), 0) * MOE_SUB
    covered = MOE_ISSUE * nf * nsub

    def gather_row(idx_ref, r, to_slot):
        tok = idx_ref[0, r]
        pltpu.make_async_copy(h2u_hbm.at[pl.ds(pl.multiple_of(tok * xt, xt), xt), :],
                              xu.at[to_slot, pl.ds(pl.multiple_of(r * xt, xt), xt), :], gsem.at[to_slot]).start()

    def scatter_row(dst, r):
        pltpu.make_async_copy(ystage.at[pl.ds(pl.multiple_of(r * yt, yt), yt), :],
                              yp_hbm.at[pl.ds(pl.multiple_of(dst * yt, yt), yt), :], ssem).start()

    def for_rows(lo, hi, fn):
        lax.fori_loop(lo, hi, lambda r, carry: (fn(r), carry)[1], 0)

    def drain(n, src, dst, sem, per):
        piece = lambda m: pltpu.make_async_copy(src.at[pl.ds(0, m * per), :], dst.at[pl.ds(0, m * per), :], sem)
        for_rows(0, n // MOE_SUB, lambda i: piece(MOE_SUB).wait())
        bit = MOE_SUB // 2
        while bit:
            if not isinstance(n, int):
                pl.when((n & bit) != 0)(piece(bit).wait)
            elif n & bit:
                piece(bit).wait()
            bit //= 2

    @pl.when(c < nu_ref[0])
    def _chunk():
        @pl.when(f == 0)
        def _arrive():
            @pl.when(c == 0)
            def _first():
                for_rows(0, nrows, lambda r: gather_row(gcur_ref, r, 0))

            started = jnp.where(c == 0, nrows, jnp.maximum(covered_prev, nrows))
            drain(started, h2u_hbm, xu.at[slot], gsem.at[slot], xt)

            def unpack(sb, carry):
                r0 = pl.multiple_of(sb * MOE_SUB, MOE_SUB)
                for s in range(xt):
                    lo, hi = _unpack_bf16_pair(xu[slot, pl.ds(r0 * xt + s, MOE_SUB, stride=xt), :])
                    xb[pl.ds(r0, MOE_SUB), s * LANES:(s + 1) * LANES] = lo.astype(BF16)
                    xb[pl.ds(r0, MOE_SUB), half + s * LANES:half + (s + 1) * LANES] = hi.astype(BF16)
                acc[pl.ds(r0, MOE_SUB), :] = jnp.zeros((MOE_SUB, acc.shape[1]), F32)
                return carry

            lax.fori_loop(0, nsub, unpack, 0)

        def matmul_steps(with_scatter):
            def up_proj(sb):
                base = (f * nsub + sb) * MOE_ISSUE
                for j in range(MOE_ISSUE):
                    gather_row(gnext_ref, base + j, 1 - slot)
                if with_scatter:
                    for j in range(MOE_ISSUE):
                        scatter_row(sprev_ref[0, base + j], jnp.minimum(base + j, nrows_prev - 1))
                x = xb[pl.ds(pl.multiple_of(sb * MOE_SUB, MOE_SUB), MOE_SUB), :]
                gate = jnp.minimum(_dot(x, w1g_ref[...].astype(BF16)) + b1g_ref[...], SWIGLU_LIMIT)
                up = jnp.clip(_dot(x, w1u_ref[...].astype(BF16)) + b1u_ref[...], -SWIGLU_LIMIT, SWIGLU_LIMIT)
                glu = gate / (1.0 + jnp.exp(-SWIGLU_ALPHA * gate))
                abuf[...] = ((up + 1.0) * glu).astype(BF16)

            def down_proj(sb):
                acc[pl.ds(pl.multiple_of(sb * MOE_SUB, MOE_SUB), MOE_SUB), :] += _dot(abuf[...], w2_ref[...].astype(BF16))

            def step(sb, carry):
                down_proj(sb - 1)
                up_proj(sb)
                return carry

            up_proj(0)
            lax.fori_loop(1, nsub, step, 0)
            down_proj(nsub - 1)

        pl.when(c == 0)(lambda: matmul_steps(False))
        pl.when(c > 0)(lambda: matmul_steps(True))

        @pl.when(f == nf - 1)
        def _leave():
            for_rows(covered, nrows_next, lambda r: gather_row(gnext_ref, r, 1 - slot))

            @pl.when(c > 0)
            def _prev_out():
                for_rows(covered, nv_prev, lambda r: scatter_row(sprev_ref[0, r], r))
                drain(jnp.maximum(covered, nv_prev), yp_hbm, ystage, ssem, yt)

            def stage(sb, carry):
                r0 = pl.multiple_of(sb * MOE_SUB, MOE_SUB)
                for s in range(yt):
                    lo, hi = slice(s * LANES, (s + 1) * LANES), slice(half + s * LANES, half + (s + 1) * LANES)
                    ystage[pl.ds(r0 * yt + s, MOE_SUB, stride=yt), :] = _pack_bf16_pair(
                        acc[pl.ds(r0, MOE_SUB), lo] + b2_ref[:, lo], acc[pl.ds(r0, MOE_SUB), hi] + b2_ref[:, hi])
                return carry

            lax.fori_loop(0, nsub, stage, 0)

            @pl.when(last)
            def _flush():
                for_rows(0, nv, lambda r: scatter_row(scur_ref[0, r], r))
                spare0 = yp_hbm.shape[0] // yt - xb.shape[0]
                for_rows(0, xb.shape[0], lambda r: scatter_row(spare0 + r, jnp.minimum(r, nrows - 1)))
                drain(nv, yp_hbm, ystage, ssem, yt)
                drain(xb.shape[0], yp_hbm, ystage, ssem, yt)
                drain(jnp.maximum(covered, nrows_next), h2u_hbm, xu.at[1 - slot], gsem.at[1 - slot], xt)


def _experts(chunk_e, chunk_nv, n_used, gidx, sidx, h2u, lp, n_out_rows):
    E, D, two_ff = lp['w_exp1'].shape
    d_ff = two_ff // 2
    nc = gidx.shape[0]
    R, tf = MOE_CHUNK, MOE_FF_TILE
    nf = d_ff // tf
    assert MOE_ISSUE * nf * (R // MOE_SUB) <= R, "a chunk's matmul steps must not start more rows than a chunk holds"

    def ff(c, f, nu):
        return jnp.where(c < nu[0], f, nf - 1)

    smem = lambda step: pl.BlockSpec((None, 1, R), lambda c, f, ce, nv, nu: (jnp.clip(c + step, 0, nc - 1), 0, 0),
                                     memory_space=pltpu.SMEM)
    grid_spec = pltpu.PrefetchScalarGridSpec(
        num_scalar_prefetch=3,
        grid=(nc, nf),
        in_specs=[smem(0), smem(1), smem(-1), smem(0),
                  pl.BlockSpec(memory_space=pl.ANY),
                  pl.BlockSpec((None, D, tf), lambda c, f, ce, nv, nu: (ce[c], 0, ff(c, f, nu))),
                  pl.BlockSpec((None, D, tf), lambda c, f, ce, nv, nu: (ce[c], 0, nf + ff(c, f, nu))),
                  pl.BlockSpec((None, 1, tf), lambda c, f, ce, nv, nu: (ce[c], 0, ff(c, f, nu))),
                  pl.BlockSpec((None, 1, tf), lambda c, f, ce, nv, nu: (ce[c], 0, nf + ff(c, f, nu))),
                  pl.BlockSpec((None, tf, D), lambda c, f, ce, nv, nu: (ce[c], ff(c, f, nu), 0)),
                  pl.BlockSpec((None, 1, D), lambda c, f, ce, nv, nu: (ce[c], 0, 0))],
        out_specs=pl.BlockSpec(memory_space=pl.ANY),
        scratch_shapes=[pltpu.VMEM((2, R * (D // 2 // LANES), LANES), jnp.uint32),
                        pltpu.VMEM((R, D), BF16),
                        pltpu.VMEM((R, D), F32),
                        pltpu.VMEM((R * (D // 2 // LANES), LANES), jnp.uint32),
                        pltpu.VMEM((MOE_SUB, tf), BF16),
                        pltpu.SemaphoreType.DMA((2,)),
                        pltpu.SemaphoreType.DMA(())],
    )
    return pl.pallas_call(
        _expert_body,
        grid_spec=grid_spec,
        out_shape=jax.ShapeDtypeStruct((n_out_rows * (D // 2 // LANES), LANES), jnp.uint32),
        name="experts",
        compiler_params=pltpu.CompilerParams(dimension_semantics=("arbitrary", "arbitrary"),
                                             vmem_limit_bytes=VMEM_LIMIT, has_side_effects=True,
                                             disable_bounds_checks=True),
    )(chunk_e, chunk_nv, n_used, gidx, gidx, sidx, sidx, h2u, lp['w_exp1'], lp['w_exp1'], lp['b_exp1'],
      lp['b_exp1'], lp['w_exp2'], lp['b_exp2'])


def _routing(logits, n_tok):
    R = MOE_CHUNK
    top_v, top_i = lax.top_k(logits, TOP_K)
    gates = jax.nn.softmax(top_v, axis=-1)
    n_pairs = n_tok * TOP_K
    flat_e = top_i.reshape(n_pairs).astype(jnp.int32)
    experts = jnp.arange(N_EXPERTS, dtype=jnp.int32)
    counts = jnp.sum((flat_e[:, None] == experts[None]).astype(jnp.int32), axis=0)
    padded = (counts + R - 1) // R * R
    pad_end = jnp.cumsum(padded)
    pad_start = pad_end - padded
    nc = n_pairs // R + N_EXPERTS
    pair = jnp.arange(n_pairs, dtype=jnp.int32)
    fill_e = jnp.repeat(experts, R)
    fill_i = jnp.tile(jnp.arange(R, dtype=jnp.int32), N_EXPERTS)
    fill_key = jnp.where(fill_i < jnp.repeat(padded - counts, R), 2 * fill_e + 1, 2 * N_EXPERTS)
    _, pid = lax.sort((jnp.concatenate([2 * flat_e, fill_key]), jnp.concatenate([pair, n_pairs + fill_i])),
                      num_keys=1)
    real = pid < n_pairs
    gidx = jnp.where(real, pid // TOP_K, 0)
    sidx = jnp.where(real, (pid % TOP_K) * n_tok + pid // TOP_K, pid)
    n_used = (pad_end[-1] // R).astype(jnp.int32)
    cstart = jnp.arange(nc, dtype=jnp.int32) * R
    ce = jnp.minimum(jnp.searchsorted(pad_end, cstart, side='right'), N_EXPERTS - 1).astype(jnp.int32)
    ce = jnp.where(jnp.arange(nc) < n_used, ce, ce[jnp.maximum(n_used - 1, 0)])
    nv = jnp.clip(counts[ce] - (cstart - pad_start[ce]), 0, R).astype(jnp.int32)
    nv = jnp.where(jnp.arange(nc) < n_used, nv, 0)
    return gates, ce, nv, n_used.reshape(1), gidx.reshape(nc, 1, R), sidx.reshape(nc, 1, R)


def _combine_body(x1_ref, y0_ref, y1_ref, y2_ref, y3_ref, g_ref, gate2_ref, gf_ref, o_ref):
    g = g_ref[...]
    tm, D = x1_ref.shape
    per = D // 2 // LANES

    def rows(y_ref):
        parts = [_unpack_bf16_pair(y_ref[pl.ds(s, tm, stride=per), :]) for s in range(per)]
        return jnp.concatenate([p[0] for p in parts] + [p[1] for p in parts], axis=-1)

    ff = (g[:, 0:1] * rows(y0_ref) + g[:, 1:2] * rows(y1_ref)) + (g[:, 2:3] * rows(y2_ref) + g[:, 3:4] * rows(y3_ref))
    x2 = x1_ref[...] + gate2_ref[...] * ff
    o_ref[...] = _rms(x2, gf_ref[...])


def _combine(x1, yp, gates, mod3, row0, g_final, tok0, B, L):
    n_tok, D = x1.shape
    tm = min(ROW_TILE, L)
    nl = L // tm
    blk0 = tok0 // tm
    nblk_tok = n_tok // tm
    mrow = (lambda b: 0) if row0 == 0 else (lambda b: b + row0)
    row = lambda b, i: (blk0 + b * nl + i, 0)
    yspec = lambda k: pl.BlockSpec((tm * (D // 2 // LANES), LANES),
                                   lambda b, i: (k * nblk_tok + blk0 + b * nl + i, 0))
    return pl.pallas_call(
        _combine_body,
        grid=(B, nl),
        in_specs=[pl.BlockSpec((tm, D), row), yspec(0), yspec(1), yspec(2), yspec(3),
                  pl.BlockSpec((tm, TOP_K), row),
                  pl.BlockSpec((None, 1, D), lambda b, i: (mrow(b), 0, 5)),
                  pl.BlockSpec((1, D), lambda b, i: (0, 0))],
        out_specs=pl.BlockSpec((None, tm, D), lambda b, i: (b, i, 0)),
        out_shape=jax.ShapeDtypeStruct((B, L, D), F32),
        name="combine",
        compiler_params=_params("arbitrary", "arbitrary"),
    )(x1, yp, yp, yp, yp, gates, mod3, g_final)


def _filter_body(tw_ref, bands_ref, w1_ref, b1_ref, fr_ref, w2_ref, b2_ref, w3_ref, dl_ref, k_ref, ss_ref):
    hi = functools.partial(jnp.dot, precision=HIGHEST, preferred_element_type=F32)
    t = tw_ref[:, 0:1]
    w = tw_ref[:, 1:2]
    lane = lax.broadcasted_iota(jnp.int32, (t.shape[0], LANES), 1)
    fw = w * bands_ref[...]
    z = jnp.where(lane == 0, t,
                  jnp.where(lane <= FILTER_BANDS, jnp.cos(fw),
                            jnp.where(lane <= 2 * FILTER_BANDS, -jnp.sin(fw), 0.0)))
    fr = fr_ref[...]
    h = jnp.sin(fr * (hi(z, w1_ref[...]) + b1_ref[...]))
    h = jnp.sin(fr * (hi(h, w2_ref[...]) + b2_ref[...]))
    decay = jnp.exp(-t * dl_ref[...])
    C = decay.shape[1]
    ss = []
    for g in range(k_ref.shape[1] // C):
        kg = hi(h, w3_ref[:, g * C:(g + 1) * C]) * decay
        k_ref[:, g * C:(g + 1) * C] = kg
        ss.append(jnp.sum(kg * kg, axis=0, keepdims=True))
    ss = jnp.concatenate(ss, axis=-1)

    @pl.when(pl.program_id(0) == 0)
    def _first():
        ss_ref[...] = ss

    @pl.when(pl.program_id(0) > 0)
    def _rest():
        ss_ref[...] += ss


def _hyena_filters(L, lp):
    C = D_HYENA
    pos = jnp.arange(L, dtype=F32)
    tw = jnp.stack([pos / max(L - 1, 1), 2 * math.pi * pos / L], axis=-1)
    bands = jnp.linspace(1e-4, FILTER_BANDS - 1, FILTER_BANDS, dtype=F32)
    bands128 = jnp.zeros((1, LANES), F32).at[0, 1:1 + 2 * FILTER_BANDS].set(jnp.concatenate([bands, bands]))
    w1 = jnp.zeros((LANES, lp['f_w1'].shape[1]), F32).at[:lp['f_w1'].shape[0]].set(lp['f_w1'])
    deltas = jnp.abs(jnp.linspace(math.log(DECAY_TARGET) / DECAY_PCT_LONG,
                                  math.log(DECAY_TARGET) / DECAY_PCT_SHORT, C, dtype=F32))[None]
    nk = lp['f_w3'].shape[1]
    tl = min(L, 256)
    args = [tw, bands128, w1, lp['f_b1'], lp['f_freq'], lp['f_w2'], lp['f_b2'], lp['f_w3'], deltas]
    const = lambda a: pl.BlockSpec(a.shape, lambda i: (0,) * a.ndim)
    k_un, ss = pl.pallas_call(
        _filter_body,
        grid=(L // tl,),
        in_specs=[pl.BlockSpec((tl, 2), lambda i: (i, 0))] + [const(a) for a in args[1:]],
        out_specs=[pl.BlockSpec((tl, nk), lambda i: (i, 0)), pl.BlockSpec((1, nk), lambda i: (0, 0))],
        out_shape=[jax.ShapeDtypeStruct((L, nk), F32), jax.ShapeDtypeStruct((1, nk), F32)],
        name="hyena_filters",
        compiler_params=_params("arbitrary"),
    )(*args)
    ss = ss.reshape(HYENA_ORDER, 2, C)
    scale = lax.rsqrt(ss[:, 0] + ss[:, 1] + EPS).reshape(1, HYENA_ORDER * C)
    return k_un, scale


def _short_conv_chunk(u_ref, r0, n, prev_last, next_first, w, b):
    u = u_ref[pl.ds(r0, n), :]
    row = lax.broadcasted_iota(jnp.int32, u.shape, 0)
    up = jnp.where(row == 0, prev_last, pltpu.roll(u, 1, axis=0))
    un = jnp.where(row == n - 1, next_first, pltpu.roll(u, n - 1, axis=0))
    return up * w[0:1] + u * w[1:2] + un * w[2:3] + b


def _dft_spectrum_body(a_ref, b_ref, s_ref, o_ref):
    o_ref[...] = jnp.dot(a_ref[...], b_ref[...], precision=HIGHEST, preferred_element_type=F32) * s_ref[...]


def _dft_spectrum(a, b, scale):
    M, K = a.shape
    ncol = b.shape[1]
    tn = min(ncol, 512)
    return pl.pallas_call(
        _dft_spectrum_body,
        grid=(ncol // tn,),
        in_specs=[pl.BlockSpec((M, K), lambda j: (0, 0)), pl.BlockSpec((K, tn), lambda j: (0, j)),
                  pl.BlockSpec((1, tn), lambda j: (0, j))],
        out_specs=pl.BlockSpec((M, tn), lambda j: (0, j)),
        out_shape=jax.ShapeDtypeStruct((M, ncol), F32),
        name="dft_spectrum",
        compiler_params=_params("arbitrary"),
    )(a, b, scale)


def _hyena_short_body(v_ref, x1_ref, x2_ref, wv_ref, w1_ref, w2_ref, bv_ref, b1_ref, b2_ref, fb_ref, h0_ref, h1_ref,
                      ff_ref, if_ref, o_ref):
    L = v_ref.shape[0]
    N = ff_ref.shape[0] // 2
    conv = lambda u_ref, w_ref, b_ref: _short_conv_chunk(u_ref, 0, L, 0.0, 0.0, w_ref[...], b_ref[...])
    z = conv(v_ref, wv_ref, bv_ref)
    gates = (conv(x1_ref, w1_ref, b1_ref), conv(x2_ref, w2_ref, b2_ref))
    for o, (gate, h_ref) in enumerate(zip(gates, (h0_ref, h1_ref))):
        X = _dot(ff_ref[...], z.astype(BF16))
        xr, xi = X[:N], X[N:]
        hr, hi = h_ref[:N, :], h_ref[N:, :]
        Y = jnp.concatenate([xr * hr - xi * hi, xr * hi + xi * hr], axis=0)
        y = _dot(if_ref[...], Y.astype(BF16))
        z = gate * (y + fb_ref[o:o + 1, :] * z)
    o_ref[...] = z


def _hyena_short(hy, lp, k_un, scale):
    B, L, _ = hy.shape
    C = D_HYENA
    N = 2 * L
    ct = 256
    ncb = C // ct
    k4 = k_un.reshape(L, HYENA_ORDER, 2, C)
    taps = jnp.concatenate([k4[:, :, 0], jnp.zeros((1, HYENA_ORDER, C), F32), k4[:0:-1, :, 1]], axis=0)
    KF = (N // 2 + 1 + SUBLANES - 1) // SUBLANES * SUBLANES
    k = np.arange(KF)
    kept = (k <= N // 2)[:, None]
    weight = np.where((k == 0) | (k == N // 2), 1.0, 2.0)[:, None] * kept / N
    ang = 2 * np.pi * np.outer(k, np.arange(N)) / N
    dft = jnp.asarray(np.concatenate([np.cos(ang) * kept, -np.sin(ang) * kept], axis=0), F32)
    H = _dft_spectrum(dft, taps.reshape(N, HYENA_ORDER * C), scale)
    fwd = jnp.asarray(np.concatenate([np.cos(ang[:, :L]) * kept, -np.sin(ang[:, :L]) * kept], axis=0), BF16)
    inv = jnp.asarray(np.concatenate([np.cos(ang[:, :L]) * weight, -np.sin(ang[:, :L]) * weight], axis=0).T, BF16)
    u = lambda g: pl.BlockSpec((None, L, ct), lambda c, b: (b, 0, g * ncb + c))
    cw = lambda g: pl.BlockSpec((3, ct), lambda c, b: (0, g * ncb + c))
    cb = lambda g: pl.BlockSpec((1, ct), lambda c, b: (0, g * ncb + c))
    const = lambda a: pl.BlockSpec(a.shape, lambda c, b: (0,) * a.ndim)
    return pl.pallas_call(
        _hyena_short_body,
        grid=(ncb, B),
        in_specs=[u(0), u(1), u(2), cw(0), cw(1), cw(2), cb(0), cb(1), cb(2),
                  pl.BlockSpec((HYENA_ORDER, ct), lambda c, b: (0, c)),
                  pl.BlockSpec((2 * KF, ct), lambda c, b: (0, c)),
                  pl.BlockSpec((2 * KF, ct), lambda c, b: (0, ncb + c)),
                  const(fwd), const(inv)],
        out_specs=pl.BlockSpec((None, L, ct), lambda c, b: (b, 0, c)),
        out_shape=jax.ShapeDtypeStruct((B, L, C), F32),
        name="hyena_short",
        compiler_params=_params("arbitrary", "arbitrary"),
    )(hy, hy, hy, lp['conv_w'], lp['conv_w'], lp['conv_w'], lp['conv_b'], lp['conv_b'], lp['conv_b'],
      lp['f_bias'], H, H, fwd, inv)


FFT_N1 = 128
FFT_BATCH = 4
FFT_GROUPS = 2
SPECTRUM_PASSES = 1


def _fft_dims(L):
    N = 2 * L
    N1 = FFT_N1
    N2 = N // N1
    step = FFT_BATCH * FFT_GROUPS
    K1 = (N1 // 2 + 1 + step - 1) // step * step
    assert N1 * N2 == N and N2 % SUBLANES == 0 and K1 % SUBLANES == 0
    return N, N1, N2, N2 + SUBLANES, 2 * K1 + SUBLANES, K1


def _fft_tables(L):
    N, N1, N2, _, _, K1 = _fft_dims(L)
    NH = N1 // 2
    n1 = np.arange(NH)
    k1 = np.arange(K1)
    n2 = np.arange(N2)
    kept = (k1 <= N1 // 2)[None, :, None]
    weight = np.where((k1 == 0) | (k1 == N1 // 2), 1.0, 2.0)[None, :, None] * kept / N
    th = 2 * np.pi * (n1[None, None, :] * k1[None, :, None] / N1 + n2[:, None, None] * k1[None, :, None] / N)
    g = np.concatenate([np.cos(th) * kept, -np.sin(th) * kept], axis=1).reshape(N2 * 2 * K1, NH)
    ig = np.concatenate([np.cos(th) * weight, -np.sin(th) * weight], axis=1).transpose(0, 2, 1)
    ig = ig.reshape(N2 * NH, 2 * K1)
    ph = 2 * np.pi * np.outer(n2, n2) / N2
    c, s = np.cos(ph), np.sin(ph)
    f2 = np.block([[c, s], [-s, c]])
    if2 = np.block([[c, -s], [s, c]])

    def hilo(a):
        hi = jnp.asarray(a, F32).astype(BF16)
        lo = (jnp.asarray(a, F32) - hi.astype(F32)).astype(BF16)
        return hi, lo

    return {'g': hilo(g), 'ig': hilo(ig), 'f2': hilo(f2), 'if2': hilo(if2)}


def _mm(tab, r0, nrows, x, passes):
    a_hi = tab[0][pl.ds(r0, nrows), :]
    if passes == 1:
        return _dot(a_hi, x.astype(BF16))
    x_hi, x_lo = _split_bf16(x)
    a_lo = tab[1][pl.ds(r0, nrows), :]
    return _dot(a_hi, x_hi) + (_dot(a_hi, x_lo) + _dot(a_lo, x_hi))


def _fft_stage1(tbuf, sbuf, g, dims, passes):
    _, N1, N2, P, Q, K1 = dims

    nb = FFT_BATCH * FFT_GROUPS

    def body(i, carry):
        n2s = [i * nb + j for j in range(nb)]
        xs = [tbuf[pl.ds(n2, N1 // 2, stride=P), :] for n2 in n2s]
        outs = [_mm(g, pl.multiple_of(n2 * 2 * K1, 2 * SUBLANES), 2 * K1, x, passes) for n2, x in zip(n2s, xs)]
        for n2, out in zip(n2s, outs):
            sbuf[pl.ds(pl.multiple_of(n2 * Q, SUBLANES), 2 * K1), :] = out
        return carry

    lax.fori_loop(0, N2 // nb, body, 0)


def _fft_stage2_load(sbuf, k1, dims):
    _, N1, N2, _, Q, K1 = dims
    re = [sbuf[pl.ds(k1 + j, N2, stride=Q), :] for j in range(FFT_BATCH)]
    im = [sbuf[pl.ds(K1 + k1 + j, N2, stride=Q), :] for j in range(FFT_BATCH)]
    return jnp.concatenate([jnp.concatenate(re, axis=1), jnp.concatenate(im, axis=1)], axis=0)


def _hyena_long_body(v_ref, x1_ref, x2_ref, wv_ref, w1_ref, w2_ref, bv_ref, b1_ref, b2_ref, fb_ref, h0_ref, h1_ref,
                     g_ref, ig_ref, f2_ref, if2_ref, o_ref, zbuf, g1buf, g2buf, sbuf, *, dims):
    N, N1, N2, P, Q, K1 = dims
    NH = N1 // 2
    zero = jnp.zeros((1, LANES), F32)

    for u_ref, w_ref, b_ref, buf in ((v_ref, wv_ref, bv_ref, zbuf), (x1_ref, w1_ref, b1_ref, g1buf),
                                     (x2_ref, w2_ref, b2_ref, g2buf)):
        w, b = w_ref[...], b_ref[...]
        for n1 in range(NH):
            r0 = n1 * N2
            prev_last = zero if n1 == 0 else u_ref[r0 - 1:r0, :]
            next_first = zero if n1 == NH - 1 else u_ref[r0 + N2:r0 + N2 + 1, :]
            buf[n1 * P:n1 * P + N2, :] = _short_conv_chunk(u_ref, r0, N2, prev_last, next_first, w, b)

    for o, (gbuf, h_ref) in enumerate(((g1buf, h0_ref), (g2buf, h1_ref))):
        _fft_stage1(zbuf, sbuf, (g_ref, None), dims, 1)

        def per_k1(i, carry):
            k1s = [(i * FFT_GROUPS + g) * FFT_BATCH for g in range(FFT_GROUPS)]
            Bs = [_fft_stage2_load(sbuf, k1, dims).astype(BF16) for k1 in k1s]
            Ccs = []
            for k1, B in zip(k1s, Bs):
                X = _dot(f2_ref[...], B)
                h0 = [pl.multiple_of((k1 + j) * 2 * N2, 2 * N2) for j in range(FFT_BATCH)]
                hr = jnp.concatenate([h_ref[pl.ds(r, N2), :] for r in h0], axis=1)
                hi = jnp.concatenate([h_ref[pl.ds(r + N2, N2), :] for r in h0], axis=1)
                xr, xi = X[:N2], X[N2:]
                Y = jnp.concatenate([xr * hr - xi * hi, xr * hi + xi * hr], axis=0)
                Ccs.append(_dot(if2_ref[...], Y.astype(BF16)))
            for k1, Cc in zip(k1s, Ccs):
                for j in range(FFT_BATCH):
                    lanes = slice(j * LANES, (j + 1) * LANES)
                    sbuf[pl.ds(k1 + j, N2, stride=Q), :] = Cc[:N2, lanes]
                    sbuf[pl.ds(K1 + k1 + j, N2, stride=Q), :] = Cc[N2:, lanes]
            return carry

        lax.fori_loop(0, K1 // (FFT_BATCH * FFT_GROUPS), per_k1, 0)
        fb = fb_ref[o:o + 1, :]

        nb = FFT_BATCH * FFT_GROUPS

        def per_n2(i, carry):
            n2s = [i * nb + j for j in range(nb)]
            Ds = [sbuf[pl.ds(pl.multiple_of(n2 * Q, SUBLANES), 2 * K1), :].astype(BF16) for n2 in n2s]
            ys = [_dot(ig_ref[pl.ds(pl.multiple_of(n2 * NH, NH), NH), :], D) for n2, D in zip(n2s, Ds)]
            for n2, y in zip(n2s, ys):
                rows = pl.ds(n2, NH, stride=P)
                zbuf[rows, :] = gbuf[rows, :] * (y + fb * zbuf[rows, :])
            return carry

        lax.fori_loop(0, N2 // nb, per_n2, 0)

    for n1 in range(NH):
        o_ref[n1 * N2:(n1 + 1) * N2, :] = zbuf[n1 * P:n1 * P + N2, :]


def _fft_spectrum_body(hf_ref, hb_ref, s_ref, g_hi, g_lo, f2_hi, f2_lo, o_ref, tbuf, sbuf, *, dims):
    N, N1, N2, P, Q, K1 = dims
    NH = N1 // 2
    scale = s_ref[...]
    for d, h_ref in enumerate((hf_ref, hb_ref)):
        for n1 in range(NH):
            h = h_ref[n1 * N2:(n1 + 1) * N2, :] * scale
            if d == 1 and n1 == 0:
                h = jnp.where(lax.broadcasted_iota(jnp.int32, h.shape, 0) == 0, 0.0, h)
            tbuf[n1 * P:n1 * P + N2, :] = h
        _fft_stage1(tbuf, sbuf, (g_hi, g_lo), dims, SPECTRUM_PASSES)

        def per_k1(i, carry):
            k1 = i * FFT_BATCH
            X = _mm((f2_hi, f2_lo), 0, 2 * N2, _fft_stage2_load(sbuf, k1, dims), SPECTRUM_PASSES)
            for j in range(FFT_BATCH):
                h0 = pl.multiple_of((k1 + j) * 2 * N2, 2 * N2)
                Xj = X[:, j * LANES:(j + 1) * LANES]
                if d == 0:
                    o_ref[pl.ds(h0, 2 * N2), :] = Xj
                else:
                    o_ref[pl.ds(h0, N2), :] += Xj[:N2]
                    o_ref[pl.ds(h0 + N2, N2), :] -= Xj[N2:]
            return carry

        lax.fori_loop(0, K1 // FFT_BATCH, per_k1, 0)


def _hyena_long(hy, lp, k_un, scale):
    B, L, _ = hy.shape
    C = D_HYENA
    dims = _fft_dims(L)
    N, N1, N2, P, Q, K1 = dims
    NH = N1 // 2
    tabs = _fft_tables(L)
    ncb = C // LANES
    nspec = HYENA_ORDER * ncb
    const1 = lambda a, n: pl.BlockSpec(a.shape, (lambda *i: (0,) * a.ndim), pipeline_mode=pl.Buffered(1))
    hcol = lambda d: pl.BlockSpec((L, LANES), lambda j: (0, (j // ncb) * 2 * ncb + d * ncb + j % ncb))
    H = pl.pallas_call(
        functools.partial(_fft_spectrum_body, dims=dims),
        grid=(nspec,),
        in_specs=[hcol(0), hcol(1), pl.BlockSpec((1, LANES), lambda j: (0, j)),
                  const1(tabs['g'][0], 1), const1(tabs['g'][1], 1),
                  const1(tabs['f2'][0], 1), const1(tabs['f2'][1], 1)],
        out_specs=pl.BlockSpec((K1 * 2 * N2, LANES), lambda j: (0, j)),
        out_shape=jax.ShapeDtypeStruct((K1 * 2 * N2, HYENA_ORDER * C), F32),
        scratch_shapes=[pltpu.VMEM((NH * P, LANES), F32), pltpu.VMEM((N2 * Q, LANES), F32)],
        name="filter_spectrum",
        compiler_params=_params("arbitrary"),
    )(k_un, k_un, scale, tabs['g'][0], tabs['g'][1], tabs['f2'][0], tabs['f2'][1])

    one = pl.Buffered(1)
    hspec = lambda o: pl.BlockSpec((K1 * 2 * N2, LANES), lambda c, b: (0, o * ncb + c), pipeline_mode=one)
    u = lambda g: pl.BlockSpec((None, L, LANES), lambda c, b: (b, 0, g * ncb + c), pipeline_mode=one)
    cw = lambda g: pl.BlockSpec((3, LANES), lambda c, b: (0, g * ncb + c))
    cb = lambda g: pl.BlockSpec((1, LANES), lambda c, b: (0, g * ncb + c))
    return pl.pallas_call(
        functools.partial(_hyena_long_body, dims=dims),
        grid=(ncb, B),
        in_specs=[u(0), u(1), u(2), cw(0), cw(1), cw(2), cb(0), cb(1), cb(2),
                  pl.BlockSpec((HYENA_ORDER, LANES), lambda c, b: (0, c)),
                  hspec(0), hspec(1),
                  const1(tabs['g'][0], 2), const1(tabs['ig'][0], 2),
                  const1(tabs['f2'][0], 2), const1(tabs['if2'][0], 2)],
        out_specs=pl.BlockSpec((None, L, LANES), lambda c, b: (b, 0, c)),
        out_shape=jax.ShapeDtypeStruct((B, L, C), F32),
        scratch_shapes=[pltpu.VMEM((NH * P, LANES), F32), pltpu.VMEM((NH * P, LANES), F32),
                        pltpu.VMEM((NH * P, LANES), F32), pltpu.VMEM((N2 * Q, LANES), F32)],
        name="hyena_long",
        compiler_params=_params("arbitrary", "arbitrary"),
    )(hy, hy, hy, lp['conv_w'], lp['conv_w'], lp['conv_w'], lp['conv_b'], lp['conv_b'], lp['conv_b'],
      lp['f_bias'], H, H, tabs['g'][0], tabs['ig'][0], tabs['f2'][0], tabs['if2'][0])


def _hyena(hy, lp):
    L = hy.shape[1]
    k_un, scale = _hyena_filters(L, lp)
    if 2 * L >= 2 * FFT_N1 * SUBLANES and (2 * L) % (FFT_N1 * SUBLANES) == 0:
        return _hyena_long(hy, lp, k_un, scale)
    return _hyena_short(hy, lp, k_un, scale)


def kernel(x_prompt, x_sample, cache_k, cache_v, c, c_ctx, w_mod, b_mod, g_mix, w_in, conv_w, conv_b, f_w1, f_b1,
           f_freq, f_w2, f_b2, f_w3, f_bias, rpb, g_out_hy, g_out_at, w_out, g_ffn, w_router, b_router, w_exp1,
           b_exp1, w_exp2, b_exp2, g_final):
    depth = w_mod.shape[0]
    Bp, Lp, D = x_prompt.shape
    Bs, Ls, _ = x_sample.shape
    n_ctx, n_lat = Bp * Lp, Bs * Ls
    n_tok = n_ctx + n_lat
    d_attn = N_HEADS * HEAD_DIM
    q_off = 3 * D_HYENA
    row2 = lambda a: a.reshape(1, -1)

    cond = jnp.zeros((SUBLANES, D), F32).at[0].set(c_ctx).at[1:1 + Bs].set(c)
    xp, xs = x_prompt, x_sample
    new_k, new_v = [], []
    for l in range(depth):
        wr = jnp.zeros((D, LANES), F32).at[:, :N_EXPERTS].set(w_router[l])
        wr_hi = wr.astype(BF16)
        lp = {
            'conv_w': conv_w[l], 'conv_b': row2(conv_b[l]), 'f_w1': f_w1[l], 'f_b1': row2(f_b1[l]),
            'f_freq': row2(f_freq[l]), 'f_w2': f_w2[l], 'f_b2': row2(f_b2[l]), 'f_w3': f_w3[l],
            'f_bias': f_bias[l], 'g_out_hy': row2(g_out_hy[l]), 'g_out_at': row2(g_out_at[l]),
            'w_out': w_out[l].astype(BF16), 'g_ffn': row2(g_ffn[l]),
            'wr_hi': wr_hi, 'wr_lo': (wr - wr_hi.astype(F32)).astype(BF16),
            'b_router': jnp.zeros((1, LANES), F32).at[0, :N_EXPERTS].set(b_router[l]),
            'w_exp1': w_exp1[l], 'b_exp1': b_exp1[l][:, None, :], 'w_exp2': w_exp2[l],
            'b_exp2': b_exp2[l][:, None, :],
        }
        mod = _modulation(cond, w_mod[l], row2(b_mod[l]))
        mod3 = mod.reshape(SUBLANES, 1, 6 * D)
        w_in_b = w_in[l].astype(BF16)
        g_mix_l = row2(g_mix[l])

        proj_p = _in_projection(xp, mod3, 0, g_mix_l, w_in_b)
        proj_s = _in_projection(xs, mod3, 1, g_mix_l, w_in_b)
        kv = proj_p[..., q_off + d_attn:].reshape(Bp, Lp, 2, N_HEADS, HEAD_DIM).transpose(2, 0, 3, 1, 4)
        new_k.append(kv[0])
        new_v.append(kv[1])

        hy_p = _hyena(proj_p, lp)
        hy_s = _hyena(proj_s, lp)
        at_p = _context_attention(proj_p, q_off)
        heads_last = lambda t: t.transpose(0, 2, 1, 3).reshape(Bs, t.shape[2], d_attn)
        at_s = _neighbourhood_attention(proj_s, q_off, heads_last(cache_k[:, l]), heads_last(cache_v[:, l]), rpb[l])

        x1, h2u, logits = _out_projection((hy_p, at_p, xp), (hy_s, at_s, xs), mod3, lp)

        gates, ce, nv, n_used, gidx, sidx = _routing(logits[:, :N_EXPERTS], n_tok)
        yp = _experts(ce, nv, n_used, gidx, sidx, h2u, lp, n_tok * TOP_K + MOE_CHUNK)
        last = l == depth - 1
        gf = row2(g_final) if last else None
        assert last, "deeper stacks need the un-normalised residual between layers"
        xp = _combine(x1, yp, gates, mod3, 0, gf, 0, Bp, Lp)
        xs = _combine(x1, yp, gates, mod3, 1, gf, n_ctx, Bs, Ls)

    return xp, xs, jnp.stack(new_k, axis=1), jnp.stack(new_v, axis=1)
```

```python
import functools
import math

import numpy as np
import jax
import jax.numpy as jnp
from jax import lax
from jax.experimental import pallas as pl
from jax.experimental.pallas import tpu as pltpu

F32 = jnp.float32
BF16 = jnp.bfloat16
HIGHEST = lax.Precision.HIGHEST

GRID_W = 64
N_HEADS = 16
HEAD_DIM = 64
D_HYENA = 1024
HYENA_ORDER = 2
FILTER_BANDS = 16
DECAY_TARGET = 1e-2
DECAY_PCT_SHORT = 0.3
DECAY_PCT_LONG = 1.5
WIN_H = 8
WIN_W = 16
N_EXPERTS = 32
TOP_K = 4
SWIGLU_ALPHA = 1.702
SWIGLU_LIMIT = 7.0
EPS = 1e-6
NEG_INF = -1e30

LANES = 128
SUBLANES = 8
VMEM_LIMIT = 60 * 1024 * 1024

ROW_TILE = 256
NA_Q_ROWS = 4
MOE_CHUNK = 1024
MOE_SUB = 256
MOE_FF_TILE = 512
MOE_ISSUE = 64


def _params(*sem):
    return pltpu.CompilerParams(dimension_semantics=sem, vmem_limit_bytes=VMEM_LIMIT)


def _rms(x, g):
    return x * lax.rsqrt(jnp.mean(x * x, axis=-1, keepdims=True) + EPS) * g


def _split_bf16(x):
    hi = x.astype(BF16)
    lo = (x - hi.astype(F32)).astype(BF16)
    return hi, lo


def _pack_bf16_pair(lo, hi):
    bl = lax.bitcast_convert_type(lo.astype(BF16).astype(F32), jnp.uint32)
    bh = lax.bitcast_convert_type(hi.astype(BF16).astype(F32), jnp.uint32)
    return (bl >> 16) | (bh & jnp.uint32(0xFFFF0000))


def _unpack_bf16_pair(words):
    lo = lax.bitcast_convert_type(words << 16, F32)
    hi = lax.bitcast_convert_type(words & jnp.uint32(0xFFFF0000), F32)
    return lo, hi


def _dot(a, b):
    return jnp.dot(a, b, preferred_element_type=F32)


def _dot_nt(a, b):
    return lax.dot_general(a, b, (((1,), (1,)), ((), ())), preferred_element_type=F32)


def _mod_body(c_ref, w_ref, b_ref, o_ref):
    c = c_ref[...]
    s = c / (1.0 + jnp.exp(-c))
    o_ref[...] = jnp.dot(s, w_ref[...], precision=HIGHEST, preferred_element_type=F32) + b_ref[...]


def _modulation(cc, w, b):
    D, N = w.shape
    tn = min(N, 1536)
    return pl.pallas_call(
        _mod_body,
        grid=(N // tn,),
        in_specs=[pl.BlockSpec((SUBLANES, D), lambda j: (0, 0)),
                  pl.BlockSpec((D, tn), lambda j: (0, j)),
                  pl.BlockSpec((1, tn), lambda j: (0, j))],
        out_specs=pl.BlockSpec((SUBLANES, tn), lambda j: (0, j)),
        out_shape=jax.ShapeDtypeStruct((SUBLANES, N), F32),
        name="modulation",
        compiler_params=_params("arbitrary"),
    )(cc, w, b)


def _inproj_body(x_ref, shift_ref, scale_ref, g_ref, w_ref, o_ref, *, n_chunk):
    h = _rms(x_ref[...], g_ref[...]) * (1.0 + scale_ref[...]) + shift_ref[...]
    hb = h.astype(BF16)
    n_out = o_ref.shape[-1]

    def col(j, carry):
        c0 = pl.multiple_of(j * n_chunk, n_chunk)
        o_ref[:, pl.ds(c0, n_chunk)] = _dot(hb, w_ref[:, pl.ds(c0, n_chunk)])
        return carry

    lax.fori_loop(0, n_out // n_chunk, col, 0)


def _in_projection(x, mod3, row0, g, w_bf16):
    B, L, D = x.shape
    N = w_bf16.shape[1]
    tm = min(ROW_TILE, L)
    mrow = (lambda b: 0) if row0 == 0 else (lambda b: b + row0)
    return pl.pallas_call(
        functools.partial(_inproj_body, n_chunk=512),
        grid=(B, L // tm),
        in_specs=[pl.BlockSpec((None, tm, D), lambda b, i: (b, i, 0)),
                  pl.BlockSpec((None, 1, D), lambda b, i: (mrow(b), 0, 0)),
                  pl.BlockSpec((None, 1, D), lambda b, i: (mrow(b), 0, 1)),
                  pl.BlockSpec((1, D), lambda b, i: (0, 0)),
                  pl.BlockSpec((D, N), lambda b, i: (0, 0), pipeline_mode=pl.Buffered(1))],
        out_specs=pl.BlockSpec((None, tm, N), lambda b, i: (b, i, 0)),
        out_shape=jax.ShapeDtypeStruct((B, L, N), F32),
        name="in_projection",
        compiler_params=_params("arbitrary", "arbitrary"),
    )(x, mod3, mod3, g, w_bf16)


def _head_lane_masks(rows, hd):
    lane = lax.broadcasted_iota(jnp.int32, (rows, LANES), 1)
    return [(lane >= h * hd) & (lane < (h + 1) * hd) for h in range(LANES // hd)]


def _ctx_attn_body(q_ref, k_ref, v_ref, o_ref, *, scale, hd):
    q = q_ref[...]
    k = k_ref[...].astype(BF16)
    v = v_ref[...].astype(BF16)
    masks = _head_lane_masks(q.shape[0], hd)
    out = None
    for msk in masks:
        qh = jnp.where(msk, q, 0.0).astype(BF16)
        s = _dot_nt(qh, k) * scale
        p = jnp.exp(s - jnp.max(s, axis=-1, keepdims=True))
        o = _dot(p.astype(BF16), v) / jnp.sum(p, axis=-1, keepdims=True)
        out = o if out is None else jnp.where(msk, o, out)
    o_ref[...] = out


def _context_attention(proj, q_off):
    B, L, _ = proj.shape
    d_attn = N_HEADS * HEAD_DIM
    nhp = d_attn // LANES
    qb, kb, vb = (q_off // LANES, (q_off + d_attn) // LANES, (q_off + 2 * d_attn) // LANES)
    spec = lambda base: pl.BlockSpec((None, L, LANES), lambda b, h: (b, 0, base + h))
    return pl.pallas_call(
        functools.partial(_ctx_attn_body, scale=1.0 / math.sqrt(HEAD_DIM), hd=HEAD_DIM),
        grid=(B, nhp),
        in_specs=[spec(qb), spec(kb), spec(vb)],
        out_specs=pl.BlockSpec((None, L, LANES), lambda b, h: (b, 0, h)),
        out_shape=jax.ShapeDtypeStruct((B, L, d_attn), F32),
        name="context_attention",
        compiler_params=_params("arbitrary", "arbitrary"),
    )(proj, proj, proj)


def _na_tables(rows, rpb):
    W = GRID_W
    kh = min(WIN_H, rows)
    rbq = NA_Q_ROWS
    kwr = rbq + kh
    assert rows % rbq == 0 and kwr <= rows
    nblk = rows // rbq
    ws = np.clip(np.arange(nblk) * rbq - kh // 2, 0, rows - kwr)
    rq = np.arange(nblk)[:, None, None] * rbq + np.arange(rbq)[None, :, None]
    rk = ws[:, None, None] + np.arange(kwr)[None, None, :]
    rs = np.clip(rq - kh // 2, 0, rows - kh)
    vr = (rk >= rs) & (rk < rs + kh)
    dr = np.where(vr, rk - rq + WIN_H - 1, 0)
    patterns = np.concatenate([dr.reshape(nblk, -1), vr.reshape(nblk, -1)], axis=1)
    _, first, btype = np.unique(patterns, axis=0, return_index=True, return_inverse=True)
    dr_t, vr_t = dr[first], vr[first]
    cq = np.arange(W)[:, None]
    ck = np.arange(W)[None, :]
    cs = np.clip(cq - WIN_W // 2, 0, W - WIN_W)
    vc = (ck >= cs) & (ck < cs + WIN_W)
    dc = np.clip(ck - cq, -(WIN_W - 1), WIN_W - 1) + WIN_W - 1
    nt = dr_t.shape[0]
    nr, ncol = 2 * WIN_H - 1, 2 * WIN_W - 1
    rowsel = np.where(vr_t, dr_t, nr)[..., None] == np.arange(nr + 1)
    colsel = np.where(vc, dc, ncol)[..., None] == np.arange(ncol + 1)
    rpb_ext = jnp.full((rpb.shape[0], nr + 1, ncol + 1), NEG_INF, F32).at[:, :nr, :ncol].set(rpb.astype(F32))
    bias = jnp.einsum('tajr,hrc,qkc->htaqjk', rowsel.astype(np.float32), rpb_ext,
                      colsel.astype(np.float32), precision=HIGHEST)
    bias = bias.reshape(rpb.shape[0], nt, rbq * W, kwr * W)
    return ws.astype(np.int32), btype.reshape(-1).astype(np.int32), bias


def _na_body(ws_ref, bt_ref, q_ref, k_ref, v_ref, kc_ref, vc_ref, bias_ref, o_ref, *, scale, hd, nk):
    i = pl.program_id(2)
    start = pl.multiple_of(ws_ref[i] * GRID_W, GRID_W)
    q = q_ref[...] * scale
    kl = k_ref[pl.ds(start, nk), :].astype(BF16)
    vl = v_ref[pl.ds(start, nk), :]
    kc = kc_ref[...].astype(BF16)
    vc = vc_ref[...]
    masks = _head_lane_masks(q.shape[0], hd)
    vl_masks = _head_lane_masks(nk, hd)
    vc_masks = _head_lane_masks(vc.shape[0], hd)
    assert len(masks) == 2, "the value lanes of the other head carry the softmax denominator"
    out = None
    for h, msk in enumerate(masks):
        qh = jnp.where(msk, q, 0.0).astype(BF16)
        sl = _dot_nt(qh, kl) + bias_ref[h]
        sc = _dot_nt(qh, kc)
        m = jnp.maximum(jnp.max(sl, axis=-1, keepdims=True), jnp.max(sc, axis=-1, keepdims=True))
        p_l = jnp.exp(sl - m).astype(BF16)
        p_c = jnp.exp(sc - m).astype(BF16)
        o = (_dot(p_l, jnp.where(vl_masks[h], vl, 1.0).astype(BF16))
             + _dot(p_c, jnp.where(vc_masks[h], vc, 1.0).astype(BF16)))
        o = o / pltpu.roll(o, hd, axis=1)
        out = o if out is None else jnp.where(msk, o, out)
    o_ref[...] = out


def _neighbourhood_attention(proj, q_off, kc, vc, rpb):
    B, L, _ = proj.shape
    Lc = kc.shape[1]
    d_attn = N_HEADS * HEAD_DIM
    hpb = LANES // HEAD_DIM
    nhp = d_attn // LANES
    rows = L // GRID_W
    ws, btype, bias = _na_tables(rows, rpb)
    nq = NA_Q_ROWS * GRID_W
    nk = (NA_Q_ROWS + min(WIN_H, rows)) * GRID_W
    nblk = rows // NA_Q_ROWS
    qb, kb, vb = (q_off // LANES, (q_off + d_attn) // LANES, (q_off + 2 * d_attn) // LANES)
    full = lambda base: pl.BlockSpec((None, L, LANES), lambda b, h, i, ws_r, bt_r: (b, 0, base + h))
    ctx = pl.BlockSpec((None, Lc, LANES), lambda b, h, i, ws_r, bt_r: (b, 0, h))
    grid_spec = pltpu.PrefetchScalarGridSpec(
        num_scalar_prefetch=2,
        grid=(B, nhp, nblk),
        in_specs=[pl.BlockSpec((None, nq, LANES), lambda b, h, i, ws_r, bt_r: (b, i, qb + h)),
                  full(kb), full(vb), ctx, ctx,
                  pl.BlockSpec((hpb, None, nq, nk), lambda b, h, i, ws_r, bt_r: (h, bt_r[i], 0, 0))],
        out_specs=pl.BlockSpec((None, nq, LANES), lambda b, h, i, ws_r, bt_r: (b, i, h)),
    )
    return pl.pallas_call(
        functools.partial(_na_body, scale=1.0 / math.sqrt(HEAD_DIM), hd=HEAD_DIM, nk=nk),
        grid_spec=grid_spec,
        out_shape=jax.ShapeDtypeStruct((B, L, d_attn), F32),
        name="neighbourhood_attention",
        compiler_params=_params("arbitrary", "arbitrary", "arbitrary"),
    )(jnp.asarray(ws), jnp.asarray(btype), proj, proj, proj, kc, vc, bias)


def _outproj_body(yhp_ref, yap_ref, xp_ref, yhs_ref, yas_ref, xs_ref, gate1_ref, shift2_ref, scale2_ref, ghy_ref,
                  gat_ref, w_ref, gffn_ref, wrh_ref, wrl_ref, br_ref, x1_ref, h2u_ref, lg_ref, *, n_ctx_tiles):
    def run(yh_ref, ya_ref, x_ref):
        cat = jnp.concatenate([_rms(yh_ref[...], ghy_ref[...]), _rms(ya_ref[...], gat_ref[...])], axis=-1)
        mix = _dot(cat.astype(BF16), w_ref[...])
        x1 = x_ref[...] + gate1_ref[...] * mix
        x1_ref[...] = x1
        h2 = _rms(x1, gffn_ref[...]) * (1.0 + scale2_ref[...]) + shift2_ref[...]
        hi, lo = _split_bf16(h2)
        lg_ref[...] = (_dot(hi, wrh_ref[...]) + _dot(lo, wrh_ref[...]) + _dot(hi, wrl_ref[...])) + br_ref[...]
        half = h2.shape[-1] // 2
        words = _pack_bf16_pair(h2[:, :half], h2[:, half:])
        per = half // LANES
        for s in range(per):
            h2u_ref[pl.ds(s, words.shape[0], stride=per), :] = words[:, s * LANES:(s + 1) * LANES]

    is_ctx = pl.program_id(0) < n_ctx_tiles
    pl.when(is_ctx)(lambda: run(yhp_ref, yap_ref, xp_ref))
    pl.when(jnp.logical_not(is_ctx))(lambda: run(yhs_ref, yas_ref, xs_ref))


def _out_projection(ctx, lat, mod3, lp):
    Bp, Lp, D = ctx[2].shape
    Bs, Ls, _ = lat[2].shape
    dh, da = ctx[0].shape[-1], ctx[1].shape[-1]
    tm = min(ROW_TILE, Lp, Ls)
    nct, nlt = Bp * Lp // tm, Bs * Ls // tm
    n_tok = (nct + nlt) * tm
    lat_tiles = Ls // tm
    mrow = lambda i: jnp.where(i < nct, 0, 1 + (i - nct) // lat_tiles)
    modspec = lambda c: pl.BlockSpec((None, 1, D), lambda i: (mrow(i), 0, c))
    const = lambda shape: pl.BlockSpec(shape, lambda i: (0,) * len(shape))
    crow = lambda w: pl.BlockSpec((tm, w), lambda i: (jnp.minimum(i, nct - 1), 0))
    lrow = lambda w: pl.BlockSpec((tm, w), lambda i: (jnp.maximum(i - nct, 0), 0))
    flat = lambda t: t.reshape(-1, t.shape[-1])
    return pl.pallas_call(
        functools.partial(_outproj_body, n_ctx_tiles=nct),
        grid=(nct + nlt,),
        in_specs=[crow(dh), crow(da), crow(D), lrow(dh), lrow(da), lrow(D),
                  modspec(2), modspec(3), modspec(4),
                  const((1, dh)), const((1, da)),
                  pl.BlockSpec((dh + da, D), lambda i: (0, 0), pipeline_mode=pl.Buffered(1)),
                  const((1, D)), const((D, LANES)), const((D, LANES)), const((1, LANES))],
        out_specs=[pl.BlockSpec((tm, D), lambda i: (i, 0)),
                   pl.BlockSpec((tm * (D // 2 // LANES), LANES), lambda i: (i, 0)),
                   pl.BlockSpec((tm, LANES), lambda i: (i, 0))],
        out_shape=[jax.ShapeDtypeStruct((n_tok, D), F32),
                   jax.ShapeDtypeStruct((n_tok * (D // 2 // LANES), LANES), jnp.uint32),
                   jax.ShapeDtypeStruct((n_tok, LANES), F32)],
        name="out_projection",
        compiler_params=_params("arbitrary"),
    )(*[flat(t) for t in ctx], *[flat(t) for t in lat], mod3, mod3, mod3, lp['g_out_hy'], lp['g_out_at'],
      lp['w_out'], lp['g_ffn'], lp['wr_hi'], lp['wr_lo'], lp['b_router'])


def _expert_body(ce_ref, nv_ref, nu_ref, gcur_ref, gnext_ref, sprev_ref, scur_ref, h2u_hbm, w1g_ref, w1u_ref,
                 b1g_ref, b1u_ref, w2_ref, b2_ref, yp_hbm, xu, xb, acc, ystage, abuf, gsem, ssem):
    c = pl.program_id(0)
    f = pl.program_id(1)
    nc = pl.num_programs(0)
    nf = pl.num_programs(1)
    half = xb.shape[1] // 2
    xt = half // LANES
    yt = xt
    slot = c % 2
    subs = lambda n: (n + MOE_SUB - 1) // MOE_SUB
    nv = nv_ref[c]
    nsub = subs(nv)
    nrows = nsub * MOE_SUB
    last = c == nu_ref[0] - 1
    nv_prev = nv_ref[jnp.maximum(c - 1, 0)]
    nrows_prev = subs(nv_prev) * MOE_SUB
    covered_prev = MOE_ISSUE * nf * subs(nv_prev)
    nrows_next = jnp.where(c + 1 < nc, subs(nv_ref[jnp.minimum(c + 1, nc - 1)]), 0) * MOE_SUB
    covered = MOE_ISSUE * nf * nsub

    def gather_row(idx_ref, r, to_slot):
        tok = idx_ref[0, r]
        pltpu.make_async_copy(h2u_hbm.at[pl.ds(pl.multiple_of(tok * xt, xt), xt), :],
                              xu.at[to_slot, pl.ds(pl.multiple_of(r * xt, xt), xt), :], gsem.at[to_slot]).start()

    def scatter_row(dst, r):
        pltpu.make_async_copy(ystage.at[pl.ds(pl.multiple_of(r * yt, yt), yt), :],
                              yp_hbm.at[pl.ds(pl.multiple_of(dst * yt, yt), yt), :], ssem).start()

    def for_rows(lo, hi, fn):
        lax.fori_loop(lo, hi, lambda r, carry: (fn(r), carry)[1], 0)

    def drain(n, src, dst, sem, per):
        piece = lambda m: pltpu.make_async_copy(src.at[pl.ds(0, m * per), :], dst.at[pl.ds(0, m * per), :], sem)
        for_rows(0, n // MOE_SUB, lambda i: piece(MOE_SUB).wait())
        bit = MOE_SUB // 2
        while bit:
            if not isinstance(n, int):
                pl.when((n & bit) != 0)(piece(bit).wait)
            elif n & bit:
                piece(bit).wait()
            bit //= 2

    @pl.when(c < nu_ref[0])
    def _chunk():
        @pl.when(f == 0)
        def _arrive():
            @pl.when(c == 0)
            def _first():
                for_rows(0, nrows, lambda r: gather_row(gcur_ref, r, 0))

            started = jnp.where(c == 0, nrows, jnp.maximum(covered_prev, nrows))
            drain(started, h2u_hbm, xu.at[slot], gsem.at[slot], xt)

            def unpack(sb, carry):
                r0 = pl.multiple_of(sb * MOE_SUB, MOE_SUB)
                for s in range(xt):
                    lo, hi = _unpack_bf16_pair(xu[slot, pl.ds(r0 * xt + s, MOE_SUB, stride=xt), :])
                    xb[pl.ds(r0, MOE_SUB), s * LANES:(s + 1) * LANES] = lo.astype(BF16)
                    xb[pl.ds(r0, MOE_SUB), half + s * LANES:half + (s + 1) * LANES] = hi.astype(BF16)
                acc[pl.ds(r0, MOE_SUB), :] = jnp.zeros((MOE_SUB, acc.shape[1]), F32)
                return carry

            lax.fori_loop(0, nsub, unpack, 0)

        def matmul_steps(with_scatter):
            def up_proj(sb):
                base = (f * nsub + sb) * MOE_ISSUE
                for j in range(MOE_ISSUE):
                    gather_row(gnext_ref, base + j, 1 - slot)
                if with_scatter:
                    for j in range(MOE_ISSUE):
                        scatter_row(sprev_ref[0, base + j], jnp.minimum(base + j, nrows_prev - 1))
                x = xb[pl.ds(pl.multiple_of(sb * MOE_SUB, MOE_SUB), MOE_SUB), :]
                gate = jnp.minimum(_dot(x, w1g_ref[...].astype(BF16)) + b1g_ref[...], SWIGLU_LIMIT)
                up = jnp.clip(_dot(x, w1u_ref[...].astype(BF16)) + b1u_ref[...], -SWIGLU_LIMIT, SWIGLU_LIMIT)
                glu = gate / (1.0 + jnp.exp(-SWIGLU_ALPHA * gate))
                abuf[...] = ((up + 1.0) * glu).astype(BF16)

            def down_proj(sb):
                acc[pl.ds(pl.multiple_of(sb * MOE_SUB, MOE_SUB), MOE_SUB), :] += _dot(abuf[...], w2_ref[...].astype(BF16))

            def step(sb, carry):
                down_proj(sb - 1)
                up_proj(sb)
                return carry

            up_proj(0)
            lax.fori_loop(1, nsub, step, 0)
            down_proj(nsub - 1)

        pl.when(c == 0)(lambda: matmul_steps(False))
        pl.when(c > 0)(lambda: matmul_steps(True))

        @pl.when(f == nf - 1)
        def _leave():
            for_rows(covered, nrows_next, lambda r: gather_row(gnext_ref, r, 1 - slot))

            @pl.when(c > 0)
            def _prev_out():
                for_rows(covered, nv_prev, lambda r: scatter_row(sprev_ref[0, r], r))
                drain(jnp.maximum(covered, nv_prev), yp_hbm, ystage, ssem, yt)

            def stage(sb, carry):
                r0 = pl.multiple_of(sb * MOE_SUB, MOE_SUB)
                for s in range(yt):
                    lo, hi = slice(s * LANES, (s + 1) * LANES), slice(half + s * LANES, half + (s + 1) * LANES)
                    ystage[pl.ds(r0 * yt + s, MOE_SUB, stride=yt), :] = _pack_bf16_pair(
                        acc[pl.ds(r0, MOE_SUB), lo] + b2_ref[:, lo], acc[pl.ds(r0, MOE_SUB), hi] + b2_ref[:, hi])
                return carry

            lax.fori_loop(0, nsub, stage, 0)

            @pl.when(last)
            def _flush():
                for_rows(0, nv, lambda r: scatter_row(scur_ref[0, r], r))
                spare0 = yp_hbm.shape[0] // yt - xb.shape[0]
                for_rows(0, xb.shape[0], lambda r: scatter_row(spare0 + r, jnp.minimum(r, nrows - 1)))
                drain(nv, yp_hbm, ystage, ssem, yt)
                drain(xb.shape[0], yp_hbm, ystage, ssem, yt)
                drain(jnp.maximum(covered, nrows_next), h2u_hbm, xu.at[1 - slot], gsem.at[1 - slot], xt)


def _experts(chunk_e, chunk_nv, n_used, gidx, sidx, h2u, lp, n_out_rows):
    E, D, two_ff = lp['w_exp1'].shape
    d_ff = two_ff // 2
    nc = gidx.shape[0]
    R, tf = MOE_CHUNK, MOE_FF_TILE
    nf = d_ff // tf
    assert MOE_ISSUE * nf * (R // MOE_SUB) <= R, "a chunk's matmul steps must not start more rows than a chunk holds"

    def ff(c, f, nu):
        return jnp.where(c < nu[0], f, nf - 1)

    smem = lambda step: pl.BlockSpec((None, 1, R), lambda c, f, ce, nv, nu: (jnp.clip(c + step, 0, nc - 1), 0, 0),
                                     memory_space=pltpu.SMEM)
    grid_spec = pltpu.PrefetchScalarGridSpec(
        num_scalar_prefetch=3,
        grid=(nc, nf),
        in_specs=[smem(0), smem(1), smem(-1), smem(0),
                  pl.BlockSpec(memory_space=pl.ANY),
                  pl.BlockSpec((None, D, tf), lambda c, f, ce, nv, nu: (ce[c], 0, ff(c, f, nu))),
                  pl.BlockSpec((None, D, tf), lambda c, f, ce, nv, nu: (ce[c], 0, nf + ff(c, f, nu))),
                  pl.BlockSpec((None, 1, tf), lambda c, f, ce, nv, nu: (ce[c], 0, ff(c, f, nu))),
                  pl.BlockSpec((None, 1, tf), lambda c, f, ce, nv, nu: (ce[c], 0, nf + ff(c, f, nu))),
                  pl.BlockSpec((None, tf, D), lambda c, f, ce, nv, nu: (ce[c], ff(c, f, nu), 0)),
                  pl.BlockSpec((None, 1, D), lambda c, f, ce, nv, nu: (ce[c], 0, 0))],
        out_specs=pl.BlockSpec(memory_space=pl.ANY),
        scratch_shapes=[pltpu.VMEM((2, R * (D // 2 // LANES), LANES), jnp.uint32),
                        pltpu.VMEM((R, D), BF16),
                        pltpu.VMEM((R, D), F32),
                        pltpu.VMEM((R * (D // 2 // LANES), LANES), jnp.uint32),
                        pltpu.VMEM((MOE_SUB, tf), BF16),
                        pltpu.SemaphoreType.DMA((2,)),
                        pltpu.SemaphoreType.DMA(())],
    )
    return pl.pallas_call(
        _expert_body,
        grid_spec=grid_spec,
        out_shape=jax.ShapeDtypeStruct((n_out_rows * (D // 2 // LANES), LANES), jnp.uint32),
        name="experts",
        compiler_params=pltpu.CompilerParams(dimension_semantics=("arbitrary", "arbitrary"),
                                             vmem_limit_bytes=VMEM_LIMIT, has_side_effects=True,
                                             disable_bounds_checks=True),
    )(chunk_e, chunk_nv, n_used, gidx, gidx, sidx, sidx, h2u, lp['w_exp1'], lp['w_exp1'], lp['b_exp1'],
      lp['b_exp1'], lp['w_exp2'], lp['b_exp2'])


def _routing(logits, n_tok):
    R = MOE_CHUNK
    top_v, top_i = lax.top_k(logits, TOP_K)
    gates = jax.nn.softmax(top_v, axis=-1)
    n_pairs = n_tok * TOP_K
    flat_e = top_i.reshape(n_pairs).astype(jnp.int32)
    experts = jnp.arange(N_EXPERTS, dtype=jnp.int32)
    counts = jnp.sum((flat_e[:, None] == experts[None]).astype(jnp.int32), axis=0)
    padded = (counts + R - 1) // R * R
    pad_end = jnp.cumsum(padded)
    pad_start = pad_end - padded
    nc = n_pairs // R + N_EXPERTS
    pair = jnp.arange(n_pairs, dtype=jnp.int32)
    fill_e = jnp.repeat(experts, R)
    fill_i = jnp.tile(jnp.arange(R, dtype=jnp.int32), N_EXPERTS)
    fill_key = jnp.where(fill_i < jnp.repeat(padded - counts, R), 2 * fill_e + 1, 2 * N_EXPERTS)
    _, pid = lax.sort((jnp.concatenate([2 * flat_e, fill_key]), jnp.concatenate([pair, n_pairs + fill_i])),
                      num_keys=1)
    real = pid < n_pairs
    gidx = jnp.where(real, pid // TOP_K, 0)
    sidx = jnp.where(real, (pid % TOP_K) * n_tok + pid // TOP_K, pid)
    n_used = (pad_end[-1] // R).astype(jnp.int32)
    cstart = jnp.arange(nc, dtype=jnp.int32) * R
    ce = jnp.minimum(jnp.searchsorted(pad_end, cstart, side='right'), N_EXPERTS - 1).astype(jnp.int32)
    ce = jnp.where(jnp.arange(nc) < n_used, ce, ce[jnp.maximum(n_used - 1, 0)])
    nv = jnp.clip(counts[ce] - (cstart - pad_start[ce]), 0, R).astype(jnp.int32)
    nv = jnp.where(jnp.arange(nc) < n_used, nv, 0)
    return gates, ce, nv, n_used.reshape(1), gidx.reshape(nc, 1, R), sidx.reshape(nc, 1, R)


def _combine_body(x1_ref, y0_ref, y1_ref, y2_ref, y3_ref, g_ref, gate2_ref, gf_ref, o_ref):
    g = g_ref[...]
    tm, D = x1_ref.shape
    per = D // 2 // LANES

    def rows(y_ref):
        parts = [_unpack_bf16_pair(y_ref[pl.ds(s, tm, stride=per), :]) for s in range(per)]
        return jnp.concatenate([p[0] for p in parts] + [p[1] for p in parts], axis=-1)

    ff = (g[:, 0:1] * rows(y0_ref) + g[:, 1:2] * rows(y1_ref)) + (g[:, 2:3] * rows(y2_ref) + g[:, 3:4] * rows(y3_ref))
    x2 = x1_ref[...] + gate2_ref[...] * ff
    o_ref[...] = _rms(x2, gf_ref[...])


def _combine(x1, yp, gates, mod3, row0, g_final, tok0, B, L):
    n_tok, D = x1.shape
    tm = min(ROW_TILE, L)
    nl = L // tm
    blk0 = tok0 // tm
    nblk_tok = n_tok // tm
    mrow = (lambda b: 0) if row0 == 0 else (lambda b: b + row0)
    row = lambda b, i: (blk0 + b * nl + i, 0)
    yspec = lambda k: pl.BlockSpec((tm * (D // 2 // LANES), LANES),
                                   lambda b, i: (k * nblk_tok + blk0 + b * nl + i, 0))
    return pl.pallas_call(
        _combine_body,
        grid=(B, nl),
        in_specs=[pl.BlockSpec((tm, D), row), yspec(0), yspec(1), yspec(2), yspec(3),
                  pl.BlockSpec((tm, TOP_K), row),
                  pl.BlockSpec((None, 1, D), lambda b, i: (mrow(b), 0, 5)),
                  pl.BlockSpec((1, D), lambda b, i: (0, 0))],
        out_specs=pl.BlockSpec((None, tm, D), lambda b, i: (b, i, 0)),
        out_shape=jax.ShapeDtypeStruct((B, L, D), F32),
        name="combine",
        compiler_params=_params("arbitrary", "arbitrary"),
    )(x1, yp, yp, yp, yp, gates, mod3, g_final)


def _filter_body(tw_ref, bands_ref, w1_ref, b1_ref, fr_ref, w2_ref, b2_ref, w3_ref, dl_ref, k_ref, ss_ref):
    hi = functools.partial(jnp.dot, precision=HIGHEST, preferred_element_type=F32)
    t = tw_ref[:, 0:1]
    w = tw_ref[:, 1:2]
    lane = lax.broadcasted_iota(jnp.int32, (t.shape[0], LANES), 1)
    fw = w * bands_ref[...]
    z = jnp.where(lane == 0, t,
                  jnp.where(lane <= FILTER_BANDS, jnp.cos(fw),
                            jnp.where(lane <= 2 * FILTER_BANDS, -jnp.sin(fw), 0.0)))
    fr = fr_ref[...]
    h = jnp.sin(fr * (hi(z, w1_ref[...]) + b1_ref[...]))
    h = jnp.sin(fr * (hi(h, w2_ref[...]) + b2_ref[...]))
    decay = jnp.exp(-t * dl_ref[...])
    C = decay.shape[1]
    ss = []
    for g in range(k_ref.shape[1] // C):
        kg = hi(h, w3_ref[:, g * C:(g + 1) * C]) * decay
        k_ref[:, g * C:(g + 1) * C] = kg
        ss.append(jnp.sum(kg * kg, axis=0, keepdims=True))
    ss = jnp.concatenate(ss, axis=-1)

    @pl.when(pl.program_id(0) == 0)
    def _first():
        ss_ref[...] = ss

    @pl.when(pl.program_id(0) > 0)
    def _rest():
        ss_ref[...] += ss


def _hyena_filters(L, lp):
    C = D_HYENA
    pos = jnp.arange(L, dtype=F32)
    tw = jnp.stack([pos / max(L - 1, 1), 2 * math.pi * pos / L], axis=-1)
    bands = jnp.linspace(1e-4, FILTER_BANDS - 1, FILTER_BANDS, dtype=F32)
    bands128 = jnp.zeros((1, LANES), F32).at[0, 1:1 + 2 * FILTER_BANDS].set(jnp.concatenate([bands, bands]))
    w1 = jnp.zeros((LANES, lp['f_w1'].shape[1]), F32).at[:lp['f_w1'].shape[0]].set(lp['f_w1'])
    deltas = jnp.abs(jnp.linspace(math.log(DECAY_TARGET) / DECAY_PCT_LONG,
                                  math.log(DECAY_TARGET) / DECAY_PCT_SHORT, C, dtype=F32))[None]
    nk = lp['f_w3'].shape[1]
    tl = min(L, 256)
    args = [tw, bands128, w1, lp['f_b1'], lp['f_freq'], lp['f_w2'], lp['f_b2'], lp['f_w3'], deltas]
    const = lambda a: pl.BlockSpec(a.shape, lambda i: (0,) * a.ndim)
    k_un, ss = pl.pallas_call(
        _filter_body,
        grid=(L // tl,),
        in_specs=[pl.BlockSpec((tl, 2), lambda i: (i, 0))] + [const(a) for a in args[1:]],
        out_specs=[pl.BlockSpec((tl, nk), lambda i: (i, 0)), pl.BlockSpec((1, nk), lambda i: (0, 0))],
        out_shape=[jax.ShapeDtypeStruct((L, nk), F32), jax.ShapeDtypeStruct((1, nk), F32)],
        name="hyena_filters",
        compiler_params=_params("arbitrary"),
    )(*args)
    ss = ss.reshape(HYENA_ORDER, 2, C)
    scale = lax.rsqrt(ss[:, 0] + ss[:, 1] + EPS).reshape(1, HYENA_ORDER * C)
    return k_un, scale


def _short_conv_chunk(u_ref, r0, n, prev_last, next_first, w, b):
    u = u_ref[pl.ds(r0, n), :]
    row = lax.broadcasted_iota(jnp.int32, u.shape, 0)
    up = jnp.where(row == 0, prev_last, pltpu.roll(u, 1, axis=0))
    un = jnp.where(row == n - 1, next_first, pltpu.roll(u, n - 1, axis=0))
    return up * w[0:1] + u * w[1:2] + un * w[2:3] + b


def _dft_spectrum_body(a_ref, b_ref, s_ref, o_ref):
    o_ref[...] = jnp.dot(a_ref[...], b_ref[...], precision=HIGHEST, preferred_element_type=F32) * s_ref[...]


def _dft_spectrum(a, b, scale):
    M, K = a.shape
    ncol = b.shape[1]
    tn = min(ncol, 512)
    return pl.pallas_call(
        _dft_spectrum_body,
        grid=(ncol // tn,),
        in_specs=[pl.BlockSpec((M, K), lambda j: (0, 0)), pl.BlockSpec((K, tn), lambda j: (0, j)),
                  pl.BlockSpec((1, tn), lambda j: (0, j))],
        out_specs=pl.BlockSpec((M, tn), lambda j: (0, j)),
        out_shape=jax.ShapeDtypeStruct((M, ncol), F32),
        name="dft_spectrum",
        compiler_params=_params("arbitrary"),
    )(a, b, scale)


def _hyena_short_body(v_ref, x1_ref, x2_ref, wv_ref, w1_ref, w2_ref, bv_ref, b1_ref, b2_ref, fb_ref, h0_ref, h1_ref,
                      ff_ref, if_ref, o_ref):
    L = v_ref.shape[0]
    N = ff_ref.shape[0] // 2
    conv = lambda u_ref, w_ref, b_ref: _short_conv_chunk(u_ref, 0, L, 0.0, 0.0, w_ref[...], b_ref[...])
    z = conv(v_ref, wv_ref, bv_ref)
    gates = (conv(x1_ref, w1_ref, b1_ref), conv(x2_ref, w2_ref, b2_ref))
    for o, (gate, h_ref) in enumerate(zip(gates, (h0_ref, h1_ref))):
        X = _dot(ff_ref[...], z.astype(BF16))
        xr, xi = X[:N], X[N:]
        hr, hi = h_ref[:N, :], h_ref[N:, :]
        Y = jnp.concatenate([xr * hr - xi * hi, xr * hi + xi * hr], axis=0)
        y = _dot(if_ref[...], Y.astype(BF16))
        z = gate * (y + fb_ref[o:o + 1, :] * z)
    o_ref[...] = z


def _hyena_short(hy, lp, k_un, scale):
    B, L, _ = hy.shape
    C = D_HYENA
    N = 2 * L
    ct = 256
    ncb = C // ct
    k4 = k_un.reshape(L, HYENA_ORDER, 2, C)
    taps = jnp.concatenate([k4[:, :, 0], jnp.zeros((1, HYENA_ORDER, C), F32), k4[:0:-1, :, 1]], axis=0)
    KF = (N // 2 + 1 + SUBLANES - 1) // SUBLANES * SUBLANES
    k = np.arange(KF)
    kept = (k <= N // 2)[:, None]
    weight = np.where((k == 0) | (k == N // 2), 1.0, 2.0)[:, None] * kept / N
    ang = 2 * np.pi * np.outer(k, np.arange(N)) / N
    dft = jnp.asarray(np.concatenate([np.cos(ang) * kept, -np.sin(ang) * kept], axis=0), F32)
    H = _dft_spectrum(dft, taps.reshape(N, HYENA_ORDER * C), scale)
    fwd = jnp.asarray(np.concatenate([np.cos(ang[:, :L]) * kept, -np.sin(ang[:, :L]) * kept], axis=0), BF16)
    inv = jnp.asarray(np.concatenate([np.cos(ang[:, :L]) * weight, -np.sin(ang[:, :L]) * weight], axis=0).T, BF16)
    u = lambda g: pl.BlockSpec((None, L, ct), lambda c, b: (b, 0, g * ncb + c))
    cw = lambda g: pl.BlockSpec((3, ct), lambda c, b: (0, g * ncb + c))
    cb = lambda g: pl.BlockSpec((1, ct), lambda c, b: (0, g * ncb + c))
    const = lambda a: pl.BlockSpec(a.shape, lambda c, b: (0,) * a.ndim)
    return pl.pallas_call(
        _hyena_short_body,
        grid=(ncb, B),
        in_specs=[u(0), u(1), u(2), cw(0), cw(1), cw(2), cb(0), cb(1), cb(2),
                  pl.BlockSpec((HYENA_ORDER, ct), lambda c, b: (0, c)),
                  pl.BlockSpec((2 * KF, ct), lambda c, b: (0, c)),
                  pl.BlockSpec((2 * KF, ct), lambda c, b: (0, ncb + c)),
                  const(fwd), const(inv)],
        out_specs=pl.BlockSpec((None, L, ct), lambda c, b: (b, 0, c)),
        out_shape=jax.ShapeDtypeStruct((B, L, C), F32),
        name="hyena_short",
        compiler_params=_params("arbitrary", "arbitrary"),
    )(hy, hy, hy, lp['conv_w'], lp['conv_w'], lp['conv_w'], lp['conv_b'], lp['conv_b'], lp['conv_b'],
      lp['f_bias'], H, H, fwd, inv)


FFT_N1 = 128
FFT_BATCH = 4
FFT_GROUPS = 2


def _fft_dims(L):
    N = 2 * L
    N1 = FFT_N1
    N2 = N // N1
    step = FFT_BATCH * FFT_GROUPS
    K1 = (N1 // 2 + 1 + step - 1) // step * step
    assert N1 * N2 == N and N2 % SUBLANES == 0 and K1 % SUBLANES == 0
    return N, N1, N2, N2 + SUBLANES, 2 * K1 + SUBLANES, K1


def _fft_tables(L):
    N, N1, N2, _, _, K1 = _fft_dims(L)
    NH = N1 // 2
    n1 = np.arange(NH)
    k1 = np.arange(K1)
    n2 = np.arange(N2)
    kept = (k1 <= N1 // 2)[None, :, None]
    weight = np.where((k1 == 0) | (k1 == N1 // 2), 1.0, 2.0)[None, :, None] * kept / N
    th = 2 * np.pi * (n1[None, None, :] * k1[None, :, None] / N1 + n2[:, None, None] * k1[None, :, None] / N)
    g = np.concatenate([np.cos(th) * kept, -np.sin(th) * kept], axis=1).reshape(N2 * 2 * K1, NH)
    ig = np.concatenate([np.cos(th) * weight, -np.sin(th) * weight], axis=1).transpose(0, 2, 1)
    ig = ig.reshape(N2 * NH, 2 * K1)
    ph = 2 * np.pi * np.outer(n2, n2) / N2
    c, s = np.cos(ph), np.sin(ph)
    f2 = np.block([[c, s], [-s, c]])
    if2 = np.block([[c, -s], [s, c]])

    return {name: jnp.asarray(a, BF16) for name, a in (('g', g), ('ig', ig), ('f2', f2), ('if2', if2))}


def _mm(tab_ref, r0, nrows, x):
    return _dot(tab_ref[pl.ds(r0, nrows), :], x.astype(BF16))


def _fft_stage1(tbuf, sbuf, g, dims):
    _, N1, N2, P, Q, K1 = dims

    nb = FFT_BATCH * FFT_GROUPS

    def body(i, carry):
        n2s = [i * nb + j for j in range(nb)]
        xs = [tbuf[pl.ds(n2, N1 // 2, stride=P), :] for n2 in n2s]
        outs = [_mm(g, pl.multiple_of(n2 * 2 * K1, 2 * SUBLANES), 2 * K1, x) for n2, x in zip(n2s, xs)]
        for n2, out in zip(n2s, outs):
            sbuf[pl.ds(pl.multiple_of(n2 * Q, SUBLANES), 2 * K1), :] = out
        return carry

    lax.fori_loop(0, N2 // nb, body, 0)


def _fft_stage2_load(sbuf, k1, dims):
    _, N1, N2, _, Q, K1 = dims
    re = [sbuf[pl.ds(k1 + j, N2, stride=Q), :] for j in range(FFT_BATCH)]
    im = [sbuf[pl.ds(K1 + k1 + j, N2, stride=Q), :] for j in range(FFT_BATCH)]
    return jnp.concatenate([jnp.concatenate(re, axis=1), jnp.concatenate(im, axis=1)], axis=0)


def _hyena_long_body(v_ref, x1_ref, x2_ref, wv_ref, w1_ref, w2_ref, bv_ref, b1_ref, b2_ref, fb_ref, h0_ref, h1_ref,
                     g_ref, ig_ref, f2_ref, if2_ref, o_ref, zbuf, g1buf, g2buf, sbuf, *, dims):
    N, N1, N2, P, Q, K1 = dims
    NH = N1 // 2
    zero = jnp.zeros((1, LANES), F32)

    for u_ref, w_ref, b_ref, buf in ((v_ref, wv_ref, bv_ref, zbuf), (x1_ref, w1_ref, b1_ref, g1buf),
                                     (x2_ref, w2_ref, b2_ref, g2buf)):
        w, b = w_ref[...], b_ref[...]
        for n1 in range(NH):
            r0 = n1 * N2
            prev_last = zero if n1 == 0 else u_ref[r0 - 1:r0, :]
            next_first = zero if n1 == NH - 1 else u_ref[r0 + N2:r0 + N2 + 1, :]
            buf[n1 * P:n1 * P + N2, :] = _short_conv_chunk(u_ref, r0, N2, prev_last, next_first, w, b)

    for o, (gbuf, h_ref) in enumerate(((g1buf, h0_ref), (g2buf, h1_ref))):
        _fft_stage1(zbuf, sbuf, g_ref, dims)

        def per_k1(i, carry):
            k1s = [(i * FFT_GROUPS + g) * FFT_BATCH for g in range(FFT_GROUPS)]
            Bs = [_fft_stage2_load(sbuf, k1, dims).astype(BF16) for k1 in k1s]
            Ccs = []
            for k1, B in zip(k1s, Bs):
                X = _dot(f2_ref[...], B)
                h0 = [pl.multiple_of((k1 + j) * 2 * N2, 2 * N2) for j in range(FFT_BATCH)]
                hr = jnp.concatenate([h_ref[pl.ds(r, N2), :] for r in h0], axis=1)
                hi = jnp.concatenate([h_ref[pl.ds(r + N2, N2), :] for r in h0], axis=1)
                xr, xi = X[:N2], X[N2:]
                Y = jnp.concatenate([xr * hr - xi * hi, xr * hi + xi * hr], axis=0)
                Ccs.append(_dot(if2_ref[...], Y.astype(BF16)))
            for k1, Cc in zip(k1s, Ccs):
                for j in range(FFT_BATCH):
                    lanes = slice(j * LANES, (j + 1) * LANES)
                    sbuf[pl.ds(k1 + j, N2, stride=Q), :] = Cc[:N2, lanes]
                    sbuf[pl.ds(K1 + k1 + j, N2, stride=Q), :] = Cc[N2:, lanes]
            return carry

        lax.fori_loop(0, K1 // (FFT_BATCH * FFT_GROUPS), per_k1, 0)
        fb = fb_ref[o:o + 1, :]

        nb = FFT_BATCH * FFT_GROUPS

        def per_n2(i, carry):
            n2s = [i * nb + j for j in range(nb)]
            Ds = [sbuf[pl.ds(pl.multiple_of(n2 * Q, SUBLANES), 2 * K1), :].astype(BF16) for n2 in n2s]
            ys = [_dot(ig_ref[pl.ds(pl.multiple_of(n2 * NH, NH), NH), :], D) for n2, D in zip(n2s, Ds)]
            for n2, y in zip(n2s, ys):
                rows = pl.ds(n2, NH, stride=P)
                zbuf[rows, :] = gbuf[rows, :] * (y + fb * zbuf[rows, :])
            return carry

        lax.fori_loop(0, N2 // nb, per_n2, 0)

    for n1 in range(NH):
        o_ref[n1 * N2:(n1 + 1) * N2, :] = zbuf[n1 * P:n1 * P + N2, :]


def _fft_spectrum_body(hf_ref, hb_ref, s_ref, g_ref, f2_ref, o_ref, tbuf, sbuf, *, dims):
    N, N1, N2, P, Q, K1 = dims
    NH = N1 // 2
    scale = s_ref[...]
    for d, h_ref in enumerate((hf_ref, hb_ref)):
        for n1 in range(NH):
            h = h_ref[n1 * N2:(n1 + 1) * N2, :] * scale
            if d == 1 and n1 == 0:
                h = jnp.where(lax.broadcasted_iota(jnp.int32, h.shape, 0) == 0, 0.0, h)
            tbuf[n1 * P:n1 * P + N2, :] = h
        _fft_stage1(tbuf, sbuf, g_ref, dims)

        def per_k1(i, carry):
            k1 = i * FFT_BATCH
            X = _mm(f2_ref, 0, 2 * N2, _fft_stage2_load(sbuf, k1, dims))
            for j in range(FFT_BATCH):
                h0 = pl.multiple_of((k1 + j) * 2 * N2, 2 * N2)
                Xj = X[:, j * LANES:(j + 1) * LANES]
                if d == 0:
                    o_ref[pl.ds(h0, 2 * N2), :] = Xj
                else:
                    o_ref[pl.ds(h0, N2), :] += Xj[:N2]
                    o_ref[pl.ds(h0 + N2, N2), :] -= Xj[N2:]
            return carry

        lax.fori_loop(0, K1 // FFT_BATCH, per_k1, 0)


def _hyena_long(hy, lp, k_un, scale):
    B, L, _ = hy.shape
    C = D_HYENA
    dims = _fft_dims(L)
    N, N1, N2, P, Q, K1 = dims
    NH = N1 // 2
    tabs = _fft_tables(L)
    ncb = C // LANES
    nspec = HYENA_ORDER * ncb
    const1 = lambda a: pl.BlockSpec(a.shape, (lambda *i: (0,) * a.ndim), pipeline_mode=pl.Buffered(1))
    hcol = lambda d: pl.BlockSpec((L, LANES), lambda j: (0, (j // ncb) * 2 * ncb + d * ncb + j % ncb))
    H = pl.pallas_call(
        functools.partial(_fft_spectrum_body, dims=dims),
        grid=(nspec,),
        in_specs=[hcol(0), hcol(1), pl.BlockSpec((1, LANES), lambda j: (0, j)),
                  const1(tabs['g']), const1(tabs['f2'])],
        out_specs=pl.BlockSpec((K1 * 2 * N2, LANES), lambda j: (0, j)),
        out_shape=jax.ShapeDtypeStruct((K1 * 2 * N2, HYENA_ORDER * C), F32),
        scratch_shapes=[pltpu.VMEM((NH * P, LANES), F32), pltpu.VMEM((N2 * Q, LANES), F32)],
        name="filter_spectrum",
        compiler_params=_params("arbitrary"),
    )(k_un, k_un, scale, tabs['g'], tabs['f2'])

    one = pl.Buffered(1)
    hspec = lambda o: pl.BlockSpec((K1 * 2 * N2, LANES), lambda c, b: (0, o * ncb + c), pipeline_mode=one)
    u = lambda g: pl.BlockSpec((None, L, LANES), lambda c, b: (b, 0, g * ncb + c), pipeline_mode=one)
    cw = lambda g: pl.BlockSpec((3, LANES), lambda c, b: (0, g * ncb + c))
    cb = lambda g: pl.BlockSpec((1, LANES), lambda c, b: (0, g * ncb + c))
    return pl.pallas_call(
        functools.partial(_hyena_long_body, dims=dims),
        grid=(ncb, B),
        in_specs=[u(0), u(1), u(2), cw(0), cw(1), cw(2), cb(0), cb(1), cb(2),
                  pl.BlockSpec((HYENA_ORDER, LANES), lambda c, b: (0, c)),
                  hspec(0), hspec(1),
                  const1(tabs['g']), const1(tabs['ig']), const1(tabs['f2']), const1(tabs['if2'])],
        out_specs=pl.BlockSpec((None, L, LANES), lambda c, b: (b, 0, c)),
        out_shape=jax.ShapeDtypeStruct((B, L, C), F32),
        scratch_shapes=[pltpu.VMEM((NH * P, LANES), F32), pltpu.VMEM((NH * P, LANES), F32),
                        pltpu.VMEM((NH * P, LANES), F32), pltpu.VMEM((N2 * Q, LANES), F32)],
        name="hyena_long",
        compiler_params=_params("arbitrary", "arbitrary"),
    )(hy, hy, hy, lp['conv_w'], lp['conv_w'], lp['conv_w'], lp['conv_b'], lp['conv_b'], lp['conv_b'],
      lp['f_bias'], H, H, tabs['g'], tabs['ig'], tabs['f2'], tabs['if2'])


def _hyena(hy, lp):
    L = hy.shape[1]
    k_un, scale = _hyena_filters(L, lp)
    if 2 * L >= 2 * FFT_N1 * SUBLANES and (2 * L) % (FFT_N1 * SUBLANES) == 0:
        return _hyena_long(hy, lp, k_un, scale)
    return _hyena_short(hy, lp, k_un, scale)


def kernel(x_prompt, x_sample, cache_k, cache_v, c, c_ctx, w_mod, b_mod, g_mix, w_in, conv_w, conv_b, f_w1, f_b1,
           f_freq, f_w2, f_b2, f_w3, f_bias, rpb, g_out_hy, g_out_at, w_out, g_ffn, w_router, b_router, w_exp1,
           b_exp1, w_exp2, b_exp2, g_final):
    depth = w_mod.shape[0]
    Bp, Lp, D = x_prompt.shape
    Bs, Ls, _ = x_sample.shape
    n_ctx, n_lat = Bp * Lp, Bs * Ls
    n_tok = n_ctx + n_lat
    d_attn = N_HEADS * HEAD_DIM
    q_off = 3 * D_HYENA
    row2 = lambda a: a.reshape(1, -1)

    cond = jnp.zeros((SUBLANES, D), F32).at[0].set(c_ctx).at[1:1 + Bs].set(c)
    xp, xs = x_prompt, x_sample
    new_k, new_v = [], []
    for l in range(depth):
        wr = jnp.zeros((D, LANES), F32).at[:, :N_EXPERTS].set(w_router[l])
        wr_hi = wr.astype(BF16)
        lp = {
            'conv_w': conv_w[l], 'conv_b': row2(conv_b[l]), 'f_w1': f_w1[l], 'f_b1': row2(f_b1[l]),
            'f_freq': row2(f_freq[l]), 'f_w2': f_w2[l], 'f_b2': row2(f_b2[l]), 'f_w3': f_w3[l],
            'f_bias': f_bias[l], 'g_out_hy': row2(g_out_hy[l]), 'g_out_at': row2(g_out_at[l]),
            'w_out': w_out[l].astype(BF16), 'g_ffn': row2(g_ffn[l]),
            'wr_hi': wr_hi, 'wr_lo': (wr - wr_hi.astype(F32)).astype(BF16),
            'b_router': jnp.zeros((1, LANES), F32).at[0, :N_EXPERTS].set(b_router[l]),
            'w_exp1': w_exp1[l], 'b_exp1': b_exp1[l][:, None, :], 'w_exp2': w_exp2[l],
            'b_exp2': b_exp2[l][:, None, :],
        }
        mod = _modulation(cond, w_mod[l], row2(b_mod[l]))
        mod3 = mod.reshape(SUBLANES, 1, 6 * D)
        w_in_b = w_in[l].astype(BF16)
        g_mix_l = row2(g_mix[l])

        proj_p = _in_projection(xp, mod3, 0, g_mix_l, w_in_b)
        proj_s = _in_projection(xs, mod3, 1, g_mix_l, w_in_b)
        kv = proj_p[..., q_off + d_attn:].reshape(Bp, Lp, 2, N_HEADS, HEAD_DIM).transpose(2, 0, 3, 1, 4)
        new_k.append(kv[0])
        new_v.append(kv[1])

        hy_p = _hyena(proj_p, lp)
        hy_s = _hyena(proj_s, lp)
        at_p = _context_attention(proj_p, q_off)
        heads_last = lambda t: t.transpose(0, 2, 1, 3).reshape(Bs, t.shape[2], d_attn)
        at_s = _neighbourhood_attention(proj_s, q_off, heads_last(cache_k[:, l]), heads_last(cache_v[:, l]), rpb[l])

        x1, h2u, logits = _out_projection((hy_p, at_p, xp), (hy_s, at_s, xs), mod3, lp)

        gates, ce, nv, n_used, gidx, sidx = _routing(logits[:, :N_EXPERTS], n_tok)
        yp = _experts(ce, nv, n_used, gidx, sidx, h2u, lp, n_tok * TOP_K + MOE_CHUNK)
        last = l == depth - 1
        gf = row2(g_final) if last else None
        assert last, "deeper stacks need the un-normalised residual between layers"
        xp = _combine(x1, yp, gates, mod3, 0, gf, 0, Bp, Lp)
        xs = _combine(x1, yp, gates, mod3, 1, gf, n_ctx, Bs, Ls)

    return xp, xs, jnp.stack(new_k, axis=1), jnp.stack(new_v, axis=1)
```

```python
import functools
import math

import numpy as np
import jax
import jax.numpy as jnp
from jax import lax
from jax.experimental import pallas as pl
from jax.experimental.pallas import tpu as pltpu

F32 = jnp.float32
BF16 = jnp.bfloat16
HIGHEST = lax.Precision.HIGHEST

GRID_W = 64
N_HEADS = 16
HEAD_DIM = 64
D_HYENA = 1024
HYENA_ORDER = 2
FILTER_BANDS = 16
DECAY_TARGET = 1e-2
DECAY_PCT_SHORT = 0.3
DECAY_PCT_LONG = 1.5
WIN_H = 8
WIN_W = 16
N_EXPERTS = 32
TOP_K = 4
SWIGLU_ALPHA = 1.702
SWIGLU_LIMIT = 7.0
EPS = 1e-6
NEG_INF = -1e30

LANES = 128
SUBLANES = 8
VMEM_LIMIT = 60 * 1024 * 1024

ROW_TILE = 256
NA_Q_ROWS = 2
MOE_CHUNK = 1024
MOE_SUB = 256
MOE_FF_TILE = 512
MOE_ISSUE = 64


def _params(*sem):
    return pltpu.CompilerParams(dimension_semantics=sem, vmem_limit_bytes=VMEM_LIMIT)


def _rms(x, g):
    return x * lax.rsqrt(jnp.mean(x * x, axis=-1, keepdims=True) + EPS) * g


def _split_bf16(x):
    hi = x.astype(BF16)
    lo = (x - hi.astype(F32)).astype(BF16)
    return hi, lo


def _pack_bf16_pair(lo, hi):
    bl = lax.bitcast_convert_type(lo.astype(BF16).astype(F32), jnp.uint32)
    bh = lax.bitcast_convert_type(hi.astype(BF16).astype(F32), jnp.uint32)
    return (bl >> 16) | (bh & jnp.uint32(0xFFFF0000))


def _unpack_bf16_pair(words):
    lo = lax.bitcast_convert_type(words << 16, F32)
    hi = lax.bitcast_convert_type(words & jnp.uint32(0xFFFF0000), F32)
    return lo, hi


def _dot(a, b):
    return jnp.dot(a, b, preferred_element_type=F32)


def _dot_nt(a, b):
    return lax.dot_general(a, b, (((1,), (1,)), ((), ())), preferred_element_type=F32)


def _mod_body(c_ref, w_ref, b_ref, o_ref):
    c = c_ref[...]
    s = c / (1.0 + jnp.exp(-c))
    o_ref[...] = jnp.dot(s, w_ref[...], precision=HIGHEST, preferred_element_type=F32) + b_ref[...]


def _modulation(cc, w, b):
    D, N = w.shape
    tn = min(N, 1536)
    return pl.pallas_call(
        _mod_body,
        grid=(N // tn,),
        in_specs=[pl.BlockSpec((SUBLANES, D), lambda j: (0, 0)),
                  pl.BlockSpec((D, tn), lambda j: (0, j)),
                  pl.BlockSpec((1, tn), lambda j: (0, j))],
        out_specs=pl.BlockSpec((SUBLANES, tn), lambda j: (0, j)),
        out_shape=jax.ShapeDtypeStruct((SUBLANES, N), F32),
        name="modulation",
        compiler_params=_params("arbitrary"),
    )(cc, w, b)


def _inproj_body(x_ref, shift_ref, scale_ref, g_ref, w_ref, o_ref, *, n_chunk):
    h = _rms(x_ref[...], g_ref[...]) * (1.0 + scale_ref[...]) + shift_ref[...]
    hb = h.astype(BF16)
    n_out = o_ref.shape[-1]

    def col(j, carry):
        c0 = pl.multiple_of(j * n_chunk, n_chunk)
        o_ref[:, pl.ds(c0, n_chunk)] = _dot(hb, w_ref[:, pl.ds(c0, n_chunk)])
        return carry

    lax.fori_loop(0, n_out // n_chunk, col, 0)


def _in_projection(x, mod3, row0, g, w_bf16):
    B, L, D = x.shape
    N = w_bf16.shape[1]
    tm = min(ROW_TILE, L)
    mrow = (lambda b: 0) if row0 == 0 else (lambda b: b + row0)
    return pl.pallas_call(
        functools.partial(_inproj_body, n_chunk=512),
        grid=(B, L // tm),
        in_specs=[pl.BlockSpec((None, tm, D), lambda b, i: (b, i, 0)),
                  pl.BlockSpec((None, 1, D), lambda b, i: (mrow(b), 0, 0)),
                  pl.BlockSpec((None, 1, D), lambda b, i: (mrow(b), 0, 1)),
                  pl.BlockSpec((1, D), lambda b, i: (0, 0)),
                  pl.BlockSpec((D, N), lambda b, i: (0, 0), pipeline_mode=pl.Buffered(1))],
        out_specs=pl.BlockSpec((None, tm, N), lambda b, i: (b, i, 0)),
        out_shape=jax.ShapeDtypeStruct((B, L, N), F32),
        name="in_projection",
        compiler_params=_params("arbitrary", "arbitrary"),
    )(x, mod3, mod3, g, w_bf16)


def _head_lane_masks(rows, hd):
    lane = lax.broadcasted_iota(jnp.int32, (rows, LANES), 1)
    return [(lane >= h * hd) & (lane < (h + 1) * hd) for h in range(LANES // hd)]


def _ctx_attn_body(q_ref, k_ref, v_ref, o_ref, *, scale, hd):
    q = q_ref[...]
    k = k_ref[...].astype(BF16)
    v = v_ref[...].astype(BF16)
    masks = _head_lane_masks(q.shape[0], hd)
    out = None
    for msk in masks:
        qh = jnp.where(msk, q, 0.0).astype(BF16)
        s = _dot_nt(qh, k) * scale
        p = jnp.exp(s - jnp.max(s, axis=-1, keepdims=True))
        o = _dot(p.astype(BF16), v) / jnp.sum(p, axis=-1, keepdims=True)
        out = o if out is None else jnp.where(msk, o, out)
    o_ref[...] = out


def _context_attention(proj, q_off):
    B, L, _ = proj.shape
    d_attn = N_HEADS * HEAD_DIM
    nhp = d_attn // LANES
    qb, kb, vb = (q_off // LANES, (q_off + d_attn) // LANES, (q_off + 2 * d_attn) // LANES)
    spec = lambda base: pl.BlockSpec((None, L, LANES), lambda b, h: (b, 0, base + h))
    return pl.pallas_call(
        functools.partial(_ctx_attn_body, scale=1.0 / math.sqrt(HEAD_DIM), hd=HEAD_DIM),
        grid=(B, nhp),
        in_specs=[spec(qb), spec(kb), spec(vb)],
        out_specs=pl.BlockSpec((None, L, LANES), lambda b, h: (b, 0, h)),
        out_shape=jax.ShapeDtypeStruct((B, L, d_attn), F32),
        name="context_attention",
        compiler_params=_params("arbitrary", "arbitrary"),
    )(proj, proj, proj)


def _na_tables(rows, rpb):
    W = GRID_W
    kh = min(WIN_H, rows)
    rbq = NA_Q_ROWS
    kwr = rbq + kh
    assert rows % rbq == 0 and kwr <= rows
    nblk = rows // rbq
    ws = np.clip(np.arange(nblk) * rbq - kh // 2, 0, rows - kwr)
    rq = np.arange(nblk)[:, None, None] * rbq + np.arange(rbq)[None, :, None]
    rk = ws[:, None, None] + np.arange(kwr)[None, None, :]
    rs = np.clip(rq - kh // 2, 0, rows - kh)
    vr = (rk >= rs) & (rk < rs + kh)
    dr = np.where(vr, rk - rq + WIN_H - 1, 0)
    patterns = np.concatenate([dr.reshape(nblk, -1), vr.reshape(nblk, -1)], axis=1)
    _, first, btype = np.unique(patterns, axis=0, return_index=True, return_inverse=True)
    dr_t, vr_t = dr[first], vr[first]
    cq = np.arange(W)[:, None]
    ck = np.arange(W)[None, :]
    cs = np.clip(cq - WIN_W // 2, 0, W - WIN_W)
    vc = (ck >= cs) & (ck < cs + WIN_W)
    dc = np.clip(ck - cq, -(WIN_W - 1), WIN_W - 1) + WIN_W - 1
    nt = dr_t.shape[0]
    nr, ncol = 2 * WIN_H - 1, 2 * WIN_W - 1
    rowsel = np.where(vr_t, dr_t, nr)[..., None] == np.arange(nr + 1)
    colsel = np.where(vc, dc, ncol)[..., None] == np.arange(ncol + 1)
    rpb_ext = jnp.full((rpb.shape[0], nr + 1, ncol + 1), NEG_INF, F32).at[:, :nr, :ncol].set(rpb.astype(F32))
    bias = jnp.einsum('tajr,hrc,qkc->htaqjk', rowsel.astype(np.float32), rpb_ext,
                      colsel.astype(np.float32), precision=HIGHEST)
    bias = bias.reshape(rpb.shape[0], nt, rbq * W, kwr * W)
    return ws.astype(np.int32), btype.reshape(-1).astype(np.int32), bias


def _na_body(ws_ref, bt_ref, q_ref, k_ref, v_ref, kc_ref, vc_ref, bias_ref, o_ref, *, scale, hd, nk):
    i = pl.program_id(2)
    start = pl.multiple_of(ws_ref[i] * GRID_W, GRID_W)
    q = q_ref[...] * scale
    kl = k_ref[pl.ds(start, nk), :].astype(BF16)
    vl = v_ref[pl.ds(start, nk), :]
    kc = kc_ref[...].astype(BF16)
    vc = vc_ref[...]
    masks = _head_lane_masks(q.shape[0], hd)
    vl_masks = _head_lane_masks(nk, hd)
    vc_masks = _head_lane_masks(vc.shape[0], hd)
    assert len(masks) == 2, "the value lanes of the other head carry the softmax denominator"
    out = None
    for h, msk in enumerate(masks):
        qh = jnp.where(msk, q, 0.0).astype(BF16)
        sl = _dot_nt(qh, kl) + bias_ref[h]
        sc = _dot_nt(qh, kc)
        m = jnp.maximum(jnp.max(sl, axis=-1, keepdims=True), jnp.max(sc, axis=-1, keepdims=True))
        p_l = jnp.exp(sl - m).astype(BF16)
        p_c = jnp.exp(sc - m).astype(BF16)
        o = (_dot(p_l, jnp.where(vl_masks[h], vl, 1.0).astype(BF16))
             + _dot(p_c, jnp.where(vc_masks[h], vc, 1.0).astype(BF16)))
        o = o / pltpu.roll(o, hd, axis=1)
        out = o if out is None else jnp.where(msk, o, out)
    o_ref[...] = out


def _neighbourhood_attention(proj, q_off, kc, vc, rpb):
    B, L, _ = proj.shape
    Lc = kc.shape[1]
    d_attn = N_HEADS * HEAD_DIM
    hpb = LANES // HEAD_DIM
    nhp = d_attn // LANES
    rows = L // GRID_W
    ws, btype, bias = _na_tables(rows, rpb)
    nq = NA_Q_ROWS * GRID_W
    nk = (NA_Q_ROWS + min(WIN_H, rows)) * GRID_W
    nblk = rows // NA_Q_ROWS
    qb, kb, vb = (q_off // LANES, (q_off + d_attn) // LANES, (q_off + 2 * d_attn) // LANES)
    full = lambda base: pl.BlockSpec((None, L, LANES), lambda b, h, i, ws_r, bt_r: (b, 0, base + h))
    ctx = pl.BlockSpec((None, Lc, LANES), lambda b, h, i, ws_r, bt_r: (b, 0, h))
    grid_spec = pltpu.PrefetchScalarGridSpec(
        num_scalar_prefetch=2,
        grid=(B, nhp, nblk),
        in_specs=[pl.BlockSpec((None, nq, LANES), lambda b, h, i, ws_r, bt_r: (b, i, qb + h)),
                  full(kb), full(vb), ctx, ctx,
                  pl.BlockSpec((hpb, None, nq, nk), lambda b, h, i, ws_r, bt_r: (h, bt_r[i], 0, 0))],
        out_specs=pl.BlockSpec((None, nq, LANES), lambda b, h, i, ws_r, bt_r: (b, i, h)),
    )
    return pl.pallas_call(
        functools.partial(_na_body, scale=1.0 / math.sqrt(HEAD_DIM), hd=HEAD_DIM, nk=nk),
        grid_spec=grid_spec,
        out_shape=jax.ShapeDtypeStruct((B, L, d_attn), F32),
        name="neighbourhood_attention",
        compiler_params=_params("arbitrary", "arbitrary", "arbitrary"),
    )(jnp.asarray(ws), jnp.asarray(btype), proj, proj, proj, kc, vc, bias)


def _outproj_body(yhp_ref, yap_ref, xp_ref, yhs_ref, yas_ref, xs_ref, gate1_ref, shift2_ref, scale2_ref, ghy_ref,
                  gat_ref, w_ref, gffn_ref, wrh_ref, wrl_ref, br_ref, x1_ref, h2u_ref, lg_ref, *, n_ctx_tiles):
    def run(yh_ref, ya_ref, x_ref):
        cat = jnp.concatenate([_rms(yh_ref[...], ghy_ref[...]), _rms(ya_ref[...], gat_ref[...])], axis=-1)
        mix = _dot(cat.astype(BF16), w_ref[...])
        x1 = x_ref[...] + gate1_ref[...] * mix
        x1_ref[...] = x1
        h2 = _rms(x1, gffn_ref[...]) * (1.0 + scale2_ref[...]) + shift2_ref[...]
        hi, lo = _split_bf16(h2)
        lg_ref[...] = (_dot(hi, wrh_ref[...]) + _dot(lo, wrh_ref[...]) + _dot(hi, wrl_ref[...])) + br_ref[...]
        half = h2.shape[-1] // 2
        words = _pack_bf16_pair(h2[:, :half], h2[:, half:])
        per = half // LANES
        for s in range(per):
            h2u_ref[pl.ds(s, words.shape[0], stride=per), :] = words[:, s * LANES:(s + 1) * LANES]

    is_ctx = pl.program_id(0) < n_ctx_tiles
    pl.when(is_ctx)(lambda: run(yhp_ref, yap_ref, xp_ref))
    pl.when(jnp.logical_not(is_ctx))(lambda: run(yhs_ref, yas_ref, xs_ref))


def _out_projection(ctx, lat, mod3, lp):
    Bp, Lp, D = ctx[2].shape
    Bs, Ls, _ = lat[2].shape
    dh, da = ctx[0].shape[-1], ctx[1].shape[-1]
    tm = min(ROW_TILE, Lp, Ls)
    nct, nlt = Bp * Lp // tm, Bs * Ls // tm
    n_tok = (nct + nlt) * tm
    lat_tiles = Ls // tm
    mrow = lambda i: jnp.where(i < nct, 0, 1 + (i - nct) // lat_tiles)
    modspec = lambda c: pl.BlockSpec((None, 1, D), lambda i: (mrow(i), 0, c))
    const = lambda shape: pl.BlockSpec(shape, lambda i: (0,) * len(shape))
    crow = lambda w: pl.BlockSpec((tm, w), lambda i: (jnp.minimum(i, nct - 1), 0))
    lrow = lambda w: pl.BlockSpec((tm, w), lambda i: (jnp.maximum(i - nct, 0), 0))
    flat = lambda t: t.reshape(-1, t.shape[-1])
    return pl.pallas_call(
        functools.partial(_outproj_body, n_ctx_tiles=nct),
        grid=(nct + nlt,),
        in_specs=[crow(dh), crow(da), crow(D), lrow(dh), lrow(da), lrow(D),
                  modspec(2), modspec(3), modspec(4),
                  const((1, dh)), const((1, da)),
                  pl.BlockSpec((dh + da, D), lambda i: (0, 0), pipeline_mode=pl.Buffered(1)),
                  const((1, D)), const((D, LANES)), const((D, LANES)), const((1, LANES))],
        out_specs=[pl.BlockSpec((tm, D), lambda i: (i, 0)),
                   pl.BlockSpec((tm * (D // 2 // LANES), LANES), lambda i: (i, 0)),
                   pl.BlockSpec((tm, LANES), lambda i: (i, 0))],
        out_shape=[jax.ShapeDtypeStruct((n_tok, D), F32),
                   jax.ShapeDtypeStruct((n_tok * (D // 2 // LANES), LANES), jnp.uint32),
                   jax.ShapeDtypeStruct((n_tok, LANES), F32)],
        name="out_projection",
        compiler_params=_params("arbitrary"),
    )(*[flat(t) for t in ctx], *[flat(t) for t in lat], mod3, mod3, mod3, lp['g_out_hy'], lp['g_out_at'],
      lp['w_out'], lp['g_ffn'], lp['wr_hi'], lp['wr_lo'], lp['b_router'])


def _expert_body(ce_ref, nv_ref, nu_ref, gcur_ref, gnext_ref, sprev_ref, scur_ref, h2u_hbm, w1g_ref, w1u_ref,
                 b1g_ref, b1u_ref, w2_ref, b2_ref, yp_hbm, xu, xb, acc, ystage, abuf, gsem, ssem):
    c = pl.program_id(0)
    f = pl.program_id(1)
    nc = pl.num_programs(0)
    nf = pl.num_programs(1)
    half = xb.shape[1] // 2
    xt = half // LANES
    yt = xt
    slot = c % 2
    subs = lambda n: (n + MOE_SUB - 1) // MOE_SUB
    nv = nv_ref[c]
    nsub = subs(nv)
    nrows = nsub * MOE_SUB
    last = c == nu_ref[0] - 1
    nv_prev = nv_ref[jnp.maximum(c - 1, 0)]
    nrows_prev = subs(nv_prev) * MOE_SUB
    covered_prev = MOE_ISSUE * nf * subs(nv_prev)
    nrows_next = jnp.where(c + 1 < nc, subs(nv_ref[jnp.minimum(c + 1, nc - 1)]), 0) * MOE_SUB
    covered = MOE_ISSUE * nf * nsub

    def gather_row(idx_ref, r, to_slot):
        tok = idx_ref[0, r]
        pltpu.make_async_copy(h2u_hbm.at[pl.ds(pl.multiple_of(tok * xt, xt), xt), :],
                              xu.at[to_slot, pl.ds(pl.multiple_of(r * xt, xt), xt), :], gsem.at[to_slot]).start()

    def scatter_row(dst, r):
        pltpu.make_async_copy(ystage.at[pl.ds(pl.multiple_of(r * yt, yt), yt), :],
                              yp_hbm.at[pl.ds(pl.multiple_of(dst * yt, yt), yt), :], ssem).start()

    def for_rows(lo, hi, fn):
        lax.fori_loop(lo, hi, lambda r, carry: (fn(r), carry)[1], 0)

    def drain(n, src, dst, sem, per):
        piece = lambda m: pltpu.make_async_copy(src.at[pl.ds(0, m * per), :], dst.at[pl.ds(0, m * per), :], sem)
        for_rows(0, n // MOE_SUB, lambda i: piece(MOE_SUB).wait())
        bit = MOE_SUB // 2
        while bit:
            if not isinstance(n, int):
                pl.when((n & bit) != 0)(piece(bit).wait)
            elif n & bit:
                piece(bit).wait()
            bit //= 2

    @pl.when(c < nu_ref[0])
    def _chunk():
        @pl.when(f == 0)
        def _arrive():
            @pl.when(c == 0)
            def _first():
                for_rows(0, nrows, lambda r: gather_row(gcur_ref, r, 0))

            started = jnp.where(c == 0, nrows, jnp.maximum(covered_prev, nrows))
            drain(started, h2u_hbm, xu.at[slot], gsem.at[slot], xt)

            def unpack(sb, carry):
                r0 = pl.multiple_of(sb * MOE_SUB, MOE_SUB)
                for s in range(xt):
                    lo, hi = _unpack_bf16_pair(xu[slot, pl.ds(r0 * xt + s, MOE_SUB, stride=xt), :])
                    xb[pl.ds(r0, MOE_SUB), s * LANES:(s + 1) * LANES] = lo.astype(BF16)
                    xb[pl.ds(r0, MOE_SUB), half + s * LANES:half + (s + 1) * LANES] = hi.astype(BF16)
                acc[pl.ds(r0, MOE_SUB), :] = jnp.zeros((MOE_SUB, acc.shape[1]), F32)
                return carry

            lax.fori_loop(0, nsub, unpack, 0)

        def matmul_steps(with_scatter):
            def up_proj(sb):
                base = (f * nsub + sb) * MOE_ISSUE
                for j in range(MOE_ISSUE):
                    gather_row(gnext_ref, base + j, 1 - slot)
                if with_scatter:
                    for j in range(MOE_ISSUE):
                        scatter_row(sprev_ref[0, base + j], jnp.minimum(base + j, nrows_prev - 1))
                x = xb[pl.ds(pl.multiple_of(sb * MOE_SUB, MOE_SUB), MOE_SUB), :]
                gate = jnp.minimum(_dot(x, w1g_ref[...].astype(BF16)) + b1g_ref[...], SWIGLU_LIMIT)
                up = jnp.clip(_dot(x, w1u_ref[...].astype(BF16)) + b1u_ref[...], -SWIGLU_LIMIT, SWIGLU_LIMIT)
                glu = gate / (1.0 + jnp.exp(-SWIGLU_ALPHA * gate))
                abuf[...] = ((up + 1.0) * glu).astype(BF16)

            def down_proj(sb):
                acc[pl.ds(pl.multiple_of(sb * MOE_SUB, MOE_SUB), MOE_SUB), :] += _dot(abuf[...], w2_ref[...].astype(BF16))

            def step(sb, carry):
                down_proj(sb - 1)
                up_proj(sb)
                return carry

            up_proj(0)
            lax.fori_loop(1, nsub, step, 0)
            down_proj(nsub - 1)

        pl.when(c == 0)(lambda: matmul_steps(False))
        pl.when(c > 0)(lambda: matmul_steps(True))

        @pl.when(f == nf - 1)
        def _leave():
            for_rows(covered, nrows_next, lambda r: gather_row(gnext_ref, r, 1 - slot))

            @pl.when(c > 0)
            def _prev_out():
                for_rows(covered, nv_prev, lambda r: scatter_row(sprev_ref[0, r], r))
                drain(jnp.maximum(covered, nv_prev), yp_hbm, ystage, ssem, yt)

            def stage(sb, carry):
                r0 = pl.multiple_of(sb * MOE_SUB, MOE_SUB)
                for s in range(yt):
                    lo, hi = slice(s * LANES, (s + 1) * LANES), slice(half + s * LANES, half + (s + 1) * LANES)
                    ystage[pl.ds(r0 * yt + s, MOE_SUB, stride=yt), :] = _pack_bf16_pair(
                        acc[pl.ds(r0, MOE_SUB), lo] + b2_ref[:, lo], acc[pl.ds(r0, MOE_SUB), hi] + b2_ref[:, hi])
                return carry

            lax.fori_loop(0, nsub, stage, 0)

            @pl.when(last)
            def _flush():
                for_rows(0, nv, lambda r: scatter_row(scur_ref[0, r], r))
                spare0 = yp_hbm.shape[0] // yt - xb.shape[0]
                for_rows(0, xb.shape[0], lambda r: scatter_row(spare0 + r, jnp.minimum(r, nrows - 1)))
                drain(nv, yp_hbm, ystage, ssem, yt)
                drain(xb.shape[0], yp_hbm, ystage, ssem, yt)
                drain(jnp.maximum(covered, nrows_next), h2u_hbm, xu.at[1 - slot], gsem.at[1 - slot], xt)


def _experts(chunk_e, chunk_nv, n_used, gidx, sidx, h2u, lp, n_out_rows):
    E, D, two_ff = lp['w_exp1'].shape
    d_ff = two_ff // 2
    nc = gidx.shape[0]
    R, tf = MOE_CHUNK, MOE_FF_TILE
    nf = d_ff // tf
    assert MOE_ISSUE * nf * (R // MOE_SUB) <= R, "a chunk's matmul steps must not start more rows than a chunk holds"

    def ff(c, f, nu):
        return jnp.where(c < nu[0], f, nf - 1)

    smem = lambda step: pl.BlockSpec((None, 1, R), lambda c, f, ce, nv, nu: (jnp.clip(c + step, 0, nc - 1), 0, 0),
                                     memory_space=pltpu.SMEM)
    grid_spec = pltpu.PrefetchScalarGridSpec(
        num_scalar_prefetch=3,
        grid=(nc, nf),
        in_specs=[smem(0), smem(1), smem(-1), smem(0),
                  pl.BlockSpec(memory_space=pl.ANY),
                  pl.BlockSpec((None, D, tf), lambda c, f, ce, nv, nu: (ce[c], 0, ff(c, f, nu))),
                  pl.BlockSpec((None, D, tf), lambda c, f, ce, nv, nu: (ce[c], 0, nf + ff(c, f, nu))),
                  pl.BlockSpec((None, 1, tf), lambda c, f, ce, nv, nu: (ce[c], 0, ff(c, f, nu))),
                  pl.BlockSpec((None, 1, tf), lambda c, f, ce, nv, nu: (ce[c], 0, nf + ff(c, f, nu))),
                  pl.BlockSpec((None, tf, D), lambda c, f, ce, nv, nu: (ce[c], ff(c, f, nu), 0)),
                  pl.BlockSpec((None, 1, D), lambda c, f, ce, nv, nu: (ce[c], 0, 0))],
        out_specs=pl.BlockSpec(memory_space=pl.ANY),
        scratch_shapes=[pltpu.VMEM((2, R * (D // 2 // LANES), LANES), jnp.uint32),
                        pltpu.VMEM((R, D), BF16),
                        pltpu.VMEM((R, D), F32),
                        pltpu.VMEM((R * (D // 2 // LANES), LANES), jnp.uint32),
                        pltpu.VMEM((MOE_SUB, tf), BF16),
                        pltpu.SemaphoreType.DMA((2,)),
                        pltpu.SemaphoreType.DMA(())],
    )
    return pl.pallas_call(
        _expert_body,
        grid_spec=grid_spec,
        out_shape=jax.ShapeDtypeStruct((n_out_rows * (D // 2 // LANES), LANES), jnp.uint32),
        name="experts",
        compiler_params=pltpu.CompilerParams(dimension_semantics=("arbitrary", "arbitrary"),
                                             vmem_limit_bytes=VMEM_LIMIT, has_side_effects=True,
                                             disable_bounds_checks=True),
    )(chunk_e, chunk_nv, n_used, gidx, gidx, sidx, sidx, h2u, lp['w_exp1'], lp['w_exp1'], lp['b_exp1'],
      lp['b_exp1'], lp['w_exp2'], lp['b_exp2'])


def _routing(logits, n_tok):
    R = MOE_CHUNK
    top_v, top_i = lax.top_k(logits, TOP_K)
    gates = jax.nn.softmax(top_v, axis=-1)
    n_pairs = n_tok * TOP_K
    flat_e = top_i.reshape(n_pairs).astype(jnp.int32)
    experts = jnp.arange(N_EXPERTS, dtype=jnp.int32)
    counts = jnp.sum((flat_e[:, None] == experts[None]).astype(jnp.int32), axis=0)
    padded = (counts + R - 1) // R * R
    pad_end = jnp.cumsum(padded)
    pad_start = pad_end - padded
    nc = n_pairs // R + N_EXPERTS
    pair = jnp.arange(n_pairs, dtype=jnp.int32)
    fill_e = jnp.repeat(experts, R)
    fill_i = jnp.tile(jnp.arange(R, dtype=jnp.int32), N_EXPERTS)
    fill_key = jnp.where(fill_i < jnp.repeat(padded - counts, R), 2 * fill_e + 1, 2 * N_EXPERTS)
    _, pid = lax.sort((jnp.concatenate([2 * flat_e, fill_key]), jnp.concatenate([pair, n_pairs + fill_i])),
                      num_keys=1)
    real = pid < n_pairs
    gidx = jnp.where(real, pid // TOP_K, 0)
    sidx = jnp.where(real, (pid % TOP_K) * n_tok + pid // TOP_K, pid)
    n_used = (pad_end[-1] // R).astype(jnp.int32)
    cstart = jnp.arange(nc, dtype=jnp.int32) * R
    ce = jnp.minimum(jnp.searchsorted(pad_end, cstart, side='right'), N_EXPERTS - 1).astype(jnp.int32)
    ce = jnp.where(jnp.arange(nc) < n_used, ce, ce[jnp.maximum(n_used - 1, 0)])
    nv = jnp.clip(counts[ce] - (cstart - pad_start[ce]), 0, R).astype(jnp.int32)
    nv = jnp.where(jnp.arange(nc) < n_used, nv, 0)
    return gates, ce, nv, n_used.reshape(1), gidx.reshape(nc, 1, R), sidx.reshape(nc, 1, R)


def _combine_body(x1_ref, y0_ref, y1_ref, y2_ref, y3_ref, g_ref, gate2_ref, gf_ref, o_ref):
    g = g_ref[...]
    tm, D = x1_ref.shape
    per = D // 2 // LANES

    def rows(y_ref):
        parts = [_unpack_bf16_pair(y_ref[pl.ds(s, tm, stride=per), :]) for s in range(per)]
        return jnp.concatenate([p[0] for p in parts] + [p[1] for p in parts], axis=-1)

    ff = (g[:, 0:1] * rows(y0_ref) + g[:, 1:2] * rows(y1_ref)) + (g[:, 2:3] * rows(y2_ref) + g[:, 3:4] * rows(y3_ref))
    x2 = x1_ref[...] + gate2_ref[...] * ff
    o_ref[...] = _rms(x2, gf_ref[...])


def _combine(x1, yp, gates, mod3, row0, g_final, tok0, B, L):
    n_tok, D = x1.shape
    tm = min(ROW_TILE, L)
    nl = L // tm
    blk0 = tok0 // tm
    nblk_tok = n_tok // tm
    mrow = (lambda b: 0) if row0 == 0 else (lambda b: b + row0)
    row = lambda b, i: (blk0 + b * nl + i, 0)
    yspec = lambda k: pl.BlockSpec((tm * (D // 2 // LANES), LANES),
                                   lambda b, i: (k * nblk_tok + blk0 + b * nl + i, 0))
    return pl.pallas_call(
        _combine_body,
        grid=(B, nl),
        in_specs=[pl.BlockSpec((tm, D), row), yspec(0), yspec(1), yspec(2), yspec(3),
                  pl.BlockSpec((tm, TOP_K), row),
                  pl.BlockSpec((None, 1, D), lambda b, i: (mrow(b), 0, 5)),
                  pl.BlockSpec((1, D), lambda b, i: (0, 0))],
        out_specs=pl.BlockSpec((None, tm, D), lambda b, i: (b, i, 0)),
        out_shape=jax.ShapeDtypeStruct((B, L, D), F32),
        name="combine",
        compiler_params=_params("arbitrary", "arbitrary"),
    )(x1, yp, yp, yp, yp, gates, mod3, g_final)


def _filter_body(tw_ref, bands_ref, w1_ref, b1_ref, fr_ref, w2_ref, b2_ref, w3_ref, dl_ref, k_ref, ss_ref):
    hi = functools.partial(jnp.dot, precision=HIGHEST, preferred_element_type=F32)
    t = tw_ref[:, 0:1]
    w = tw_ref[:, 1:2]
    lane = lax.broadcasted_iota(jnp.int32, (t.shape[0], LANES), 1)
    fw = w * bands_ref[...]
    z = jnp.where(lane == 0, t,
                  jnp.where(lane <= FILTER_BANDS, jnp.cos(fw),
                            jnp.where(lane <= 2 * FILTER_BANDS, -jnp.sin(fw), 0.0)))
    fr = fr_ref[...]
    h = jnp.sin(fr * (hi(z, w1_ref[...]) + b1_ref[...]))
    h = jnp.sin(fr * (hi(h, w2_ref[...]) + b2_ref[...]))
    decay = jnp.exp(-t * dl_ref[...])
    C = decay.shape[1]
    ss = []
    for g in range(k_ref.shape[1] // C):
        kg = hi(h, w3_ref[:, g * C:(g + 1) * C]) * decay
        k_ref[:, g * C:(g + 1) * C] = kg
        ss.append(jnp.sum(kg * kg, axis=0, keepdims=True))
    ss = jnp.concatenate(ss, axis=-1)

    @pl.when(pl.program_id(0) == 0)
    def _first():
        ss_ref[...] = ss

    @pl.when(pl.program_id(0) > 0)
    def _rest():
        ss_ref[...] += ss


def _hyena_filters(L, lp):
    C = D_HYENA
    pos = jnp.arange(L, dtype=F32)
    tw = jnp.stack([pos / max(L - 1, 1), 2 * math.pi * pos / L], axis=-1)
    bands = jnp.linspace(1e-4, FILTER_BANDS - 1, FILTER_BANDS, dtype=F32)
    bands128 = jnp.zeros((1, LANES), F32).at[0, 1:1 + 2 * FILTER_BANDS].set(jnp.concatenate([bands, bands]))
    w1 = jnp.zeros((LANES, lp['f_w1'].shape[1]), F32).at[:lp['f_w1'].shape[0]].set(lp['f_w1'])
    deltas = jnp.abs(jnp.linspace(math.log(DECAY_TARGET) / DECAY_PCT_LONG,
                                  math.log(DECAY_TARGET) / DECAY_PCT_SHORT, C, dtype=F32))[None]
    nk = lp['f_w3'].shape[1]
    tl = min(L, 256)
    args = [tw, bands128, w1, lp['f_b1'], lp['f_freq'], lp['f_w2'], lp['f_b2'], lp['f_w3'], deltas]
    const = lambda a: pl.BlockSpec(a.shape, lambda i: (0,) * a.ndim)
    k_un, ss = pl.pallas_call(
        _filter_body,
        grid=(L // tl,),
        in_specs=[pl.BlockSpec((tl, 2), lambda i: (i, 0))] + [const(a) for a in args[1:]],
        out_specs=[pl.BlockSpec((tl, nk), lambda i: (i, 0)), pl.BlockSpec((1, nk), lambda i: (0, 0))],
        out_shape=[jax.ShapeDtypeStruct((L, nk), F32), jax.ShapeDtypeStruct((1, nk), F32)],
        name="hyena_filters",
        compiler_params=_params("arbitrary"),
    )(*args)
    ss = ss.reshape(HYENA_ORDER, 2, C)
    scale = lax.rsqrt(ss[:, 0] + ss[:, 1] + EPS).reshape(1, HYENA_ORDER * C)
    return k_un, scale


def _short_conv_chunk(u_ref, r0, n, prev_last, next_first, w, b):
    u = u_ref[pl.ds(r0, n), :]
    row = lax.broadcasted_iota(jnp.int32, u.shape, 0)
    up = jnp.where(row == 0, prev_last, pltpu.roll(u, 1, axis=0))
    un = jnp.where(row == n - 1, next_first, pltpu.roll(u, n - 1, axis=0))
    return up * w[0:1] + u * w[1:2] + un * w[2:3] + b


def _dft_spectrum_body(a_ref, b_ref, s_ref, o_ref):
    o_ref[...] = jnp.dot(a_ref[...], b_ref[...], precision=HIGHEST, preferred_element_type=F32) * s_ref[...]


def _dft_spectrum(a, b, scale):
    M, K = a.shape
    ncol = b.shape[1]
    tn = min(ncol, 512)
    return pl.pallas_call(
        _dft_spectrum_body,
        grid=(ncol // tn,),
        in_specs=[pl.BlockSpec((M, K), lambda j: (0, 0)), pl.BlockSpec((K, tn), lambda j: (0, j)),
                  pl.BlockSpec((1, tn), lambda j: (0, j))],
        out_specs=pl.BlockSpec((M, tn), lambda j: (0, j)),
        out_shape=jax.ShapeDtypeStruct((M, ncol), F32),
        name="dft_spectrum",
        compiler_params=_params("arbitrary"),
    )(a, b, scale)


def _hyena_short_body(v_ref, x1_ref, x2_ref, wv_ref, w1_ref, w2_ref, bv_ref, b1_ref, b2_ref, fb_ref, h0_ref, h1_ref,
                      ff_ref, if_ref, o_ref):
    L = v_ref.shape[0]
    N = ff_ref.shape[0] // 2
    conv = lambda u_ref, w_ref, b_ref: _short_conv_chunk(u_ref, 0, L, 0.0, 0.0, w_ref[...], b_ref[...])
    z = conv(v_ref, wv_ref, bv_ref)
    gates = (conv(x1_ref, w1_ref, b1_ref), conv(x2_ref, w2_ref, b2_ref))
    for o, (gate, h_ref) in enumerate(zip(gates, (h0_ref, h1_ref))):
        X = _dot(ff_ref[...], z.astype(BF16))
        xr, xi = X[:N], X[N:]
        hr, hi = h_ref[:N, :], h_ref[N:, :]
        Y = jnp.concatenate([xr * hr - xi * hi, xr * hi + xi * hr], axis=0)
        y = _dot(if_ref[...], Y.astype(BF16))
        z = gate * (y + fb_ref[o:o + 1, :] * z)
    o_ref[...] = z


def _hyena_short(hy, lp, k_un, scale):
    B, L, _ = hy.shape
    C = D_HYENA
    N = 2 * L
    ct = 256
    ncb = C // ct
    k4 = k_un.reshape(L, HYENA_ORDER, 2, C)
    taps = jnp.concatenate([k4[:, :, 0], jnp.zeros((1, HYENA_ORDER, C), F32), k4[:0:-1, :, 1]], axis=0)
    KF = (N // 2 + 1 + SUBLANES - 1) // SUBLANES * SUBLANES
    k = np.arange(KF)
    kept = (k <= N // 2)[:, None]
    weight = np.where((k == 0) | (k == N // 2), 1.0, 2.0)[:, None] * kept / N
    ang = 2 * np.pi * np.outer(k, np.arange(N)) / N
    dft = jnp.asarray(np.concatenate([np.cos(ang) * kept, -np.sin(ang) * kept], axis=0), F32)
    H = _dft_spectrum(dft, taps.reshape(N, HYENA_ORDER * C), scale)
    fwd = jnp.asarray(np.concatenate([np.cos(ang[:, :L]) * kept, -np.sin(ang[:, :L]) * kept], axis=0), BF16)
    inv = jnp.asarray(np.concatenate([np.cos(ang[:, :L]) * weight, -np.sin(ang[:, :L]) * weight], axis=0).T, BF16)
    u = lambda g: pl.BlockSpec((None, L, ct), lambda c, b: (b, 0, g * ncb + c))
    cw = lambda g: pl.BlockSpec((3, ct), lambda c, b: (0, g * ncb + c))
    cb = lambda g: pl.BlockSpec((1, ct), lambda c, b: (0, g * ncb + c))
    const = lambda a: pl.BlockSpec(a.shape, lambda c, b: (0,) * a.ndim)
    return pl.pallas_call(
        _hyena_short_body,
        grid=(ncb, B),
        in_specs=[u(0), u(1), u(2), cw(0), cw(1), cw(2), cb(0), cb(1), cb(2),
                  pl.BlockSpec((HYENA_ORDER, ct), lambda c, b: (0, c)),
                  pl.BlockSpec((2 * KF, ct), lambda c, b: (0, c)),
                  pl.BlockSpec((2 * KF, ct), lambda c, b: (0, ncb + c)),
                  const(fwd), const(inv)],
        out_specs=pl.BlockSpec((None, L, ct), lambda c, b: (b, 0, c)),
        out_shape=jax.ShapeDtypeStruct((B, L, C), F32),
        name="hyena_short",
        compiler_params=_params("arbitrary", "arbitrary"),
    )(hy, hy, hy, lp['conv_w'], lp['conv_w'], lp['conv_w'], lp['conv_b'], lp['conv_b'], lp['conv_b'],
      lp['f_bias'], H, H, fwd, inv)


FFT_N1 = 128
FFT_BATCH = 4
FFT_GROUPS = 2


def _fft_dims(L):
    N = 2 * L
    N1 = FFT_N1
    N2 = N // N1
    step = FFT_BATCH * FFT_GROUPS
    K1 = (N1 // 2 + 1 + step - 1) // step * step
    assert N1 * N2 == N and N2 % SUBLANES == 0 and K1 % SUBLANES == 0
    return N, N1, N2, N2 + SUBLANES, 2 * K1 + SUBLANES, K1


def _fft_tables(L):
    N, N1, N2, _, _, K1 = _fft_dims(L)
    NH = N1 // 2
    n1 = np.arange(NH)
    k1 = np.arange(K1)
    n2 = np.arange(N2)
    kept = (k1 <= N1 // 2)[None, :, None]
    weight = np.where((k1 == 0) | (k1 == N1 // 2), 1.0, 2.0)[None, :, None] * kept / N
    th = 2 * np.pi * (n1[None, None, :] * k1[None, :, None] / N1 + n2[:, None, None] * k1[None, :, None] / N)
    g = np.concatenate([np.cos(th) * kept, -np.sin(th) * kept], axis=1).reshape(N2 * 2 * K1, NH)
    ig = np.concatenate([np.cos(th) * weight, -np.sin(th) * weight], axis=1).transpose(0, 2, 1)
    ig = ig.reshape(N2 * NH, 2 * K1)
    ph = 2 * np.pi * np.outer(n2, n2) / N2
    c, s = np.cos(ph), np.sin(ph)
    f2 = np.block([[c, s], [-s, c]])
    if2 = np.block([[c, -s], [s, c]])

    return {name: jnp.asarray(a, BF16) for name, a in (('g', g), ('ig', ig), ('f2', f2), ('if2', if2))}


def _mm(tab_ref, r0, nrows, x):
    return _dot(tab_ref[pl.ds(r0, nrows), :], x.astype(BF16))


def _fft_stage1(tbuf, sbuf, g, dims):
    _, N1, N2, P, Q, K1 = dims

    nb = FFT_BATCH * FFT_GROUPS

    def body(i, carry):
        n2s = [i * nb + j for j in range(nb)]
        xs = [tbuf[pl.ds(n2, N1 // 2, stride=P), :] for n2 in n2s]
        outs = [_mm(g, pl.multiple_of(n2 * 2 * K1, 2 * SUBLANES), 2 * K1, x) for n2, x in zip(n2s, xs)]
        for n2, out in zip(n2s, outs):
            sbuf[pl.ds(pl.multiple_of(n2 * Q, SUBLANES), 2 * K1), :] = out
        return carry

    lax.fori_loop(0, N2 // nb, body, 0)


def _fft_stage2_load(sbuf, k1, dims):
    _, N1, N2, _, Q, K1 = dims
    re = [sbuf[pl.ds(k1 + j, N2, stride=Q), :] for j in range(FFT_BATCH)]
    im = [sbuf[pl.ds(K1 + k1 + j, N2, stride=Q), :] for j in range(FFT_BATCH)]
    return jnp.concatenate([jnp.concatenate(re, axis=1), jnp.concatenate(im, axis=1)], axis=0)


def _hyena_long_body(v_ref, x1_ref, x2_ref, wv_ref, w1_ref, w2_ref, bv_ref, b1_ref, b2_ref, fb_ref, h0_ref, h1_ref,
                     g_ref, ig_ref, f2_ref, if2_ref, o_ref, zbuf, g1buf, g2buf, sbuf, *, dims):
    N, N1, N2, P, Q, K1 = dims
    NH = N1 // 2
    zero = jnp.zeros((1, LANES), F32)

    for u_ref, w_ref, b_ref, buf in ((v_ref, wv_ref, bv_ref, zbuf), (x1_ref, w1_ref, b1_ref, g1buf),
                                     (x2_ref, w2_ref, b2_ref, g2buf)):
        w, b = w_ref[...], b_ref[...]
        for n1 in range(NH):
            r0 = n1 * N2
            prev_last = zero if n1 == 0 else u_ref[r0 - 1:r0, :]
            next_first = zero if n1 == NH - 1 else u_ref[r0 + N2:r0 + N2 + 1, :]
            buf[n1 * P:n1 * P + N2, :] = _short_conv_chunk(u_ref, r0, N2, prev_last, next_first, w, b)

    for o, (gbuf, h_ref) in enumerate(((g1buf, h0_ref), (g2buf, h1_ref))):
        _fft_stage1(zbuf, sbuf, g_ref, dims)

        def per_k1(i, carry):
            k1s = [(i * FFT_GROUPS + g) * FFT_BATCH for g in range(FFT_GROUPS)]
            Bs = [_fft_stage2_load(sbuf, k1, dims).astype(BF16) for k1 in k1s]
            Ccs = []
            for k1, B in zip(k1s, Bs):
                X = _dot(f2_ref[...], B)
                h0 = [pl.multiple_of((k1 + j) * 2 * N2, 2 * N2) for j in range(FFT_BATCH)]
                hr = jnp.concatenate([h_ref[pl.ds(r, N2), :] for r in h0], axis=1)
                hi = jnp.concatenate([h_ref[pl.ds(r + N2, N2), :] for r in h0], axis=1)
                xr, xi = X[:N2], X[N2:]
                Y = jnp.concatenate([xr * hr - xi * hi, xr * hi + xi * hr], axis=0)
                Ccs.append(_dot(if2_ref[...], Y.astype(BF16)))
            for k1, Cc in zip(k1s, Ccs):
                for j in range(FFT_BATCH):
                    lanes = slice(j * LANES, (j + 1) * LANES)
                    sbuf[pl.ds(k1 + j, N2, stride=Q), :] = Cc[:N2, lanes]
                    sbuf[pl.ds(K1 + k1 + j, N2, stride=Q), :] = Cc[N2:, lanes]
            return carry

        lax.fori_loop(0, K1 // (FFT_BATCH * FFT_GROUPS), per_k1, 0)
        fb = fb_ref[o:o + 1, :]

        nb = FFT_BATCH * FFT_GROUPS

        def per_n2(i, carry):
            n2s = [i * nb + j for j in range(nb)]
            Ds = [sbuf[pl.ds(pl.multiple_of(n2 * Q, SUBLANES), 2 * K1), :].astype(BF16) for n2 in n2s]
            ys = [_dot(ig_ref[pl.ds(pl.multiple_of(n2 * NH, NH), NH), :], D) for n2, D in zip(n2s, Ds)]
            for n2, y in zip(n2s, ys):
                rows = pl.ds(n2, NH, stride=P)
                zbuf[rows, :] = gbuf[rows, :] * (y + fb * zbuf[rows, :])
            return carry

        lax.fori_loop(0, N2 // nb, per_n2, 0)

    for n1 in range(NH):
        o_ref[n1 * N2:(n1 + 1) * N2, :] = zbuf[n1 * P:n1 * P + N2, :]


def _fft_spectrum_body(hf_ref, hb_ref, s_ref, g_ref, f2_ref, o_ref, tbuf, sbuf, *, dims):
    N, N1, N2, P, Q, K1 = dims
    NH = N1 // 2
    scale = s_ref[...]
    for d, h_ref in enumerate((hf_ref, hb_ref)):
        for n1 in range(NH):
            h = h_ref[n1 * N2:(n1 + 1) * N2, :] * scale
            if d == 1 and n1 == 0:
                h = jnp.where(lax.broadcasted_iota(jnp.int32, h.shape, 0) == 0, 0.0, h)
            tbuf[n1 * P:n1 * P + N2, :] = h
        _fft_stage1(tbuf, sbuf, g_ref, dims)

        def per_k1(i, carry):
            k1 = i * FFT_BATCH
            X = _mm(f2_ref, 0, 2 * N2, _fft_stage2_load(sbuf, k1, dims))
            for j in range(FFT_BATCH):
                h0 = pl.multiple_of((k1 + j) * 2 * N2, 2 * N2)
                Xj = X[:, j * LANES:(j + 1) * LANES]
                if d == 0:
                    o_ref[pl.ds(h0, 2 * N2), :] = Xj
                else:
                    o_ref[pl.ds(h0, N2), :] += Xj[:N2]
                    o_ref[pl.ds(h0 + N2, N2), :] -= Xj[N2:]
            return carry

        lax.fori_loop(0, K1 // FFT_BATCH, per_k1, 0)


def _hyena_long(hy, lp, k_un, scale):
    B, L, _ = hy.shape
    C = D_HYENA
    dims = _fft_dims(L)
    N, N1, N2, P, Q, K1 = dims
    NH = N1 // 2
    tabs = _fft_tables(L)
    ncb = C // LANES
    nspec = HYENA_ORDER * ncb
    const1 = lambda a: pl.BlockSpec(a.shape, (lambda *i: (0,) * a.ndim), pipeline_mode=pl.Buffered(1))
    hcol = lambda d: pl.BlockSpec((L, LANES), lambda j: (0, (j // ncb) * 2 * ncb + d * ncb + j % ncb))
    H = pl.pallas_call(
        functools.partial(_fft_spectrum_body, dims=dims),
        grid=(nspec,),
        in_specs=[hcol(0), hcol(1), pl.BlockSpec((1, LANES), lambda j: (0, j)),
                  const1(tabs['g']), const1(tabs['f2'])],
        out_specs=pl.BlockSpec((K1 * 2 * N2, LANES), lambda j: (0, j)),
        out_shape=jax.ShapeDtypeStruct((K1 * 2 * N2, HYENA_ORDER * C), F32),
        scratch_shapes=[pltpu.VMEM((NH * P, LANES), F32), pltpu.VMEM((N2 * Q, LANES), F32)],
        name="filter_spectrum",
        compiler_params=_params("arbitrary"),
    )(k_un, k_un, scale, tabs['g'], tabs['f2'])

    one = pl.Buffered(1)
    hspec = lambda o: pl.BlockSpec((K1 * 2 * N2, LANES), lambda c, b: (0, o * ncb + c), pipeline_mode=one)
    u = lambda g: pl.BlockSpec((None, L, LANES), lambda c, b: (b, 0, g * ncb + c), pipeline_mode=one)
    cw = lambda g: pl.BlockSpec((3, LANES), lambda c, b: (0, g * ncb + c))
    cb = lambda g: pl.BlockSpec((1, LANES), lambda c, b: (0, g * ncb + c))
    return pl.pallas_call(
        functools.partial(_hyena_long_body, dims=dims),
        grid=(ncb, B),
        in_specs=[u(0), u(1), u(2), cw(0), cw(1), cw(2), cb(0), cb(1), cb(2),
                  pl.BlockSpec((HYENA_ORDER, LANES), lambda c, b: (0, c)),
                  hspec(0), hspec(1),
                  const1(tabs['g']), const1(tabs['ig']), const1(tabs['f2']), const1(tabs['if2'])],
        out_specs=pl.BlockSpec((None, L, LANES), lambda c, b: (b, 0, c)),
        out_shape=jax.ShapeDtypeStruct((B, L, C), F32),
        scratch_shapes=[pltpu.VMEM((NH * P, LANES), F32), pltpu.VMEM((NH * P, LANES), F32),
                        pltpu.VMEM((NH * P, LANES), F32), pltpu.VMEM((N2 * Q, LANES), F32)],
        name="hyena_long",
        compiler_params=_params("arbitrary", "arbitrary"),
    )(hy, hy, hy, lp['conv_w'], lp['conv_w'], lp['conv_w'], lp['conv_b'], lp['conv_b'], lp['conv_b'],
      lp['f_bias'], H, H, tabs['g'], tabs['ig'], tabs['f2'], tabs['if2'])


def _hyena(hy, lp):
    L = hy.shape[1]
    k_un, scale = _hyena_filters(L, lp)
    if 2 * L >= 2 * FFT_N1 * SUBLANES and (2 * L) % (FFT_N1 * SUBLANES) == 0:
        return _hyena_long(hy, lp, k_un, scale)
    return _hyena_short(hy, lp, k_un, scale)


def kernel(x_prompt, x_sample, cache_k, cache_v, c, c_ctx, w_mod, b_mod, g_mix, w_in, conv_w, conv_b, f_w1, f_b1,
           f_freq, f_w2, f_b2, f_w3, f_bias, rpb, g_out_hy, g_out_at, w_out, g_ffn, w_router, b_router, w_exp1,
           b_exp1, w_exp2, b_exp2, g_final):
    depth = w_mod.shape[0]
    Bp, Lp, D = x_prompt.shape
    Bs, Ls, _ = x_sample.shape
    n_ctx, n_lat = Bp * Lp, Bs * Ls
    n_tok = n_ctx + n_lat
    d_attn = N_HEADS * HEAD_DIM
    q_off = 3 * D_HYENA
    row2 = lambda a: a.reshape(1, -1)

    cond = jnp.zeros((SUBLANES, D), F32).at[0].set(c_ctx).at[1:1 + Bs].set(c)
    xp, xs = x_prompt, x_sample
    new_k, new_v = [], []
    for l in range(depth):
        wr = jnp.zeros((D, LANES), F32).at[:, :N_EXPERTS].set(w_router[l])
        wr_hi = wr.astype(BF16)
        lp = {
            'conv_w': conv_w[l], 'conv_b': row2(conv_b[l]), 'f_w1': f_w1[l], 'f_b1': row2(f_b1[l]),
            'f_freq': row2(f_freq[l]), 'f_w2': f_w2[l], 'f_b2': row2(f_b2[l]), 'f_w3': f_w3[l],
            'f_bias': f_bias[l], 'g_out_hy': row2(g_out_hy[l]), 'g_out_at': row2(g_out_at[l]),
            'w_out': w_out[l].astype(BF16), 'g_ffn': row2(g_ffn[l]),
            'wr_hi': wr_hi, 'wr_lo': (wr - wr_hi.astype(F32)).astype(BF16),
            'b_router': jnp.zeros((1, LANES), F32).at[0, :N_EXPERTS].set(b_router[l]),
            'w_exp1': w_exp1[l], 'b_exp1': b_exp1[l][:, None, :], 'w_exp2': w_exp2[l],
            'b_exp2': b_exp2[l][:, None, :],
        }
        mod = _modulation(cond, w_mod[l], row2(b_mod[l]))
        mod3 = mod.reshape(SUBLANES, 1, 6 * D)
        w_in_b = w_in[l].astype(BF16)
        g_mix_l = row2(g_mix[l])

        proj_p = _in_projection(xp, mod3, 0, g_mix_l, w_in_b)
        proj_s = _in_projection(xs, mod3, 1, g_mix_l, w_in_b)
        kv = proj_p[..., q_off + d_attn:].reshape(Bp, Lp, 2, N_HEADS, HEAD_DIM).transpose(2, 0, 3, 1, 4)
        new_k.append(kv[0])
        new_v.append(kv[1])

        hy_p = _hyena(proj_p, lp)
        hy_s = _hyena(proj_s, lp)
        at_p = _context_attention(proj_p, q_off)
        heads_last = lambda t: t.transpose(0, 2, 1, 3).reshape(Bs, t.shape[2], d_attn)
        at_s = _neighbourhood_attention(proj_s, q_off, heads_last(cache_k[:, l]), heads_last(cache_v[:, l]), rpb[l])

        x1, h2u, logits = _out_projection((hy_p, at_p, xp), (hy_s, at_s, xs), mod3, lp)

        gates, ce, nv, n_used, gidx, sidx = _routing(logits[:, :N_EXPERTS], n_tok)
        yp = _experts(ce, nv, n_used, gidx, sidx, h2u, lp, n_tok * TOP_K + MOE_CHUNK)
        last = l == depth - 1
        gf = row2(g_final) if last else None
        assert last, "deeper stacks need the un-normalised residual between layers"
        xp = _combine(x1, yp, gates, mod3, 0, gf, 0, Bp, Lp)
        xs = _combine(x1, yp, gates, mod3, 1, gf, n_ctx, Bs, Ls)

    return xp, xs, jnp.stack(new_k, axis=1), jnp.stack(new_v, axis=1)
```

```python
import functools
import math

import numpy as np
import jax
import jax.numpy as jnp
from jax import lax
from jax.experimental import pallas as pl
from jax.experimental.pallas import tpu as pltpu

F32 = jnp.float32
BF16 = jnp.bfloat16
HIGHEST = lax.Precision.HIGHEST

GRID_W = 64
N_HEADS = 16
HEAD_DIM = 64
D_HYENA = 1024
HYENA_ORDER = 2
FILTER_BANDS = 16
DECAY_TARGET = 1e-2
DECAY_PCT_SHORT = 0.3
DECAY_PCT_LONG = 1.5
WIN_H = 8
WIN_W = 16
N_EXPERTS = 32
TOP_K = 4
SWIGLU_ALPHA = 1.702
SWIGLU_LIMIT = 7.0
EPS = 1e-6
NEG_INF = -1e30

LANES = 128
SUBLANES = 8
VMEM_LIMIT = 60 * 1024 * 1024

ROW_TILE = 256
NA_Q_ROWS = 8
MOE_CHUNK = 1024
MOE_SUB = 256
MOE_FF_TILE = 512
MOE_ISSUE = 64


def _params(*sem):
    return pltpu.CompilerParams(dimension_semantics=sem, vmem_limit_bytes=VMEM_LIMIT)


def _rms(x, g):
    return x * lax.rsqrt(jnp.mean(x * x, axis=-1, keepdims=True) + EPS) * g


def _split_bf16(x):
    hi = x.astype(BF16)
    lo = (x - hi.astype(F32)).astype(BF16)
    return hi, lo


def _pack_bf16_pair(lo, hi):
    bl = lax.bitcast_convert_type(lo.astype(BF16).astype(F32), jnp.uint32)
    bh = lax.bitcast_convert_type(hi.astype(BF16).astype(F32), jnp.uint32)
    return (bl >> 16) | (bh & jnp.uint32(0xFFFF0000))


def _unpack_bf16_pair(words):
    lo = lax.bitcast_convert_type(words << 16, F32)
    hi = lax.bitcast_convert_type(words & jnp.uint32(0xFFFF0000), F32)
    return lo, hi


def _dot(a, b):
    return jnp.dot(a, b, preferred_element_type=F32)


def _dot_nt(a, b):
    return lax.dot_general(a, b, (((1,), (1,)), ((), ())), preferred_element_type=F32)


def _mod_body(c_ref, w_ref, b_ref, o_ref):
    c = c_ref[...]
    s = c / (1.0 + jnp.exp(-c))
    o_ref[...] = jnp.dot(s, w_ref[...], precision=HIGHEST, preferred_element_type=F32) + b_ref[...]


def _modulation(cc, w, b):
    D, N = w.shape
    tn = min(N, 1536)
    return pl.pallas_call(
        _mod_body,
        grid=(N // tn,),
        in_specs=[pl.BlockSpec((SUBLANES, D), lambda j: (0, 0)),
                  pl.BlockSpec((D, tn), lambda j: (0, j)),
                  pl.BlockSpec((1, tn), lambda j: (0, j))],
        out_specs=pl.BlockSpec((SUBLANES, tn), lambda j: (0, j)),
        out_shape=jax.ShapeDtypeStruct((SUBLANES, N), F32),
        name="modulation",
        compiler_params=_params("arbitrary"),
    )(cc, w, b)


def _inproj_body(x_ref, shift_ref, scale_ref, g_ref, w_ref, o_ref, *, n_chunk):
    h = _rms(x_ref[...], g_ref[...]) * (1.0 + scale_ref[...]) + shift_ref[...]
    hb = h.astype(BF16)
    n_out = o_ref.shape[-1]

    def col(j, carry):
        c0 = pl.multiple_of(j * n_chunk, n_chunk)
        o_ref[:, pl.ds(c0, n_chunk)] = _dot(hb, w_ref[:, pl.ds(c0, n_chunk)])
        return carry

    lax.fori_loop(0, n_out // n_chunk, col, 0)


def _in_projection(x, mod3, row0, g, w_bf16):
    B, L, D = x.shape
    N = w_bf16.shape[1]
    tm = min(ROW_TILE, L)
    mrow = (lambda b: 0) if row0 == 0 else (lambda b: b + row0)
    return pl.pallas_call(
        functools.partial(_inproj_body, n_chunk=512),
        grid=(B, L // tm),
        in_specs=[pl.BlockSpec((None, tm, D), lambda b, i: (b, i, 0)),
                  pl.BlockSpec((None, 1, D), lambda b, i: (mrow(b), 0, 0)),
                  pl.BlockSpec((None, 1, D), lambda b, i: (mrow(b), 0, 1)),
                  pl.BlockSpec((1, D), lambda b, i: (0, 0)),
                  pl.BlockSpec((D, N), lambda b, i: (0, 0), pipeline_mode=pl.Buffered(1))],
        out_specs=pl.BlockSpec((None, tm, N), lambda b, i: (b, i, 0)),
        out_shape=jax.ShapeDtypeStruct((B, L, N), F32),
        name="in_projection",
        compiler_params=_params("arbitrary", "arbitrary"),
    )(x, mod3, mod3, g, w_bf16)


def _head_lane_masks(rows, hd):
    lane = lax.broadcasted_iota(jnp.int32, (rows, LANES), 1)
    return [(lane >= h * hd) & (lane < (h + 1) * hd) for h in range(LANES // hd)]


def _ctx_attn_body(q_ref, k_ref, v_ref, o_ref, *, scale, hd):
    q = q_ref[...]
    k = k_ref[...].astype(BF16)
    v = v_ref[...].astype(BF16)
    masks = _head_lane_masks(q.shape[0], hd)
    out = None
    for msk in masks:
        qh = jnp.where(msk, q, 0.0).astype(BF16)
        s = _dot_nt(qh, k) * scale
        p = jnp.exp(s - jnp.max(s, axis=-1, keepdims=True))
        o = _dot(p.astype(BF16), v) / jnp.sum(p, axis=-1, keepdims=True)
        out = o if out is None else jnp.where(msk, o, out)
    o_ref[...] = out


def _context_attention(proj, q_off):
    B, L, _ = proj.shape
    d_attn = N_HEADS * HEAD_DIM
    nhp = d_attn // LANES
    qb, kb, vb = (q_off // LANES, (q_off + d_attn) // LANES, (q_off + 2 * d_attn) // LANES)
    spec = lambda base: pl.BlockSpec((None, L, LANES), lambda b, h: (b, 0, base + h))
    return pl.pallas_call(
        functools.partial(_ctx_attn_body, scale=1.0 / math.sqrt(HEAD_DIM), hd=HEAD_DIM),
        grid=(B, nhp),
        in_specs=[spec(qb), spec(kb), spec(vb)],
        out_specs=pl.BlockSpec((None, L, LANES), lambda b, h: (b, 0, h)),
        out_shape=jax.ShapeDtypeStruct((B, L, d_attn), F32),
        name="context_attention",
        compiler_params=_params("arbitrary", "arbitrary"),
    )(proj, proj, proj)


def _na_tables(rows, rpb):
    W = GRID_W
    kh = min(WIN_H, rows)
    rbq = NA_Q_ROWS
    kwr = rbq + kh
    assert rows % rbq == 0 and kwr <= rows
    nblk = rows // rbq
    ws = np.clip(np.arange(nblk) * rbq - kh // 2, 0, rows - kwr)
    rq = np.arange(nblk)[:, None, None] * rbq + np.arange(rbq)[None, :, None]
    rk = ws[:, None, None] + np.arange(kwr)[None, None, :]
    rs = np.clip(rq - kh // 2, 0, rows - kh)
    vr = (rk >= rs) & (rk < rs + kh)
    dr = np.where(vr, rk - rq + WIN_H - 1, 0)
    patterns = np.concatenate([dr.reshape(nblk, -1), vr.reshape(nblk, -1)], axis=1)
    _, first, btype = np.unique(patterns, axis=0, return_index=True, return_inverse=True)
    dr_t, vr_t = dr[first], vr[first]
    cq = np.arange(W)[:, None]
    ck = np.arange(W)[None, :]
    cs = np.clip(cq - WIN_W // 2, 0, W - WIN_W)
    vc = (ck >= cs) & (ck < cs + WIN_W)
    dc = np.clip(ck - cq, -(WIN_W - 1), WIN_W - 1) + WIN_W - 1
    nt = dr_t.shape[0]
    nr, ncol = 2 * WIN_H - 1, 2 * WIN_W - 1
    rowsel = np.where(vr_t, dr_t, nr)[..., None] == np.arange(nr + 1)
    colsel = np.where(vc, dc, ncol)[..., None] == np.arange(ncol + 1)
    rpb_ext = jnp.full((rpb.shape[0], nr + 1, ncol + 1), NEG_INF, F32).at[:, :nr, :ncol].set(rpb.astype(F32))
    bias = jnp.einsum('tajr,hrc,qkc->htaqjk', rowsel.astype(np.float32), rpb_ext,
                      colsel.astype(np.float32), precision=HIGHEST)
    bias = bias.reshape(rpb.shape[0], nt, rbq * W, kwr * W)
    return ws.astype(np.int32), btype.reshape(-1).astype(np.int32), bias


def _na_body(ws_ref, bt_ref, q_ref, k_ref, v_ref, kc_ref, vc_ref, bias_ref, o_ref, *, scale, hd, nk):
    i = pl.program_id(2)
    start = pl.multiple_of(ws_ref[i] * GRID_W, GRID_W)
    q = q_ref[...] * scale
    kl = k_ref[pl.ds(start, nk), :].astype(BF16)
    vl = v_ref[pl.ds(start, nk), :]
    kc = kc_ref[...].astype(BF16)
    vc = vc_ref[...]
    masks = _head_lane_masks(q.shape[0], hd)
    vl_masks = _head_lane_masks(nk, hd)
    vc_masks = _head_lane_masks(vc.shape[0], hd)
    assert len(masks) == 2, "the value lanes of the other head carry the softmax denominator"
    out = None
    for h, msk in enumerate(masks):
        qh = jnp.where(msk, q, 0.0).astype(BF16)
        sl = _dot_nt(qh, kl) + bias_ref[h]
        sc = _dot_nt(qh, kc)
        m = jnp.maximum(jnp.max(sl, axis=-1, keepdims=True), jnp.max(sc, axis=-1, keepdims=True))
        p_l = jnp.exp(sl - m).astype(BF16)
        p_c = jnp.exp(sc - m).astype(BF16)
        o = (_dot(p_l, jnp.where(vl_masks[h], vl, 1.0).astype(BF16))
             + _dot(p_c, jnp.where(vc_masks[h], vc, 1.0).astype(BF16)))
        o = o / pltpu.roll(o, hd, axis=1)
        out = o if out is None else jnp.where(msk, o, out)
    o_ref[...] = out


def _neighbourhood_attention(proj, q_off, kc, vc, rpb):
    B, L, _ = proj.shape
    Lc = kc.shape[1]
    d_attn = N_HEADS * HEAD_DIM
    hpb = LANES // HEAD_DIM
    nhp = d_attn // LANES
    rows = L // GRID_W
    ws, btype, bias = _na_tables(rows, rpb)
    nq = NA_Q_ROWS * GRID_W
    nk = (NA_Q_ROWS + min(WIN_H, rows)) * GRID_W
    nblk = rows // NA_Q_ROWS
    qb, kb, vb = (q_off // LANES, (q_off + d_attn) // LANES, (q_off + 2 * d_attn) // LANES)
    full = lambda base: pl.BlockSpec((None, L, LANES), lambda b, h, i, ws_r, bt_r: (b, 0, base + h))
    ctx = pl.BlockSpec((None, Lc, LANES), lambda b, h, i, ws_r, bt_r: (b, 0, h))
    grid_spec = pltpu.PrefetchScalarGridSpec(
        num_scalar_prefetch=2,
        grid=(B, nhp, nblk),
        in_specs=[pl.BlockSpec((None, nq, LANES), lambda b, h, i, ws_r, bt_r: (b, i, qb + h)),
                  full(kb), full(vb), ctx, ctx,
                  pl.BlockSpec((hpb, None, nq, nk), lambda b, h, i, ws_r, bt_r: (h, bt_r[i], 0, 0))],
        out_specs=pl.BlockSpec((None, nq, LANES), lambda b, h, i, ws_r, bt_r: (b, i, h)),
    )
    return pl.pallas_call(
        functools.partial(_na_body, scale=1.0 / math.sqrt(HEAD_DIM), hd=HEAD_DIM, nk=nk),
        grid_spec=grid_spec,
        out_shape=jax.ShapeDtypeStruct((B, L, d_attn), F32),
        name="neighbourhood_attention",
        compiler_params=_params("arbitrary", "arbitrary", "arbitrary"),
    )(jnp.asarray(ws), jnp.asarray(btype), proj, proj, proj, kc, vc, bias)


def _outproj_body(yhp_ref, yap_ref, xp_ref, yhs_ref, yas_ref, xs_ref, gate1_ref, shift2_ref, scale2_ref, ghy_ref,
                  gat_ref, w_ref, gffn_ref, wrh_ref, wrl_ref, br_ref, x1_ref, h2u_ref, lg_ref, *, n_ctx_tiles):
    def run(yh_ref, ya_ref, x_ref):
        cat = jnp.concatenate([_rms(yh_ref[...], ghy_ref[...]), _rms(ya_ref[...], gat_ref[...])], axis=-1)
        mix = _dot(cat.astype(BF16), w_ref[...])
        x1 = x_ref[...] + gate1_ref[...] * mix
        x1_ref[...] = x1
        h2 = _rms(x1, gffn_ref[...]) * (1.0 + scale2_ref[...]) + shift2_ref[...]
        hi, lo = _split_bf16(h2)
        lg_ref[...] = (_dot(hi, wrh_ref[...]) + _dot(lo, wrh_ref[...]) + _dot(hi, wrl_ref[...])) + br_ref[...]
        half = h2.shape[-1] // 2
        words = _pack_bf16_pair(h2[:, :half], h2[:, half:])
        per = half // LANES
        for s in range(per):
            h2u_ref[pl.ds(s, words.shape[0], stride=per), :] = words[:, s * LANES:(s + 1) * LANES]

    is_ctx = pl.program_id(0) < n_ctx_tiles
    pl.when(is_ctx)(lambda: run(yhp_ref, yap_ref, xp_ref))
    pl.when(jnp.logical_not(is_ctx))(lambda: run(yhs_ref, yas_ref, xs_ref))


def _out_projection(ctx, lat, mod3, lp):
    Bp, Lp, D = ctx[2].shape
    Bs, Ls, _ = lat[2].shape
    dh, da = ctx[0].shape[-1], ctx[1].shape[-1]
    tm = min(ROW_TILE, Lp, Ls)
    nct, nlt = Bp * Lp // tm, Bs * Ls // tm
    n_tok = (nct + nlt) * tm
    lat_tiles = Ls // tm
    mrow = lambda i: jnp.where(i < nct, 0, 1 + (i - nct) // lat_tiles)
    modspec = lambda c: pl.BlockSpec((None, 1, D), lambda i: (mrow(i), 0, c))
    const = lambda shape: pl.BlockSpec(shape, lambda i: (0,) * len(shape))
    crow = lambda w: pl.BlockSpec((tm, w), lambda i: (jnp.minimum(i, nct - 1), 0))
    lrow = lambda w: pl.BlockSpec((tm, w), lambda i: (jnp.maximum(i - nct, 0), 0))
    flat = lambda t: t.reshape(-1, t.shape[-1])
    return pl.pallas_call(
        functools.partial(_outproj_body, n_ctx_tiles=nct),
        grid=(nct + nlt,),
        in_specs=[crow(dh), crow(da), crow(D), lrow(dh), lrow(da), lrow(D),
                  modspec(2), modspec(3), modspec(4),
                  const((1, dh)), const((1, da)),
                  pl.BlockSpec((dh + da, D), lambda i: (0, 0), pipeline_mode=pl.Buffered(1)),
                  const((1, D)), const((D, LANES)), const((D, LANES)), const((1, LANES))],
        out_specs=[pl.BlockSpec((tm, D), lambda i: (i, 0)),
                   pl.BlockSpec((tm * (D // 2 // LANES), LANES), lambda i: (i, 0)),
                   pl.BlockSpec((tm, LANES), lambda i: (i, 0))],
        out_shape=[jax.ShapeDtypeStruct((n_tok, D), F32),
                   jax.ShapeDtypeStruct((n_tok * (D // 2 // LANES), LANES), jnp.uint32),
                   jax.ShapeDtypeStruct((n_tok, LANES), F32)],
        name="out_projection",
        compiler_params=_params("arbitrary"),
    )(*[flat(t) for t in ctx], *[flat(t) for t in lat], mod3, mod3, mod3, lp['g_out_hy'], lp['g_out_at'],
      lp['w_out'], lp['g_ffn'], lp['wr_hi'], lp['wr_lo'], lp['b_router'])


def _expert_body(ce_ref, nv_ref, nu_ref, gcur_ref, gnext_ref, sprev_ref, scur_ref, h2u_hbm, w1g_ref, w1u_ref,
                 b1g_ref, b1u_ref, w2_ref, b2_ref, yp_hbm, xu, xb, acc, ystage, abuf, gsem, ssem):
    c = pl.program_id(0)
    f = pl.program_id(1)
    nc = pl.num_programs(0)
    nf = pl.num_programs(1)
    half = xb.shape[1] // 2
    xt = half // LANES
    yt = xt
    slot = c % 2
    subs = lambda n: (n + MOE_SUB - 1) // MOE_SUB
    nv = nv_ref[c]
    nsub = subs(nv)
    nrows = nsub * MOE_SUB
    last = c == nu_ref[0] - 1
    nv_prev = nv_ref[jnp.maximum(c - 1, 0)]
    nrows_prev = subs(nv_prev) * MOE_SUB
    covered_prev = MOE_ISSUE * nf * subs(nv_prev)
    nrows_next = jnp.where(c + 1 < nc, subs(nv_ref[jnp.minimum(c + 1, nc - 1)]), 0) * MOE_SUB
    covered = MOE_ISSUE * nf * nsub

    def gather_row(idx_ref, r, to_slot):
        tok = idx_ref[0, r]
        pltpu.make_async_copy(h2u_hbm.at[pl.ds(pl.multiple_of(tok * xt, xt), xt), :],
                              xu.at[to_slot, pl.ds(pl.multiple_of(r * xt, xt), xt), :], gsem.at[to_slot]).start()

    def scatter_row(dst, r):
        pltpu.make_async_copy(ystage.at[pl.ds(pl.multiple_of(r * yt, yt), yt), :],
                              yp_hbm.at[pl.ds(pl.multiple_of(dst * yt, yt), yt), :], ssem).start()

    def for_rows(lo, hi, fn):
        lax.fori_loop(lo, hi, lambda r, carry: (fn(r), carry)[1], 0)

    def drain(n, src, dst, sem, per):
        piece = lambda m: pltpu.make_async_copy(src.at[pl.ds(0, m * per), :], dst.at[pl.ds(0, m * per), :], sem)
        for_rows(0, n // MOE_SUB, lambda i: piece(MOE_SUB).wait())
        bit = MOE_SUB // 2
        while bit:
            if not isinstance(n, int):
                pl.when((n & bit) != 0)(piece(bit).wait)
            elif n & bit:
                piece(bit).wait()
            bit //= 2

    @pl.when(c < nu_ref[0])
    def _chunk():
        @pl.when(f == 0)
        def _arrive():
            @pl.when(c == 0)
            def _first():
                for_rows(0, nrows, lambda r: gather_row(gcur_ref, r, 0))

            started = jnp.where(c == 0, nrows, jnp.maximum(covered_prev, nrows))
            drain(started, h2u_hbm, xu.at[slot], gsem.at[slot], xt)

            def unpack(sb, carry):
                r0 = pl.multiple_of(sb * MOE_SUB, MOE_SUB)
                for s in range(xt):
                    lo, hi = _unpack_bf16_pair(xu[slot, pl.ds(r0 * xt + s, MOE_SUB, stride=xt), :])
                    xb[pl.ds(r0, MOE_SUB), s * LANES:(s + 1) * LANES] = lo.astype(BF16)
                    xb[pl.ds(r0, MOE_SUB), half + s * LANES:half + (s + 1) * LANES] = hi.astype(BF16)
                acc[pl.ds(r0, MOE_SUB), :] = jnp.zeros((MOE_SUB, acc.shape[1]), F32)
                return carry

            lax.fori_loop(0, nsub, unpack, 0)

        def matmul_steps(with_scatter):
            def up_proj(sb):
                base = (f * nsub + sb) * MOE_ISSUE
                for j in range(MOE_ISSUE):
                    gather_row(gnext_ref, base + j, 1 - slot)
                if with_scatter:
                    for j in range(MOE_ISSUE):
                        scatter_row(sprev_ref[0, base + j], jnp.minimum(base + j, nrows_prev - 1))
                x = xb[pl.ds(pl.multiple_of(sb * MOE_SUB, MOE_SUB), MOE_SUB), :]
                gate = jnp.minimum(_dot(x, w1g_ref[...].astype(BF16)) + b1g_ref[...], SWIGLU_LIMIT)
                up = jnp.clip(_dot(x, w1u_ref[...].astype(BF16)) + b1u_ref[...], -SWIGLU_LIMIT, SWIGLU_LIMIT)
                glu = gate / (1.0 + jnp.exp(-SWIGLU_ALPHA * gate))
                abuf[...] = ((up + 1.0) * glu).astype(BF16)

            def down_proj(sb):
                acc[pl.ds(pl.multiple_of(sb * MOE_SUB, MOE_SUB), MOE_SUB), :] += _dot(abuf[...], w2_ref[...].astype(BF16))

            def step(sb, carry):
                down_proj(sb - 1)
                up_proj(sb)
                return carry

            up_proj(0)
            lax.fori_loop(1, nsub, step, 0)
            down_proj(nsub - 1)

        pl.when(c == 0)(lambda: matmul_steps(False))
        pl.when(c > 0)(lambda: matmul_steps(True))

        @pl.when(f == nf - 1)
        def _leave():
            for_rows(covered, nrows_next, lambda r: gather_row(gnext_ref, r, 1 - slot))

            @pl.when(c > 0)
            def _prev_out():
                for_rows(covered, nv_prev, lambda r: scatter_row(sprev_ref[0, r], r))
                drain(jnp.maximum(covered, nv_prev), yp_hbm, ystage, ssem, yt)

            def stage(sb, carry):
                r0 = pl.multiple_of(sb * MOE_SUB, MOE_SUB)
                for s in range(yt):
                    lo, hi = slice(s * LANES, (s + 1) * LANES), slice(half + s * LANES, half + (s + 1) * LANES)
                    ystage[pl.ds(r0 * yt + s, MOE_SUB, stride=yt), :] = _pack_bf16_pair(
                        acc[pl.ds(r0, MOE_SUB), lo] + b2_ref[:, lo], acc[pl.ds(r0, MOE_SUB), hi] + b2_ref[:, hi])
                return carry

            lax.fori_loop(0, nsub, stage, 0)

            @pl.when(last)
            def _flush():
                for_rows(0, nv, lambda r: scatter_row(scur_ref[0, r], r))
                spare0 = yp_hbm.shape[0] // yt - xb.shape[0]
                for_rows(0, xb.shape[0], lambda r: scatter_row(spare0 + r, jnp.minimum(r, nrows - 1)))
                drain(nv, yp_hbm, ystage, ssem, yt)
                drain(xb.shape[0], yp_hbm, ystage, ssem, yt)
                drain(jnp.maximum(covered, nrows_next), h2u_hbm, xu.at[1 - slot], gsem.at[1 - slot], xt)


def _experts(chunk_e, chunk_nv, n_used, gidx, sidx, h2u, lp, n_out_rows):
    E, D, two_ff = lp['w_exp1'].shape
    d_ff = two_ff // 2
    nc = gidx.shape[0]
    R, tf = MOE_CHUNK, MOE_FF_TILE
    nf = d_ff // tf
    assert MOE_ISSUE * nf * (R // MOE_SUB) <= R, "a chunk's matmul steps must not start more rows than a chunk holds"

    def ff(c, f, nu):
        return jnp.where(c < nu[0], f, nf - 1)

    smem = lambda step: pl.BlockSpec((None, 1, R), lambda c, f, ce, nv, nu: (jnp.clip(c + step, 0, nc - 1), 0, 0),
                                     memory_space=pltpu.SMEM)
    grid_spec = pltpu.PrefetchScalarGridSpec(
        num_scalar_prefetch=3,
        grid=(nc, nf),
        in_specs=[smem(0), smem(1), smem(-1), smem(0),
                  pl.BlockSpec(memory_space=pl.ANY),
                  pl.BlockSpec((None, D, tf), lambda c, f, ce, nv, nu: (ce[c], 0, ff(c, f, nu))),
                  pl.BlockSpec((None, D, tf), lambda c, f, ce, nv, nu: (ce[c], 0, nf + ff(c, f, nu))),
                  pl.BlockSpec((None, 1, tf), lambda c, f, ce, nv, nu: (ce[c], 0, ff(c, f, nu))),
                  pl.BlockSpec((None, 1, tf), lambda c, f, ce, nv, nu: (ce[c], 0, nf + ff(c, f, nu))),
                  pl.BlockSpec((None, tf, D), lambda c, f, ce, nv, nu: (ce[c], ff(c, f, nu), 0)),
                  pl.BlockSpec((None, 1, D), lambda c, f, ce, nv, nu: (ce[c], 0, 0))],
        out_specs=pl.BlockSpec(memory_space=pl.ANY),
        scratch_shapes=[pltpu.VMEM((2, R * (D // 2 // LANES), LANES), jnp.uint32),
                        pltpu.VMEM((R, D), BF16),
                        pltpu.VMEM((R, D), F32),
                        pltpu.VMEM((R * (D // 2 // LANES), LANES), jnp.uint32),
                        pltpu.VMEM((MOE_SUB, tf), BF16),
                        pltpu.SemaphoreType.DMA((2,)),
                        pltpu.SemaphoreType.DMA(())],
    )
    return pl.pallas_call(
        _expert_body,
        grid_spec=grid_spec,
        out_shape=jax.ShapeDtypeStruct((n_out_rows * (D // 2 // LANES), LANES), jnp.uint32),
        name="experts",
        compiler_params=pltpu.CompilerParams(dimension_semantics=("arbitrary", "arbitrary"),
                                             vmem_limit_bytes=VMEM_LIMIT, has_side_effects=True,
                                             disable_bounds_checks=True),
    )(chunk_e, chunk_nv, n_used, gidx, gidx, sidx, sidx, h2u, lp['w_exp1'], lp['w_exp1'], lp['b_exp1'],
      lp['b_exp1'], lp['w_exp2'], lp['b_exp2'])


def _routing(logits, n_tok):
    R = MOE_CHUNK
    top_v, top_i = lax.top_k(logits, TOP_K)
    gates = jax.nn.softmax(top_v, axis=-1)
    n_pairs = n_tok * TOP_K
    flat_e = top_i.reshape(n_pairs).astype(jnp.int32)
    experts = jnp.arange(N_EXPERTS, dtype=jnp.int32)
    counts = jnp.sum((flat_e[:, None] == experts[None]).astype(jnp.int32), axis=0)
    padded = (counts + R - 1) // R * R
    pad_end = jnp.cumsum(padded)
    pad_start = pad_end - padded
    nc = n_pairs // R + N_EXPERTS
    pair = jnp.arange(n_pairs, dtype=jnp.int32)
    fill_e = jnp.repeat(experts, R)
    fill_i = jnp.tile(jnp.arange(R, dtype=jnp.int32), N_EXPERTS)
    fill_key = jnp.where(fill_i < jnp.repeat(padded - counts, R), 2 * fill_e + 1, 2 * N_EXPERTS)
    _, pid = lax.sort((jnp.concatenate([2 * flat_e, fill_key]), jnp.concatenate([pair, n_pairs + fill_i])),
                      num_keys=1)
    real = pid < n_pairs
    gidx = jnp.where(real, pid // TOP_K, 0)
    sidx = jnp.where(real, (pid % TOP_K) * n_tok + pid // TOP_K, pid)
    n_used = (pad_end[-1] // R).astype(jnp.int32)
    cstart = jnp.arange(nc, dtype=jnp.int32) * R
    ce = jnp.minimum(jnp.searchsorted(pad_end, cstart, side='right'), N_EXPERTS - 1).astype(jnp.int32)
    ce = jnp.where(jnp.arange(nc) < n_used, ce, ce[jnp.maximum(n_used - 1, 0)])
    nv = jnp.clip(counts[ce] - (cstart - pad_start[ce]), 0, R).astype(jnp.int32)
    nv = jnp.where(jnp.arange(nc) < n_used, nv, 0)
    return gates, ce, nv, n_used.reshape(1), gidx.reshape(nc, 1, R), sidx.reshape(nc, 1, R)


def _combine_body(x1_ref, y0_ref, y1_ref, y2_ref, y3_ref, g_ref, gate2_ref, gf_ref, o_ref):
    g = g_ref[...]
    tm, D = x1_ref.shape
    per = D // 2 // LANES

    def rows(y_ref):
        parts = [_unpack_bf16_pair(y_ref[pl.ds(s, tm, stride=per), :]) for s in range(per)]
        return jnp.concatenate([p[0] for p in parts] + [p[1] for p in parts], axis=-1)

    ff = (g[:, 0:1] * rows(y0_ref) + g[:, 1:2] * rows(y1_ref)) + (g[:, 2:3] * rows(y2_ref) + g[:, 3:4] * rows(y3_ref))
    x2 = x1_ref[...] + gate2_ref[...] * ff
    o_ref[...] = _rms(x2, gf_ref[...])


def _combine(x1, yp, gates, mod3, row0, g_final, tok0, B, L):
    n_tok, D = x1.shape
    tm = min(ROW_TILE, L)
    nl = L // tm
    blk0 = tok0 // tm
    nblk_tok = n_tok // tm
    mrow = (lambda b: 0) if row0 == 0 else (lambda b: b + row0)
    row = lambda b, i: (blk0 + b * nl + i, 0)
    yspec = lambda k: pl.BlockSpec((tm * (D // 2 // LANES), LANES),
                                   lambda b, i: (k * nblk_tok + blk0 + b * nl + i, 0))
    return pl.pallas_call(
        _combine_body,
        grid=(B, nl),
        in_specs=[pl.BlockSpec((tm, D), row), yspec(0), yspec(1), yspec(2), yspec(3),
                  pl.BlockSpec((tm, TOP_K), row),
                  pl.BlockSpec((None, 1, D), lambda b, i: (mrow(b), 0, 5)),
                  pl.BlockSpec((1, D), lambda b, i: (0, 0))],
        out_specs=pl.BlockSpec((None, tm, D), lambda b, i: (b, i, 0)),
        out_shape=jax.ShapeDtypeStruct((B, L, D), F32),
        name="combine",
        compiler_params=_params("arbitrary", "arbitrary"),
    )(x1, yp, yp, yp, yp, gates, mod3, g_final)


def _filter_body(tw_ref, bands_ref, w1_ref, b1_ref, fr_ref, w2_ref, b2_ref, w3_ref, dl_ref, k_ref, ss_ref):
    hi = functools.partial(jnp.dot, precision=HIGHEST, preferred_element_type=F32)
    t = tw_ref[:, 0:1]
    w = tw_ref[:, 1:2]
    lane = lax.broadcasted_iota(jnp.int32, (t.shape[0], LANES), 1)
    fw = w * bands_ref[...]
    z = jnp.where(lane == 0, t,
                  jnp.where(lane <= FILTER_BANDS, jnp.cos(fw),
                            jnp.where(lane <= 2 * FILTER_BANDS, -jnp.sin(fw), 0.0)))
    fr = fr_ref[...]
    h = jnp.sin(fr * (hi(z, w1_ref[...]) + b1_ref[...]))
    h = jnp.sin(fr * (hi(h, w2_ref[...]) + b2_ref[...]))
    decay = jnp.exp(-t * dl_ref[...])
    C = decay.shape[1]
    ss = []
    for g in range(k_ref.shape[1] // C):
        kg = hi(h, w3_ref[:, g * C:(g + 1) * C]) * decay
        k_ref[:, g * C:(g + 1) * C] = kg
        ss.append(jnp.sum(kg * kg, axis=0, keepdims=True))
    ss = jnp.concatenate(ss, axis=-1)

    @pl.when(pl.program_id(0) == 0)
    def _first():
        ss_ref[...] = ss

    @pl.when(pl.program_id(0) > 0)
    def _rest():
        ss_ref[...] += ss


def _hyena_filters(L, lp):
    C = D_HYENA
    pos = jnp.arange(L, dtype=F32)
    tw = jnp.stack([pos / max(L - 1, 1), 2 * math.pi * pos / L], axis=-1)
    bands = jnp.linspace(1e-4, FILTER_BANDS - 1, FILTER_BANDS, dtype=F32)
    bands128 = jnp.zeros((1, LANES), F32).at[0, 1:1 + 2 * FILTER_BANDS].set(jnp.concatenate([bands, bands]))
    w1 = jnp.zeros((LANES, lp['f_w1'].shape[1]), F32).at[:lp['f_w1'].shape[0]].set(lp['f_w1'])
    deltas = jnp.abs(jnp.linspace(math.log(DECAY_TARGET) / DECAY_PCT_LONG,
                                  math.log(DECAY_TARGET) / DECAY_PCT_SHORT, C, dtype=F32))[None]
    nk = lp['f_w3'].shape[1]
    tl = min(L, 256)
    args = [tw, bands128, w1, lp['f_b1'], lp['f_freq'], lp['f_w2'], lp['f_b2'], lp['f_w3'], deltas]
    const = lambda a: pl.BlockSpec(a.shape, lambda i: (0,) * a.ndim)
    k_un, ss = pl.pallas_call(
        _filter_body,
        grid=(L // tl,),
        in_specs=[pl.BlockSpec((tl, 2), lambda i: (i, 0))] + [const(a) for a in args[1:]],
        out_specs=[pl.BlockSpec((tl, nk), lambda i: (i, 0)), pl.BlockSpec((1, nk), lambda i: (0, 0))],
        out_shape=[jax.ShapeDtypeStruct((L, nk), F32), jax.ShapeDtypeStruct((1, nk), F32)],
        name="hyena_filters",
        compiler_params=_params("arbitrary"),
    )(*args)
    ss = ss.reshape(HYENA_ORDER, 2, C)
    scale = lax.rsqrt(ss[:, 0] + ss[:, 1] + EPS).reshape(1, HYENA_ORDER * C)
    return k_un, scale


def _short_conv_chunk(u_ref, r0, n, prev_last, next_first, w, b):
    u = u_ref[pl.ds(r0, n), :]
    row = lax.broadcasted_iota(jnp.int32, u.shape, 0)
    up = jnp.where(row == 0, prev_last, pltpu.roll(u, 1, axis=0))
    un = jnp.where(row == n - 1, next_first, pltpu.roll(u, n - 1, axis=0))
    return up * w[0:1] + u * w[1:2] + un * w[2:3] + b


def _dft_spectrum_body(a_ref, b_ref, s_ref, o_ref):
    o_ref[...] = jnp.dot(a_ref[...], b_ref[...], precision=HIGHEST, preferred_element_type=F32) * s_ref[...]


def _dft_spectrum(a, b, scale):
    M, K = a.shape
    ncol = b.shape[1]
    tn = min(ncol, 512)
    return pl.pallas_call(
        _dft_spectrum_body,
        grid=(ncol // tn,),
        in_specs=[pl.BlockSpec((M, K), lambda j: (0, 0)), pl.BlockSpec((K, tn), lambda j: (0, j)),
                  pl.BlockSpec((1, tn), lambda j: (0, j))],
        out_specs=pl.BlockSpec((M, tn), lambda j: (0, j)),
        out_shape=jax.ShapeDtypeStruct((M, ncol), F32),
        name="dft_spectrum",
        compiler_params=_params("arbitrary"),
    )(a, b, scale)


def _hyena_short_body(v_ref, x1_ref, x2_ref, wv_ref, w1_ref, w2_ref, bv_ref, b1_ref, b2_ref, fb_ref, h0_ref, h1_ref,
                      ff_ref, if_ref, o_ref):
    L = v_ref.shape[0]
    N = ff_ref.shape[0] // 2
    conv = lambda u_ref, w_ref, b_ref: _short_conv_chunk(u_ref, 0, L, 0.0, 0.0, w_ref[...], b_ref[...])
    z = conv(v_ref, wv_ref, bv_ref)
    gates = (conv(x1_ref, w1_ref, b1_ref), conv(x2_ref, w2_ref, b2_ref))
    for o, (gate, h_ref) in enumerate(zip(gates, (h0_ref, h1_ref))):
        X = _dot(ff_ref[...], z.astype(BF16))
        xr, xi = X[:N], X[N:]
        hr, hi = h_ref[:N, :], h_ref[N:, :]
        Y = jnp.concatenate([xr * hr - xi * hi, xr * hi + xi * hr], axis=0)
        y = _dot(if_ref[...], Y.astype(BF16))
        z = gate * (y + fb_ref[o:o + 1, :] * z)
    o_ref[...] = z


def _hyena_short(hy, lp, k_un, scale):
    B, L, _ = hy.shape
    C = D_HYENA
    N = 2 * L
    ct = 256
    ncb = C // ct
    k4 = k_un.reshape(L, HYENA_ORDER, 2, C)
    taps = jnp.concatenate([k4[:, :, 0], jnp.zeros((1, HYENA_ORDER, C), F32), k4[:0:-1, :, 1]], axis=0)
    KF = (N // 2 + 1 + SUBLANES - 1) // SUBLANES * SUBLANES
    k = np.arange(KF)
    kept = (k <= N // 2)[:, None]
    weight = np.where((k == 0) | (k == N // 2), 1.0, 2.0)[:, None] * kept / N
    ang = 2 * np.pi * np.outer(k, np.arange(N)) / N
    dft = jnp.asarray(np.concatenate([np.cos(ang) * kept, -np.sin(ang) * kept], axis=0), F32)
    H = _dft_spectrum(dft, taps.reshape(N, HYENA_ORDER * C), scale)
    fwd = jnp.asarray(np.concatenate([np.cos(ang[:, :L]) * kept, -np.sin(ang[:, :L]) * kept], axis=0), BF16)
    inv = jnp.asarray(np.concatenate([np.cos(ang[:, :L]) * weight, -np.sin(ang[:, :L]) * weight], axis=0).T, BF16)
    u = lambda g: pl.BlockSpec((None, L, ct), lambda c, b: (b, 0, g * ncb + c))
    cw = lambda g: pl.BlockSpec((3, ct), lambda c, b: (0, g * ncb + c))
    cb = lambda g: pl.BlockSpec((1, ct), lambda c, b: (0, g * ncb + c))
    const = lambda a: pl.BlockSpec(a.shape, lambda c, b: (0,) * a.ndim)
    return pl.pallas_call(
        _hyena_short_body,
        grid=(ncb, B),
        in_specs=[u(0), u(1), u(2), cw(0), cw(1), cw(2), cb(0), cb(1), cb(2),
                  pl.BlockSpec((HYENA_ORDER, ct), lambda c, b: (0, c)),
                  pl.BlockSpec((2 * KF, ct), lambda c, b: (0, c)),
                  pl.BlockSpec((2 * KF, ct), lambda c, b: (0, ncb + c)),
                  const(fwd), const(inv)],
        out_specs=pl.BlockSpec((None, L, ct), lambda c, b: (b, 0, c)),
        out_shape=jax.ShapeDtypeStruct((B, L, C), F32),
        name="hyena_short",
        compiler_params=_params("arbitrary", "arbitrary"),
    )(hy, hy, hy, lp['conv_w'], lp['conv_w'], lp['conv_w'], lp['conv_b'], lp['conv_b'], lp['conv_b'],
      lp['f_bias'], H, H, fwd, inv)


FFT_N1 = 128
FFT_BATCH = 4
FFT_GROUPS = 2


def _fft_dims(L):
    N = 2 * L
    N1 = FFT_N1
    N2 = N // N1
    step = FFT_BATCH * FFT_GROUPS
    K1 = (N1 // 2 + 1 + step - 1) // step * step
    assert N1 * N2 == N and N2 % SUBLANES == 0 and K1 % SUBLANES == 0
    return N, N1, N2, N2 + SUBLANES, 2 * K1 + SUBLANES, K1


def _fft_tables(L):
    N, N1, N2, _, _, K1 = _fft_dims(L)
    NH = N1 // 2
    n1 = np.arange(NH)
    k1 = np.arange(K1)
    n2 = np.arange(N2)
    kept = (k1 <= N1 // 2)[None, :, None]
    weight = np.where((k1 == 0) | (k1 == N1 // 2), 1.0, 2.0)[None, :, None] * kept / N
    th = 2 * np.pi * (n1[None, None, :] * k1[None, :, None] / N1 + n2[:, None, None] * k1[None, :, None] / N)
    g = np.concatenate([np.cos(th) * kept, -np.sin(th) * kept], axis=1).reshape(N2 * 2 * K1, NH)
    ig = np.concatenate([np.cos(th) * weight, -np.sin(th) * weight], axis=1).transpose(0, 2, 1)
    ig = ig.reshape(N2 * NH, 2 * K1)
    ph = 2 * np.pi * np.outer(n2, n2) / N2
    c, s = np.cos(ph), np.sin(ph)
    f2 = np.block([[c, s], [-s, c]])
    if2 = np.block([[c, -s], [s, c]])

    return {name: jnp.asarray(a, BF16) for name, a in (('g', g), ('ig', ig), ('f2', f2), ('if2', if2))}


def _mm(tab_ref, r0, nrows, x):
    return _dot(tab_ref[pl.ds(r0, nrows), :], x.astype(BF16))


def _fft_stage1(tbuf, sbuf, g, dims):
    _, N1, N2, P, Q, K1 = dims

    nb = FFT_BATCH * FFT_GROUPS

    def body(i, carry):
        n2s = [i * nb + j for j in range(nb)]
        xs = [tbuf[pl.ds(n2, N1 // 2, stride=P), :] for n2 in n2s]
        outs = [_mm(g, pl.multiple_of(n2 * 2 * K1, 2 * SUBLANES), 2 * K1, x) for n2, x in zip(n2s, xs)]
        for n2, out in zip(n2s, outs):
            sbuf[pl.ds(pl.multiple_of(n2 * Q, SUBLANES), 2 * K1), :] = out
        return carry

    lax.fori_loop(0, N2 // nb, body, 0)


def _fft_stage2_load(sbuf, k1, dims):
    _, N1, N2, _, Q, K1 = dims
    re = [sbuf[pl.ds(k1 + j, N2, stride=Q), :] for j in range(FFT_BATCH)]
    im = [sbuf[pl.ds(K1 + k1 + j, N2, stride=Q), :] for j in range(FFT_BATCH)]
    return jnp.concatenate([jnp.concatenate(re, axis=1), jnp.concatenate(im, axis=1)], axis=0)


def _hyena_long_body(v_ref, x1_ref, x2_ref, wv_ref, w1_ref, w2_ref, bv_ref, b1_ref, b2_ref, fb_ref, h0_ref, h1_ref,
                     g_ref, ig_ref, f2_ref, if2_ref, o_ref, zbuf, g1buf, g2buf, sbuf, *, dims):
    N, N1, N2, P, Q, K1 = dims
    NH = N1 // 2
    zero = jnp.zeros((1, LANES), F32)

    for u_ref, w_ref, b_ref, buf in ((v_ref, wv_ref, bv_ref, zbuf), (x1_ref, w1_ref, b1_ref, g1buf),
                                     (x2_ref, w2_ref, b2_ref, g2buf)):
        w, b = w_ref[...], b_ref[...]
        for n1 in range(NH):
            r0 = n1 * N2
            prev_last = zero if n1 == 0 else u_ref[r0 - 1:r0, :]
            next_first = zero if n1 == NH - 1 else u_ref[r0 + N2:r0 + N2 + 1, :]
            buf[n1 * P:n1 * P + N2, :] = _short_conv_chunk(u_ref, r0, N2, prev_last, next_first, w, b)

    for o, (gbuf, h_ref) in enumerate(((g1buf, h0_ref), (g2buf, h1_ref))):
        _fft_stage1(zbuf, sbuf, g_ref, dims)

        def per_k1(i, carry):
            k1s = [(i * FFT_GROUPS + g) * FFT_BATCH for g in range(FFT_GROUPS)]
            Bs = [_fft_stage2_load(sbuf, k1, dims).astype(BF16) for k1 in k1s]
            Ccs = []
            for k1, B in zip(k1s, Bs):
                X = _dot(f2_ref[...], B)
                h0 = [pl.multiple_of((k1 + j) * 2 * N2, 2 * N2) for j in range(FFT_BATCH)]
                hr = jnp.concatenate([h_ref[pl.ds(r, N2), :] for r in h0], axis=1)
                hi = jnp.concatenate([h_ref[pl.ds(r + N2, N2), :] for r in h0], axis=1)
                xr, xi = X[:N2], X[N2:]
                Y = jnp.concatenate([xr * hr - xi * hi, xr * hi + xi * hr], axis=0)
                Ccs.append(_dot(if2_ref[...], Y.astype(BF16)))
            for k1, Cc in zip(k1s, Ccs):
                for j in range(FFT_BATCH):
                    lanes = slice(j * LANES, (j + 1) * LANES)
                    sbuf[pl.ds(k1 + j, N2, stride=Q), :] = Cc[:N2, lanes]
                    sbuf[pl.ds(K1 + k1 + j, N2, stride=Q), :] = Cc[N2:, lanes]
            return carry

        lax.fori_loop(0, K1 // (FFT_BATCH * FFT_GROUPS), per_k1, 0)
        fb = fb_ref[o:o + 1, :]

        nb = FFT_BATCH * FFT_GROUPS

        def per_n2(i, carry):
            n2s = [i * nb + j for j in range(nb)]
            Ds = [sbuf[pl.ds(pl.multiple_of(n2 * Q, SUBLANES), 2 * K1), :].astype(BF16) for n2 in n2s]
            ys = [_dot(ig_ref[pl.ds(pl.multiple_of(n2 * NH, NH), NH), :], D) for n2, D in zip(n2s, Ds)]
            for n2, y in zip(n2s, ys):
                rows = pl.ds(n2, NH, stride=P)
                zbuf[rows, :] = gbuf[rows, :] * (y + fb * zbuf[rows, :])
            return carry

        lax.fori_loop(0, N2 // nb, per_n2, 0)

    for n1 in range(NH):
        o_ref[n1 * N2:(n1 + 1) * N2, :] = zbuf[n1 * P:n1 * P + N2, :]


def _fft_spectrum_body(hf_ref, hb_ref, s_ref, g_ref, f2_ref, o_ref, tbuf, sbuf, *, dims):
    N, N1, N2, P, Q, K1 = dims
    NH = N1 // 2
    scale = s_ref[...]
    for d, h_ref in enumerate((hf_ref, hb_ref)):
        for n1 in range(NH):
            h = h_ref[n1 * N2:(n1 + 1) * N2, :] * scale
            if d == 1 and n1 == 0:
                h = jnp.where(lax.broadcasted_iota(jnp.int32, h.shape, 0) == 0, 0.0, h)
            tbuf[n1 * P:n1 * P + N2, :] = h
        _fft_stage1(tbuf, sbuf, g_ref, dims)

        def per_k1(i, carry):
            k1 = i * FFT_BATCH
            X = _mm(f2_ref, 0, 2 * N2, _fft_stage2_load(sbuf, k1, dims))
            for j in range(FFT_BATCH):
                h0 = pl.multiple_of((k1 + j) * 2 * N2, 2 * N2)
                Xj = X[:, j * LANES:(j + 1) * LANES]
                if d == 0:
                    o_ref[pl.ds(h0, 2 * N2), :] = Xj
                else:
                    o_ref[pl.ds(h0, N2), :] += Xj[:N2]
                    o_ref[pl.ds(h0 + N2, N2), :] -= Xj[N2:]
            return carry

        lax.fori_loop(0, K1 // FFT_BATCH, per_k1, 0)


def _hyena_long(hy, lp, k_un, scale):
    B, L, _ = hy.shape
    C = D_HYENA
    dims = _fft_dims(L)
    N, N1, N2, P, Q, K1 = dims
    NH = N1 // 2
    tabs = _fft_tables(L)
    ncb = C // LANES
    nspec = HYENA_ORDER * ncb
    const1 = lambda a: pl.BlockSpec(a.shape, (lambda *i: (0,) * a.ndim), pipeline_mode=pl.Buffered(1))
    hcol = lambda d: pl.BlockSpec((L, LANES), lambda j: (0, (j // ncb) * 2 * ncb + d * ncb + j % ncb))
    H = pl.pallas_call(
        functools.partial(_fft_spectrum_body, dims=dims),
        grid=(nspec,),
        in_specs=[hcol(0), hcol(1), pl.BlockSpec((1, LANES), lambda j: (0, j)),
                  const1(tabs['g']), const1(tabs['f2'])],
        out_specs=pl.BlockSpec((K1 * 2 * N2, LANES), lambda j: (0, j)),
        out_shape=jax.ShapeDtypeStruct((K1 * 2 * N2, HYENA_ORDER * C), F32),
        scratch_shapes=[pltpu.VMEM((NH * P, LANES), F32), pltpu.VMEM((N2 * Q, LANES), F32)],
        name="filter_spectrum",
        compiler_params=_params("arbitrary"),
    )(k_un, k_un, scale, tabs['g'], tabs['f2'])

    one = pl.Buffered(1)
    hspec = lambda o: pl.BlockSpec((K1 * 2 * N2, LANES), lambda c, b: (0, o * ncb + c), pipeline_mode=one)
    u = lambda g: pl.BlockSpec((None, L, LANES), lambda c, b: (b, 0, g * ncb + c), pipeline_mode=one)
    cw = lambda g: pl.BlockSpec((3, LANES), lambda c, b: (0, g * ncb + c))
    cb = lambda g: pl.BlockSpec((1, LANES), lambda c, b: (0, g * ncb + c))
    return pl.pallas_call(
        functools.partial(_hyena_long_body, dims=dims),
        grid=(ncb, B),
        in_specs=[u(0), u(1), u(2), cw(0), cw(1), cw(2), cb(0), cb(1), cb(2),
                  pl.BlockSpec((HYENA_ORDER, LANES), lambda c, b: (0, c)),
                  hspec(0), hspec(1),
                  const1(tabs['g']), const1(tabs['ig']), const1(tabs['f2']), const1(tabs['if2'])],
        out_specs=pl.BlockSpec((None, L, LANES), lambda c, b: (b, 0, c)),
        out_shape=jax.ShapeDtypeStruct((B, L, C), F32),
        scratch_shapes=[pltpu.VMEM((NH * P, LANES), F32), pltpu.VMEM((NH * P, LANES), F32),
                        pltpu.VMEM((NH * P, LANES), F32), pltpu.VMEM((N2 * Q, LANES), F32)],
        name="hyena_long",
        compiler_params=_params("arbitrary", "arbitrary"),
    )(hy, hy, hy, lp['conv_w'], lp['conv_w'], lp['conv_w'], lp['conv_b'], lp['conv_b'], lp['conv_b'],
      lp['f_bias'], H, H, tabs['g'], tabs['ig'], tabs['f2'], tabs['if2'])


def _hyena(hy, lp):
    L = hy.shape[1]
    k_un, scale = _hyena_filters(L, lp)
    if 2 * L >= 2 * FFT_N1 * SUBLANES and (2 * L) % (FFT_N1 * SUBLANES) == 0:
        return _hyena_long(hy, lp, k_un, scale)
    return _hyena_short(hy, lp, k_un, scale)


def kernel(x_prompt, x_sample, cache_k, cache_v, c, c_ctx, w_mod, b_mod, g_mix, w_in, conv_w, conv_b, f_w1, f_b1,
           f_freq, f_w2, f_b2, f_w3, f_bias, rpb, g_out_hy, g_out_at, w_out, g_ffn, w_router, b_router, w_exp1,
           b_exp1, w_exp2, b_exp2, g_final):
    depth = w_mod.shape[0]
    Bp, Lp, D = x_prompt.shape
    Bs, Ls, _ = x_sample.shape
    n_ctx, n_lat = Bp * Lp, Bs * Ls
    n_tok = n_ctx + n_lat
    d_attn = N_HEADS * HEAD_DIM
    q_off = 3 * D_HYENA
    row2 = lambda a: a.reshape(1, -1)

    cond = jnp.zeros((SUBLANES, D), F32).at[0].set(c_ctx).at[1:1 + Bs].set(c)
    xp, xs = x_prompt, x_sample
    new_k, new_v = [], []
    for l in range(depth):
        wr = jnp.zeros((D, LANES), F32).at[:, :N_EXPERTS].set(w_router[l])
        wr_hi = wr.astype(BF16)
        lp = {
            'conv_w': conv_w[l], 'conv_b': row2(conv_b[l]), 'f_w1': f_w1[l], 'f_b1': row2(f_b1[l]),
            'f_freq': row2(f_freq[l]), 'f_w2': f_w2[l], 'f_b2': row2(f_b2[l]), 'f_w3': f_w3[l],
            'f_bias': f_bias[l], 'g_out_hy': row2(g_out_hy[l]), 'g_out_at': row2(g_out_at[l]),
            'w_out': w_out[l].astype(BF16), 'g_ffn': row2(g_ffn[l]),
            'wr_hi': wr_hi, 'wr_lo': (wr - wr_hi.astype(F32)).astype(BF16),
            'b_router': jnp.zeros((1, LANES), F32).at[0, :N_EXPERTS].set(b_router[l]),
            'w_exp1': w_exp1[l], 'b_exp1': b_exp1[l][:, None, :], 'w_exp2': w_exp2[l],
            'b_exp2': b_exp2[l][:, None, :],
        }
        mod = _modulation(cond, w_mod[l], row2(b_mod[l]))
        mod3 = mod.reshape(SUBLANES, 1, 6 * D)
        w_in_b = w_in[l].astype(BF16)
        g_mix_l = row2(g_mix[l])

        proj_p = _in_projection(xp, mod3, 0, g_mix_l, w_in_b)
        proj_s = _in_projection(xs, mod3, 1, g_mix_l, w_in_b)
        kv = proj_p[..., q_off + d_attn:].reshape(Bp, Lp, 2, N_HEADS, HEAD_DIM).transpose(2, 0, 3, 1, 4)
        new_k.append(kv[0])
        new_v.append(kv[1])

        hy_p = _hyena(proj_p, lp)
        hy_s = _hyena(proj_s, lp)
        at_p = _context_attention(proj_p, q_off)
        heads_last = lambda t: t.transpose(0, 2, 1, 3).reshape(Bs, t.shape[2], d_attn)
        at_s = _neighbourhood_attention(proj_s, q_off, heads_last(cache_k[:, l]), heads_last(cache_v[:, l]), rpb[l])

        x1, h2u, logits = _out_projection((hy_p, at_p, xp), (hy_s, at_s, xs), mod3, lp)

        gates, ce, nv, n_used, gidx, sidx = _routing(logits[:, :N_EXPERTS], n_tok)
        yp = _experts(ce, nv, n_used, gidx, sidx, h2u, lp, n_tok * TOP_K + MOE_CHUNK)
        last = l == depth - 1
        gf = row2(g_final) if last else None
        assert last, "deeper stacks need the un-normalised residual between layers"
        xp = _combine(x1, yp, gates, mod3, 0, gf, 0, Bp, Lp)
        xs = _combine(x1, yp, gates, mod3, 1, gf, n_ctx, Bs, Ls)

    return xp, xs, jnp.stack(new_k, axis=1), jnp.stack(new_v, axis=1)
```

```python
import functools
import math

import numpy as np
import jax
import jax.numpy as jnp
from jax import lax
from jax.experimental import pallas as pl
from jax.experimental.pallas import tpu as pltpu

F32 = jnp.float32
BF16 = jnp.bfloat16
HIGHEST = lax.Precision.HIGHEST

GRID_W = 64
N_HEADS = 16
HEAD_DIM = 64
D_HYENA = 1024
HYENA_ORDER = 2
FILTER_BANDS = 16
DECAY_TARGET = 1e-2
DECAY_PCT_SHORT = 0.3
DECAY_PCT_LONG = 1.5
WIN_H = 8
WIN_W = 16
N_EXPERTS = 32
TOP_K = 4
SWIGLU_ALPHA = 1.702
SWIGLU_LIMIT = 7.0
EPS = 1e-6
NEG_INF = -1e30

LANES = 128
SUBLANES = 8
VMEM_LIMIT = 60 * 1024 * 1024

ROW_TILE = 256
NA_Q_ROWS = 4
MOE_CHUNK = 1024
MOE_SUB = 256
MOE_FF_TILE = 512
MOE_ISSUE = 64


def _params(*sem):
    return pltpu.CompilerParams(dimension_semantics=sem, vmem_limit_bytes=VMEM_LIMIT)


def _rms(x, g):
    return x * lax.rsqrt(jnp.mean(x * x, axis=-1, keepdims=True) + EPS) * g


def _split_bf16(x):
    hi = x.astype(BF16)
    lo = (x - hi.astype(F32)).astype(BF16)
    return hi, lo


def _pack_bf16_pair(lo, hi):
    bl = lax.bitcast_convert_type(lo.astype(BF16).astype(F32), jnp.uint32)
    bh = lax.bitcast_convert_type(hi.astype(BF16).astype(F32), jnp.uint32)
    return (bl >> 16) | (bh & jnp.uint32(0xFFFF0000))


def _unpack_bf16_pair(words):
    lo = lax.bitcast_convert_type(words << 16, F32)
    hi = lax.bitcast_convert_type(words & jnp.uint32(0xFFFF0000), F32)
    return lo, hi


def _dot(a, b):
    return jnp.dot(a, b, preferred_element_type=F32)


def _dot_nt(a, b):
    return lax.dot_general(a, b, (((1,), (1,)), ((), ())), preferred_element_type=F32)


def _mod_body(c_ref, w_ref, b_ref, o_ref):
    c = c_ref[...]
    s = c / (1.0 + jnp.exp(-c))
    o_ref[...] = jnp.dot(s, w_ref[...], precision=HIGHEST, preferred_element_type=F32) + b_ref[...]


def _modulation(cc, w, b):
    D, N = w.shape
    tn = min(N, 1536)
    return pl.pallas_call(
        _mod_body,
        grid=(N // tn,),
        in_specs=[pl.BlockSpec((SUBLANES, D), lambda j: (0, 0)),
                  pl.BlockSpec((D, tn), lambda j: (0, j)),
                  pl.BlockSpec((1, tn), lambda j: (0, j))],
        out_specs=pl.BlockSpec((SUBLANES, tn), lambda j: (0, j)),
        out_shape=jax.ShapeDtypeStruct((SUBLANES, N), F32),
        name="modulation",
        compiler_params=_params("arbitrary"),
    )(cc, w, b)


def _inproj_body(x_ref, shift_ref, scale_ref, g_ref, w_ref, o_ref, *, n_chunk):
    h = _rms(x_ref[...], g_ref[...]) * (1.0 + scale_ref[...]) + shift_ref[...]
    hb = h.astype(BF16)
    n_out = o_ref.shape[-1]

    def col(j, carry):
        c0 = pl.multiple_of(j * n_chunk, n_chunk)
        o_ref[:, pl.ds(c0, n_chunk)] = _dot(hb, w_ref[:, pl.ds(c0, n_chunk)])
        return carry

    lax.fori_loop(0, n_out // n_chunk, col, 0)


def _in_projection(x, mod3, row0, g, w_bf16):
    B, L, D = x.shape
    N = w_bf16.shape[1]
    tm = min(ROW_TILE, L)
    mrow = (lambda b: 0) if row0 == 0 else (lambda b: b + row0)
    return pl.pallas_call(
        functools.partial(_inproj_body, n_chunk=1024),
        grid=(B, L // tm),
        in_specs=[pl.BlockSpec((None, tm, D), lambda b, i: (b, i, 0)),
                  pl.BlockSpec((None, 1, D), lambda b, i: (mrow(b), 0, 0)),
                  pl.BlockSpec((None, 1, D), lambda b, i: (mrow(b), 0, 1)),
                  pl.BlockSpec((1, D), lambda b, i: (0, 0)),
                  pl.BlockSpec((D, N), lambda b, i: (0, 0), pipeline_mode=pl.Buffered(1))],
        out_specs=pl.BlockSpec((None, tm, N), lambda b, i: (b, i, 0)),
        out_shape=jax.ShapeDtypeStruct((B, L, N), F32),
        name="in_projection",
        compiler_params=_params("arbitrary", "arbitrary"),
    )(x, mod3, mod3, g, w_bf16)


def _head_lane_masks(rows, hd):
    lane = lax.broadcasted_iota(jnp.int32, (rows, LANES), 1)
    return [(lane >= h * hd) & (lane < (h + 1) * hd) for h in range(LANES // hd)]


def _ctx_attn_body(q_ref, k_ref, v_ref, o_ref, *, scale, hd):
    q = q_ref[...]
    k = k_ref[...].astype(BF16)
    v = v_ref[...].astype(BF16)
    masks = _head_lane_masks(q.shape[0], hd)
    out = None
    for msk in masks:
        qh = jnp.where(msk, q, 0.0).astype(BF16)
        s = _dot_nt(qh, k) * scale
        p = jnp.exp(s - jnp.max(s, axis=-1, keepdims=True))
        o = _dot(p.astype(BF16), v) / jnp.sum(p, axis=-1, keepdims=True)
        out = o if out is None else jnp.where(msk, o, out)
    o_ref[...] = out


def _context_attention(proj, q_off):
    B, L, _ = proj.shape
    d_attn = N_HEADS * HEAD_DIM
    nhp = d_attn // LANES
    qb, kb, vb = (q_off // LANES, (q_off + d_attn) // LANES, (q_off + 2 * d_attn) // LANES)
    spec = lambda base: pl.BlockSpec((None, L, LANES), lambda b, h: (b, 0, base + h))
    return pl.pallas_call(
        functools.partial(_ctx_attn_body, scale=1.0 / math.sqrt(HEAD_DIM), hd=HEAD_DIM),
        grid=(B, nhp),
        in_specs=[spec(qb), spec(kb), spec(vb)],
        out_specs=pl.BlockSpec((None, L, LANES), lambda b, h: (b, 0, h)),
        out_shape=jax.ShapeDtypeStruct((B, L, d_attn), F32),
        name="context_attention",
        compiler_params=_params("arbitrary", "arbitrary"),
    )(proj, proj, proj)


def _na_tables(rows, rpb):
    W = GRID_W
    kh = min(WIN_H, rows)
    rbq = NA_Q_ROWS
    kwr = rbq + kh
    assert rows % rbq == 0 and kwr <= rows
    nblk = rows // rbq
    ws = np.clip(np.arange(nblk) * rbq - kh // 2, 0, rows - kwr)
    rq = np.arange(nblk)[:, None, None] * rbq + np.arange(rbq)[None, :, None]
    rk = ws[:, None, None] + np.arange(kwr)[None, None, :]
    rs = np.clip(rq - kh // 2, 0, rows - kh)
    vr = (rk >= rs) & (rk < rs + kh)
    dr = np.where(vr, rk - rq + WIN_H - 1, 0)
    patterns = np.concatenate([dr.reshape(nblk, -1), vr.reshape(nblk, -1)], axis=1)
    _, first, btype = np.unique(patterns, axis=0, return_index=True, return_inverse=True)
    dr_t, vr_t = dr[first], vr[first]
    cq = np.arange(W)[:, None]
    ck = np.arange(W)[None, :]
    cs = np.clip(cq - WIN_W // 2, 0, W - WIN_W)
    vc = (ck >= cs) & (ck < cs + WIN_W)
    dc = np.clip(ck - cq, -(WIN_W - 1), WIN_W - 1) + WIN_W - 1
    nt = dr_t.shape[0]
    nr, ncol = 2 * WIN_H - 1, 2 * WIN_W - 1
    rowsel = np.where(vr_t, dr_t, nr)[..., None] == np.arange(nr + 1)
    colsel = np.where(vc, dc, ncol)[..., None] == np.arange(ncol + 1)
    rpb_ext = jnp.full((rpb.shape[0], nr + 1, ncol + 1), NEG_INF, F32).at[:, :nr, :ncol].set(rpb.astype(F32))
    bias = jnp.einsum('tajr,hrc,qkc->htaqjk', rowsel.astype(np.float32), rpb_ext,
                      colsel.astype(np.float32), precision=HIGHEST)
    bias = bias.reshape(rpb.shape[0], nt, rbq * W, kwr * W)
    return ws.astype(np.int32), btype.reshape(-1).astype(np.int32), bias


def _na_body(ws_ref, bt_ref, q_ref, k_ref, v_ref, kc_ref, vc_ref, bias_ref, o_ref, *, scale, hd, nk):
    i = pl.program_id(2)
    start = pl.multiple_of(ws_ref[i] * GRID_W, GRID_W)
    q = q_ref[...] * scale
    kl = k_ref[pl.ds(start, nk), :].astype(BF16)
    vl = v_ref[pl.ds(start, nk), :]
    kc = kc_ref[...].astype(BF16)
    vc = vc_ref[...]
    masks = _head_lane_masks(q.shape[0], hd)
    vl_masks = _head_lane_masks(nk, hd)
    vc_masks = _head_lane_masks(vc.shape[0], hd)
    assert len(masks) == 2, "the value lanes of the other head carry the softmax denominator"
    out = None
    for h, msk in enumerate(masks):
        qh = jnp.where(msk, q, 0.0).astype(BF16)
        sl = _dot_nt(qh, kl) + bias_ref[h]
        sc = _dot_nt(qh, kc)
        m = jnp.maximum(jnp.max(sl, axis=-1, keepdims=True), jnp.max(sc, axis=-1, keepdims=True))
        p_l = jnp.exp(sl - m).astype(BF16)
        p_c = jnp.exp(sc - m).astype(BF16)
        o = (_dot(p_l, jnp.where(vl_masks[h], vl, 1.0).astype(BF16))
             + _dot(p_c, jnp.where(vc_masks[h], vc, 1.0).astype(BF16)))
        o = o / pltpu.roll(o, hd, axis=1)
        out = o if out is None else jnp.where(msk, o, out)
    o_ref[...] = out


def _neighbourhood_attention(proj, q_off, kc, vc, rpb):
    B, L, _ = proj.shape
    Lc = kc.shape[1]
    d_attn = N_HEADS * HEAD_DIM
    hpb = LANES // HEAD_DIM
    nhp = d_attn // LANES
    rows = L // GRID_W
    ws, btype, bias = _na_tables(rows, rpb)
    nq = NA_Q_ROWS * GRID_W
    nk = (NA_Q_ROWS + min(WIN_H, rows)) * GRID_W
    nblk = rows // NA_Q_ROWS
    qb, kb, vb = (q_off // LANES, (q_off + d_attn) // LANES, (q_off + 2 * d_attn) // LANES)
    full = lambda base: pl.BlockSpec((None, L, LANES), lambda b, h, i, ws_r, bt_r: (b, 0, base + h))
    ctx = pl.BlockSpec((None, Lc, LANES), lambda b, h, i, ws_r, bt_r: (b, 0, h))
    grid_spec = pltpu.PrefetchScalarGridSpec(
        num_scalar_prefetch=2,
        grid=(B, nhp, nblk),
        in_specs=[pl.BlockSpec((None, nq, LANES), lambda b, h, i, ws_r, bt_r: (b, i, qb + h)),
                  full(kb), full(vb), ctx, ctx,
                  pl.BlockSpec((hpb, None, nq, nk), lambda b, h, i, ws_r, bt_r: (h, bt_r[i], 0, 0))],
        out_specs=pl.BlockSpec((None, nq, LANES), lambda b, h, i, ws_r, bt_r: (b, i, h)),
    )
    return pl.pallas_call(
        functools.partial(_na_body, scale=1.0 / math.sqrt(HEAD_DIM), hd=HEAD_DIM, nk=nk),
        grid_spec=grid_spec,
        out_shape=jax.ShapeDtypeStruct((B, L, d_attn), F32),
        name="neighbourhood_attention",
        compiler_params=_params("arbitrary", "arbitrary", "arbitrary"),
    )(jnp.asarray(ws), jnp.asarray(btype), proj, proj, proj, kc, vc, bias)


def _outproj_body(yhp_ref, yap_ref, xp_ref, yhs_ref, yas_ref, xs_ref, gate1_ref, shift2_ref, scale2_ref, ghy_ref,
                  gat_ref, w_ref, gffn_ref, wrh_ref, wrl_ref, br_ref, x1_ref, h2u_ref, lg_ref, *, n_ctx_tiles):
    def run(yh_ref, ya_ref, x_ref):
        cat = jnp.concatenate([_rms(yh_ref[...], ghy_ref[...]), _rms(ya_ref[...], gat_ref[...])], axis=-1)
        mix = _dot(cat.astype(BF16), w_ref[...])
        x1 = x_ref[...] + gate1_ref[...] * mix
        x1_ref[...] = x1
        h2 = _rms(x1, gffn_ref[...]) * (1.0 + scale2_ref[...]) + shift2_ref[...]
        hi, lo = _split_bf16(h2)
        lg_ref[...] = (_dot(hi, wrh_ref[...]) + _dot(lo, wrh_ref[...]) + _dot(hi, wrl_ref[...])) + br_ref[...]
        half = h2.shape[-1] // 2
        words = _pack_bf16_pair(h2[:, :half], h2[:, half:])
        per = half // LANES
        for s in range(per):
            h2u_ref[pl.ds(s, words.shape[0], stride=per), :] = words[:, s * LANES:(s + 1) * LANES]

    is_ctx = pl.program_id(0) < n_ctx_tiles
    pl.when(is_ctx)(lambda: run(yhp_ref, yap_ref, xp_ref))
    pl.when(jnp.logical_not(is_ctx))(lambda: run(yhs_ref, yas_ref, xs_ref))


def _out_projection(ctx, lat, mod3, lp):
    Bp, Lp, D = ctx[2].shape
    Bs, Ls, _ = lat[2].shape
    dh, da = ctx[0].shape[-1], ctx[1].shape[-1]
    tm = min(ROW_TILE, Lp, Ls)
    nct, nlt = Bp * Lp // tm, Bs * Ls // tm
    n_tok = (nct + nlt) * tm
    lat_tiles = Ls // tm
    mrow = lambda i: jnp.where(i < nct, 0, 1 + (i - nct) // lat_tiles)
    modspec = lambda c: pl.BlockSpec((None, 1, D), lambda i: (mrow(i), 0, c))
    const = lambda shape: pl.BlockSpec(shape, lambda i: (0,) * len(shape))
    crow = lambda w: pl.BlockSpec((tm, w), lambda i: (jnp.minimum(i, nct - 1), 0))
    lrow = lambda w: pl.BlockSpec((tm, w), lambda i: (jnp.maximum(i - nct, 0), 0))
    flat = lambda t: t.reshape(-1, t.shape[-1])
    return pl.pallas_call(
        functools.partial(_outproj_body, n_ctx_tiles=nct),
        grid=(nct + nlt,),
        in_specs=[crow(dh), crow(da), crow(D), lrow(dh), lrow(da), lrow(D),
                  modspec(2), modspec(3), modspec(4),
                  const((1, dh)), const((1, da)),
                  pl.BlockSpec((dh + da, D), lambda i: (0, 0), pipeline_mode=pl.Buffered(1)),
                  const((1, D)), const((D, LANES)), const((D, LANES)), const((1, LANES))],
        out_specs=[pl.BlockSpec((tm, D), lambda i: (i, 0)),
                   pl.BlockSpec((tm * (D // 2 // LANES), LANES), lambda i: (i, 0)),
                   pl.BlockSpec((tm, LANES), lambda i: (i, 0))],
        out_shape=[jax.ShapeDtypeStruct((n_tok, D), F32),
                   jax.ShapeDtypeStruct((n_tok * (D // 2 // LANES), LANES), jnp.uint32),
                   jax.ShapeDtypeStruct((n_tok, LANES), F32)],
        name="out_projection",
        compiler_params=_params("arbitrary"),
    )(*[flat(t) for t in ctx], *[flat(t) for t in lat], mod3, mod3, mod3, lp['g_out_hy'], lp['g_out_at'],
      lp['w_out'], lp['g_ffn'], lp['wr_hi'], lp['wr_lo'], lp['b_router'])


def _expert_body(ce_ref, nv_ref, nu_ref, gcur_ref, gnext_ref, sprev_ref, scur_ref, h2u_hbm, w1g_ref, w1u_ref,
                 b1g_ref, b1u_ref, w2_ref, b2_ref, yp_hbm, xu, xb, acc, ystage, abuf, gsem, ssem):
    c = pl.program_id(0)
    f = pl.program_id(1)
    nc = pl.num_programs(0)
    nf = pl.num_programs(1)
    half = xb.shape[1] // 2
    xt = half // LANES
    yt = xt
    slot = c % 2
    subs = lambda n: (n + MOE_SUB - 1) // MOE_SUB
    nv = nv_ref[c]
    nsub = subs(nv)
    nrows = nsub * MOE_SUB
    last = c == nu_ref[0] - 1
    nv_prev = nv_ref[jnp.maximum(c - 1, 0)]
    nrows_prev = subs(nv_prev) * MOE_SUB
    covered_prev = MOE_ISSUE * nf * subs(nv_prev)
    nrows_next = jnp.where(c + 1 < nc, subs(nv_ref[jnp.minimum(c + 1, nc - 1)]), 0) * MOE_SUB
    covered = MOE_ISSUE * nf * nsub

    def gather_row(idx_ref, r, to_slot):
        tok = idx_ref[0, r]
        pltpu.make_async_copy(h2u_hbm.at[pl.ds(pl.multiple_of(tok * xt, xt), xt), :],
                              xu.at[to_slot, pl.ds(pl.multiple_of(r * xt, xt), xt), :], gsem.at[to_slot]).start()

    def scatter_row(dst, r):
        pltpu.make_async_copy(ystage.at[pl.ds(pl.multiple_of(r * yt, yt), yt), :],
                              yp_hbm.at[pl.ds(pl.multiple_of(dst * yt, yt), yt), :], ssem).start()

    def for_rows(lo, hi, fn):
        lax.fori_loop(lo, hi, lambda r, carry: (fn(r), carry)[1], 0)

    def drain(n, src, dst, sem, per):
        piece = lambda m: pltpu.make_async_copy(src.at[pl.ds(0, m * per), :], dst.at[pl.ds(0, m * per), :], sem)
        for_rows(0, n // MOE_SUB, lambda i: piece(MOE_SUB).wait())
        bit = MOE_SUB // 2
        while bit:
            if not isinstance(n, int):
                pl.when((n & bit) != 0)(piece(bit).wait)
            elif n & bit:
                piece(bit).wait()
            bit //= 2

    @pl.when(c < nu_ref[0])
    def _chunk():
        @pl.when(f == 0)
        def _arrive():
            @pl.when(c == 0)
            def _first():
                for_rows(0, nrows, lambda r: gather_row(gcur_ref, r, 0))

            started = jnp.where(c == 0, nrows, jnp.maximum(covered_prev, nrows))
            drain(started, h2u_hbm, xu.at[slot], gsem.at[slot], xt)

            def unpack(sb, carry):
                r0 = pl.multiple_of(sb * MOE_SUB, MOE_SUB)
                for s in range(xt):
                    lo, hi = _unpack_bf16_pair(xu[slot, pl.ds(r0 * xt + s, MOE_SUB, stride=xt), :])
                    xb[pl.ds(r0, MOE_SUB), s * LANES:(s + 1) * LANES] = lo.astype(BF16)
                    xb[pl.ds(r0, MOE_SUB), half + s * LANES:half + (s + 1) * LANES] = hi.astype(BF16)
                acc[pl.ds(r0, MOE_SUB), :] = jnp.zeros((MOE_SUB, acc.shape[1]), F32)
                return carry

            lax.fori_loop(0, nsub, unpack, 0)

        def matmul_steps(with_scatter):
            def up_proj(sb):
                base = (f * nsub + sb) * MOE_ISSUE
                for j in range(MOE_ISSUE):
                    gather_row(gnext_ref, base + j, 1 - slot)
                if with_scatter:
                    for j in range(MOE_ISSUE):
                        scatter_row(sprev_ref[0, base + j], jnp.minimum(base + j, nrows_prev - 1))
                x = xb[pl.ds(pl.multiple_of(sb * MOE_SUB, MOE_SUB), MOE_SUB), :]
                gate = jnp.minimum(_dot(x, w1g_ref[...].astype(BF16)) + b1g_ref[...], SWIGLU_LIMIT)
                up = jnp.clip(_dot(x, w1u_ref[...].astype(BF16)) + b1u_ref[...], -SWIGLU_LIMIT, SWIGLU_LIMIT)
                glu = gate / (1.0 + jnp.exp(-SWIGLU_ALPHA * gate))
                abuf[...] = ((up + 1.0) * glu).astype(BF16)

            def down_proj(sb):
                acc[pl.ds(pl.multiple_of(sb * MOE_SUB, MOE_SUB), MOE_SUB), :] += _dot(abuf[...], w2_ref[...].astype(BF16))

            def step(sb, carry):
                down_proj(sb - 1)
                up_proj(sb)
                return carry

            up_proj(0)
            lax.fori_loop(1, nsub, step, 0)
            down_proj(nsub - 1)

        pl.when(c == 0)(lambda: matmul_steps(False))
        pl.when(c > 0)(lambda: matmul_steps(True))

        @pl.when(f == nf - 1)
        def _leave():
            for_rows(covered, nrows_next, lambda r: gather_row(gnext_ref, r, 1 - slot))

            @pl.when(c > 0)
            def _prev_out():
                for_rows(covered, nv_prev, lambda r: scatter_row(sprev_ref[0, r], r))
                drain(jnp.maximum(covered, nv_prev), yp_hbm, ystage, ssem, yt)

            def stage(sb, carry):
                r0 = pl.multiple_of(sb * MOE_SUB, MOE_SUB)
                for s in range(yt):
                    lo, hi = slice(s * LANES, (s + 1) * LANES), slice(half + s * LANES, half + (s + 1) * LANES)
                    ystage[pl.ds(r0 * yt + s, MOE_SUB, stride=yt), :] = _pack_bf16_pair(
                        acc[pl.ds(r0, MOE_SUB), lo] + b2_ref[:, lo], acc[pl.ds(r0, MOE_SUB), hi] + b2_ref[:, hi])
                return carry

            lax.fori_loop(0, nsub, stage, 0)

            @pl.when(last)
            def _flush():
                for_rows(0, nv, lambda r: scatter_row(scur_ref[0, r], r))
                spare0 = yp_hbm.shape[0] // yt - xb.shape[0]
                for_rows(0, xb.shape[0], lambda r: scatter_row(spare0 + r, jnp.minimum(r, nrows - 1)))
                drain(nv, yp_hbm, ystage, ssem, yt)
                drain(xb.shape[0], yp_hbm, ystage, ssem, yt)
                drain(jnp.maximum(covered, nrows_next), h2u_hbm, xu.at[1 - slot], gsem.at[1 - slot], xt)


def _experts(chunk_e, chunk_nv, n_used, gidx, sidx, h2u, lp, n_out_rows):
    E, D, two_ff = lp['w_exp1'].shape
    d_ff = two_ff // 2
    nc = gidx.shape[0]
    R, tf = MOE_CHUNK, MOE_FF_TILE
    nf = d_ff // tf
    assert MOE_ISSUE * nf * (R // MOE_SUB) <= R, "a chunk's matmul steps must not start more rows than a chunk holds"

    def ff(c, f, nu):
        return jnp.where(c < nu[0], f, nf - 1)

    smem = lambda step: pl.BlockSpec((None, 1, R), lambda c, f, ce, nv, nu: (jnp.clip(c + step, 0, nc - 1), 0, 0),
                                     memory_space=pltpu.SMEM)
    grid_spec = pltpu.PrefetchScalarGridSpec(
        num_scalar_prefetch=3,
        grid=(nc, nf),
        in_specs=[smem(0), smem(1), smem(-1), smem(0),
                  pl.BlockSpec(memory_space=pl.ANY),
                  pl.BlockSpec((None, D, tf), lambda c, f, ce, nv, nu: (ce[c], 0, ff(c, f, nu))),
                  pl.BlockSpec((None, D, tf), lambda c, f, ce, nv, nu: (ce[c], 0, nf + ff(c, f, nu))),
                  pl.BlockSpec((None, 1, tf), lambda c, f, ce, nv, nu: (ce[c], 0, ff(c, f, nu))),
                  pl.BlockSpec((None, 1, tf), lambda c, f, ce, nv, nu: (ce[c], 0, nf + ff(c, f, nu))),
                  pl.BlockSpec((None, tf, D), lambda c, f, ce, nv, nu: (ce[c], ff(c, f, nu), 0)),
                  pl.BlockSpec((None, 1, D), lambda c, f, ce, nv, nu: (ce[c], 0, 0))],
        out_specs=pl.BlockSpec(memory_space=pl.ANY),
        scratch_shapes=[pltpu.VMEM((2, R * (D // 2 // LANES), LANES), jnp.uint32),
                        pltpu.VMEM((R, D), BF16),
                        pltpu.VMEM((R, D), F32),
                        pltpu.VMEM((R * (D // 2 // LANES), LANES), jnp.uint32),
                        pltpu.VMEM((MOE_SUB, tf), BF16),
                        pltpu.SemaphoreType.DMA((2,)),
                        pltpu.SemaphoreType.DMA(())],
    )
    return pl.pallas_call(
        _expert_body,
        grid_spec=grid_spec,
        out_shape=jax.ShapeDtypeStruct((n_out_rows * (D // 2 // LANES), LANES), jnp.uint32),
        name="experts",
        compiler_params=pltpu.CompilerParams(dimension_semantics=("arbitrary", "arbitrary"),
                                             vmem_limit_bytes=VMEM_LIMIT, has_side_effects=True,
                                             disable_bounds_checks=True),
    )(chunk_e, chunk_nv, n_used, gidx, gidx, sidx, sidx, h2u, lp['w_exp1'], lp['w_exp1'], lp['b_exp1'],
      lp['b_exp1'], lp['w_exp2'], lp['b_exp2'])


def _routing(logits, n_tok):
    R = MOE_CHUNK
    top_v, top_i = lax.top_k(logits, TOP_K)
    gates = jax.nn.softmax(top_v, axis=-1)
    n_pairs = n_tok * TOP_K
    flat_e = top_i.reshape(n_pairs).astype(jnp.int32)
    experts = jnp.arange(N_EXPERTS, dtype=jnp.int32)
    counts = jnp.sum((flat_e[:, None] == experts[None]).astype(jnp.int32), axis=0)
    padded = (counts + R - 1) // R * R
    pad_end = jnp.cumsum(padded)
    pad_start = pad_end - padded
    nc = n_pairs // R + N_EXPERTS
    pair = jnp.arange(n_pairs, dtype=jnp.int32)
    fill_e = jnp.repeat(experts, R)
    fill_i = jnp.tile(jnp.arange(R, dtype=jnp.int32), N_EXPERTS)
    fill_key = jnp.where(fill_i < jnp.repeat(padded - counts, R), 2 * fill_e + 1, 2 * N_EXPERTS)
    _, pid = lax.sort((jnp.concatenate([2 * flat_e, fill_key]), jnp.concatenate([pair, n_pairs + fill_i])),
                      num_keys=1)
    real = pid < n_pairs
    gidx = jnp.where(real, pid // TOP_K, 0)
    sidx = jnp.where(real, (pid % TOP_K) * n_tok + pid // TOP_K, pid)
    n_used = (pad_end[-1] // R).astype(jnp.int32)
    cstart = jnp.arange(nc, dtype=jnp.int32) * R
    ce = jnp.minimum(jnp.searchsorted(pad_end, cstart, side='right'), N_EXPERTS - 1).astype(jnp.int32)
    ce = jnp.where(jnp.arange(nc) < n_used, ce, ce[jnp.maximum(n_used - 1, 0)])
    nv = jnp.clip(counts[ce] - (cstart - pad_start[ce]), 0, R).astype(jnp.int32)
    nv = jnp.where(jnp.arange(nc) < n_used, nv, 0)
    return gates, ce, nv, n_used.reshape(1), gidx.reshape(nc, 1, R), sidx.reshape(nc, 1, R)


def _combine_body(x1_ref, y0_ref, y1_ref, y2_ref, y3_ref, g_ref, gate2_ref, gf_ref, o_ref):
    g = g_ref[...]
    tm, D = x1_ref.shape
    per = D // 2 // LANES

    def rows(y_ref):
        parts = [_unpack_bf16_pair(y_ref[pl.ds(s, tm, stride=per), :]) for s in range(per)]
        return jnp.concatenate([p[0] for p in parts] + [p[1] for p in parts], axis=-1)

    ff = (g[:, 0:1] * rows(y0_ref) + g[:, 1:2] * rows(y1_ref)) + (g[:, 2:3] * rows(y2_ref) + g[:, 3:4] * rows(y3_ref))
    x2 = x1_ref[...] + gate2_ref[...] * ff
    o_ref[...] = _rms(x2, gf_ref[...])


def _combine(x1, yp, gates, mod3, row0, g_final, tok0, B, L):
    n_tok, D = x1.shape
    tm = min(ROW_TILE, L)
    nl = L // tm
    blk0 = tok0 // tm
    nblk_tok = n_tok // tm
    mrow = (lambda b: 0) if row0 == 0 else (lambda b: b + row0)
    row = lambda b, i: (blk0 + b * nl + i, 0)
    yspec = lambda k: pl.BlockSpec((tm * (D // 2 // LANES), LANES),
                                   lambda b, i: (k * nblk_tok + blk0 + b * nl + i, 0))
    return pl.pallas_call(
        _combine_body,
        grid=(B, nl),
        in_specs=[pl.BlockSpec((tm, D), row), yspec(0), yspec(1), yspec(2), yspec(3),
                  pl.BlockSpec((tm, TOP_K), row),
                  pl.BlockSpec((None, 1, D), lambda b, i: (mrow(b), 0, 5)),
                  pl.BlockSpec((1, D), lambda b, i: (0, 0))],
        out_specs=pl.BlockSpec((None, tm, D), lambda b, i: (b, i, 0)),
        out_shape=jax.ShapeDtypeStruct((B, L, D), F32),
        name="combine",
        compiler_params=_params("arbitrary", "arbitrary"),
    )(x1, yp, yp, yp, yp, gates, mod3, g_final)


def _filter_body(tw_ref, bands_ref, w1_ref, b1_ref, fr_ref, w2_ref, b2_ref, w3_ref, dl_ref, k_ref, ss_ref):
    hi = functools.partial(jnp.dot, precision=HIGHEST, preferred_element_type=F32)
    t = tw_ref[:, 0:1]
    w = tw_ref[:, 1:2]
    lane = lax.broadcasted_iota(jnp.int32, (t.shape[0], LANES), 1)
    fw = w * bands_ref[...]
    z = jnp.where(lane == 0, t,
                  jnp.where(lane <= FILTER_BANDS, jnp.cos(fw),
                            jnp.where(lane <= 2 * FILTER_BANDS, -jnp.sin(fw), 0.0)))
    fr = fr_ref[...]
    h = jnp.sin(fr * (hi(z, w1_ref[...]) + b1_ref[...]))
    h = jnp.sin(fr * (hi(h, w2_ref[...]) + b2_ref[...]))
    decay = jnp.exp(-t * dl_ref[...])
    C = decay.shape[1]
    ss = []
    for g in range(k_ref.shape[1] // C):
        kg = hi(h, w3_ref[:, g * C:(g + 1) * C]) * decay
        k_ref[:, g * C:(g + 1) * C] = kg
        ss.append(jnp.sum(kg * kg, axis=0, keepdims=True))
    ss = jnp.concatenate(ss, axis=-1)

    @pl.when(pl.program_id(0) == 0)
    def _first():
        ss_ref[...] = ss

    @pl.when(pl.program_id(0) > 0)
    def _rest():
        ss_ref[...] += ss


def _hyena_filters(L, lp):
    C = D_HYENA
    pos = jnp.arange(L, dtype=F32)
    tw = jnp.stack([pos / max(L - 1, 1), 2 * math.pi * pos / L], axis=-1)
    bands = jnp.linspace(1e-4, FILTER_BANDS - 1, FILTER_BANDS, dtype=F32)
    bands128 = jnp.zeros((1, LANES), F32).at[0, 1:1 + 2 * FILTER_BANDS].set(jnp.concatenate([bands, bands]))
    w1 = jnp.zeros((LANES, lp['f_w1'].shape[1]), F32).at[:lp['f_w1'].shape[0]].set(lp['f_w1'])
    deltas = jnp.abs(jnp.linspace(math.log(DECAY_TARGET) / DECAY_PCT_LONG,
                                  math.log(DECAY_TARGET) / DECAY_PCT_SHORT, C, dtype=F32))[None]
    nk = lp['f_w3'].shape[1]
    tl = min(L, 256)
    args = [tw, bands128, w1, lp['f_b1'], lp['f_freq'], lp['f_w2'], lp['f_b2'], lp['f_w3'], deltas]
    const = lambda a: pl.BlockSpec(a.shape, lambda i: (0,) * a.ndim)
    k_un, ss = pl.pallas_call(
        _filter_body,
        grid=(L // tl,),
        in_specs=[pl.BlockSpec((tl, 2), lambda i: (i, 0))] + [const(a) for a in args[1:]],
        out_specs=[pl.BlockSpec((tl, nk), lambda i: (i, 0)), pl.BlockSpec((1, nk), lambda i: (0, 0))],
        out_shape=[jax.ShapeDtypeStruct((L, nk), F32), jax.ShapeDtypeStruct((1, nk), F32)],
        name="hyena_filters",
        compiler_params=_params("arbitrary"),
    )(*args)
    ss = ss.reshape(HYENA_ORDER, 2, C)
    scale = lax.rsqrt(ss[:, 0] + ss[:, 1] + EPS).reshape(1, HYENA_ORDER * C)
    return k_un, scale


def _short_conv_chunk(u_ref, r0, n, prev_last, next_first, w, b):
    u = u_ref[pl.ds(r0, n), :]
    row = lax.broadcasted_iota(jnp.int32, u.shape, 0)
    up = jnp.where(row == 0, prev_last, pltpu.roll(u, 1, axis=0))
    un = jnp.where(row == n - 1, next_first, pltpu.roll(u, n - 1, axis=0))
    return up * w[0:1] + u * w[1:2] + un * w[2:3] + b


def _dft_spectrum_body(a_ref, b_ref, s_ref, o_ref):
    o_ref[...] = jnp.dot(a_ref[...], b_ref[...], precision=HIGHEST, preferred_element_type=F32) * s_ref[...]


def _dft_spectrum(a, b, scale):
    M, K = a.shape
    ncol = b.shape[1]
    tn = min(ncol, 512)
    return pl.pallas_call(
        _dft_spectrum_body,
        grid=(ncol // tn,),
        in_specs=[pl.BlockSpec((M, K), lambda j: (0, 0)), pl.BlockSpec((K, tn), lambda j: (0, j)),
                  pl.BlockSpec((1, tn), lambda j: (0, j))],
        out_specs=pl.BlockSpec((M, tn), lambda j: (0, j)),
        out_shape=jax.ShapeDtypeStruct((M, ncol), F32),
        name="dft_spectrum",
        compiler_params=_params("arbitrary"),
    )(a, b, scale)


def _hyena_short_body(v_ref, x1_ref, x2_ref, wv_ref, w1_ref, w2_ref, bv_ref, b1_ref, b2_ref, fb_ref, h0_ref, h1_ref,
                      ff_ref, if_ref, o_ref):
    L = v_ref.shape[0]
    N = ff_ref.shape[0] // 2
    conv = lambda u_ref, w_ref, b_ref: _short_conv_chunk(u_ref, 0, L, 0.0, 0.0, w_ref[...], b_ref[...])
    z = conv(v_ref, wv_ref, bv_ref)
    gates = (conv(x1_ref, w1_ref, b1_ref), conv(x2_ref, w2_ref, b2_ref))
    for o, (gate, h_ref) in enumerate(zip(gates, (h0_ref, h1_ref))):
        X = _dot(ff_ref[...], z.astype(BF16))
        xr, xi = X[:N], X[N:]
        hr, hi = h_ref[:N, :], h_ref[N:, :]
        Y = jnp.concatenate([xr * hr - xi * hi, xr * hi + xi * hr], axis=0)
        y = _dot(if_ref[...], Y.astype(BF16))
        z = gate * (y + fb_ref[o:o + 1, :] * z)
    o_ref[...] = z


def _hyena_short(hy, lp, k_un, scale):
    B, L, _ = hy.shape
    C = D_HYENA
    N = 2 * L
    ct = 256
    ncb = C // ct
    k4 = k_un.reshape(L, HYENA_ORDER, 2, C)
    taps = jnp.concatenate([k4[:, :, 0], jnp.zeros((1, HYENA_ORDER, C), F32), k4[:0:-1, :, 1]], axis=0)
    KF = (N // 2 + 1 + SUBLANES - 1) // SUBLANES * SUBLANES
    k = np.arange(KF)
    kept = (k <= N // 2)[:, None]
    weight = np.where((k == 0) | (k == N // 2), 1.0, 2.0)[:, None] * kept / N
    ang = 2 * np.pi * np.outer(k, np.arange(N)) / N
    dft = jnp.asarray(np.concatenate([np.cos(ang) * kept, -np.sin(ang) * kept], axis=0), F32)
    H = _dft_spectrum(dft, taps.reshape(N, HYENA_ORDER * C), scale)
    fwd = jnp.asarray(np.concatenate([np.cos(ang[:, :L]) * kept, -np.sin(ang[:, :L]) * kept], axis=0), BF16)
    inv = jnp.asarray(np.concatenate([np.cos(ang[:, :L]) * weight, -np.sin(ang[:, :L]) * weight], axis=0).T, BF16)
    u = lambda g: pl.BlockSpec((None, L, ct), lambda c, b: (b, 0, g * ncb + c))
    cw = lambda g: pl.BlockSpec((3, ct), lambda c, b: (0, g * ncb + c))
    cb = lambda g: pl.BlockSpec((1, ct), lambda c, b: (0, g * ncb + c))
    const = lambda a: pl.BlockSpec(a.shape, lambda c, b: (0,) * a.ndim)
    return pl.pallas_call(
        _hyena_short_body,
        grid=(ncb, B),
        in_specs=[u(0), u(1), u(2), cw(0), cw(1), cw(2), cb(0), cb(1), cb(2),
                  pl.BlockSpec((HYENA_ORDER, ct), lambda c, b: (0, c)),
                  pl.BlockSpec((2 * KF, ct), lambda c, b: (0, c)),
                  pl.BlockSpec((2 * KF, ct), lambda c, b: (0, ncb + c)),
                  const(fwd), const(inv)],
        out_specs=pl.BlockSpec((None, L, ct), lambda c, b: (b, 0, c)),
        out_shape=jax.ShapeDtypeStruct((B, L, C), F32),
        name="hyena_short",
        compiler_params=_params("arbitrary", "arbitrary"),
    )(hy, hy, hy, lp['conv_w'], lp['conv_w'], lp['conv_w'], lp['conv_b'], lp['conv_b'], lp['conv_b'],
      lp['f_bias'], H, H, fwd, inv)


FFT_N1 = 128
FFT_BATCH = 4
FFT_GROUPS = 2


def _fft_dims(L):
    N = 2 * L
    N1 = FFT_N1
    N2 = N // N1
    step = FFT_BATCH * FFT_GROUPS
    K1 = (N1 // 2 + 1 + step - 1) // step * step
    assert N1 * N2 == N and N2 % SUBLANES == 0 and K1 % SUBLANES == 0
    return N, N1, N2, N2 + SUBLANES, 2 * K1 + SUBLANES, K1


def _fft_tables(L):
    N, N1, N2, _, _, K1 = _fft_dims(L)
    NH = N1 // 2
    n1 = np.arange(NH)
    k1 = np.arange(K1)
    n2 = np.arange(N2)
    kept = (k1 <= N1 // 2)[None, :, None]
    weight = np.where((k1 == 0) | (k1 == N1 // 2), 1.0, 2.0)[None, :, None] * kept / N
    th = 2 * np.pi * (n1[None, None, :] * k1[None, :, None] / N1 + n2[:, None, None] * k1[None, :, None] / N)
    g = np.concatenate([np.cos(th) * kept, -np.sin(th) * kept], axis=1).reshape(N2 * 2 * K1, NH)
    ig = np.concatenate([np.cos(th) * weight, -np.sin(th) * weight], axis=1).transpose(0, 2, 1)
    ig = ig.reshape(N2 * NH, 2 * K1)
    ph = 2 * np.pi * np.outer(n2, n2) / N2
    c, s = np.cos(ph), np.sin(ph)
    f2 = np.block([[c, s], [-s, c]])
    if2 = np.block([[c, -s], [s, c]])

    return {name: jnp.asarray(a, BF16) for name, a in (('g', g), ('ig', ig), ('f2', f2), ('if2', if2))}


def _mm(tab_ref, r0, nrows, x):
    return _dot(tab_ref[pl.ds(r0, nrows), :], x.astype(BF16))


def _fft_stage1(tbuf, sbuf, g, dims):
    _, N1, N2, P, Q, K1 = dims

    nb = FFT_BATCH * FFT_GROUPS

    def body(i, carry):
        n2s = [i * nb + j for j in range(nb)]
        xs = [tbuf[pl.ds(n2, N1 // 2, stride=P), :] for n2 in n2s]
        outs = [_mm(g, pl.multiple_of(n2 * 2 * K1, 2 * SUBLANES), 2 * K1, x) for n2, x in zip(n2s, xs)]
        for n2, out in zip(n2s, outs):
            sbuf[pl.ds(pl.multiple_of(n2 * Q, SUBLANES), 2 * K1), :] = out
        return carry

    lax.fori_loop(0, N2 // nb, body, 0)


def _fft_stage2_load(sbuf, k1, dims):
    _, N1, N2, _, Q, K1 = dims
    re = [sbuf[pl.ds(k1 + j, N2, stride=Q), :] for j in range(FFT_BATCH)]
    im = [sbuf[pl.ds(K1 + k1 + j, N2, stride=Q), :] for j in range(FFT_BATCH)]
    return jnp.concatenate([jnp.concatenate(re, axis=1), jnp.concatenate(im, axis=1)], axis=0)


def _hyena_long_body(v_ref, x1_ref, x2_ref, wv_ref, w1_ref, w2_ref, bv_ref, b1_ref, b2_ref, fb_ref, h0_ref, h1_ref,
                     g_ref, ig_ref, f2_ref, if2_ref, o_ref, zbuf, g1buf, g2buf, sbuf, *, dims):
    N, N1, N2, P, Q, K1 = dims
    NH = N1 // 2
    zero = jnp.zeros((1, LANES), F32)

    for u_ref, w_ref, b_ref, buf in ((v_ref, wv_ref, bv_ref, zbuf), (x1_ref, w1_ref, b1_ref, g1buf),
                                     (x2_ref, w2_ref, b2_ref, g2buf)):
        w, b = w_ref[...], b_ref[...]
        for n1 in range(NH):
            r0 = n1 * N2
            prev_last = zero if n1 == 0 else u_ref[r0 - 1:r0, :]
            next_first = zero if n1 == NH - 1 else u_ref[r0 + N2:r0 + N2 + 1, :]
            buf[n1 * P:n1 * P + N2, :] = _short_conv_chunk(u_ref, r0, N2, prev_last, next_first, w, b)

    for o, (gbuf, h_ref) in enumerate(((g1buf, h0_ref), (g2buf, h1_ref))):
        _fft_stage1(zbuf, sbuf, g_ref, dims)

        def per_k1(i, carry):
            k1s = [(i * FFT_GROUPS + g) * FFT_BATCH for g in range(FFT_GROUPS)]
            Bs = [_fft_stage2_load(sbuf, k1, dims).astype(BF16) for k1 in k1s]
            Ccs = []
            for k1, B in zip(k1s, Bs):
                X = _dot(f2_ref[...], B)
                h0 = [pl.multiple_of((k1 + j) * 2 * N2, 2 * N2) for j in range(FFT_BATCH)]
                hr = jnp.concatenate([h_ref[pl.ds(r, N2), :] for r in h0], axis=1)
                hi = jnp.concatenate([h_ref[pl.ds(r + N2, N2), :] for r in h0], axis=1)
                xr, xi = X[:N2], X[N2:]
                Y = jnp.concatenate([xr * hr - xi * hi, xr * hi + xi * hr], axis=0)
                Ccs.append(_dot(if2_ref[...], Y.astype(BF16)))
            for k1, Cc in zip(k1s, Ccs):
                for j in range(FFT_BATCH):
                    lanes = slice(j * LANES, (j + 1) * LANES)
                    sbuf[pl.ds(k1 + j, N2, stride=Q), :] = Cc[:N2, lanes]
                    sbuf[pl.ds(K1 + k1 + j, N2, stride=Q), :] = Cc[N2:, lanes]
            return carry

        lax.fori_loop(0, K1 // (FFT_BATCH * FFT_GROUPS), per_k1, 0)
        fb = fb_ref[o:o + 1, :]

        nb = FFT_BATCH * FFT_GROUPS

        def per_n2(i, carry):
            n2s = [i * nb + j for j in range(nb)]
            Ds = [sbuf[pl.ds(pl.multiple_of(n2 * Q, SUBLANES), 2 * K1), :].astype(BF16) for n2 in n2s]
            ys = [_dot(ig_ref[pl.ds(pl.multiple_of(n2 * NH, NH), NH), :], D) for n2, D in zip(n2s, Ds)]
            for n2, y in zip(n2s, ys):
                rows = pl.ds(n2, NH, stride=P)
                zbuf[rows, :] = gbuf[rows, :] * (y + fb * zbuf[rows, :])
            return carry

        lax.fori_loop(0, N2 // nb, per_n2, 0)

    for n1 in range(NH):
        o_ref[n1 * N2:(n1 + 1) * N2, :] = zbuf[n1 * P:n1 * P + N2, :]


def _fft_spectrum_body(hf_ref, hb_ref, s_ref, g_ref, f2_ref, o_ref, tbuf, sbuf, *, dims):
    N, N1, N2, P, Q, K1 = dims
    NH = N1 // 2
    scale = s_ref[...]
    for d, h_ref in enumerate((hf_ref, hb_ref)):
        for n1 in range(NH):
            h = h_ref[n1 * N2:(n1 + 1) * N2, :] * scale
            if d == 1 and n1 == 0:
                h = jnp.where(lax.broadcasted_iota(jnp.int32, h.shape, 0) == 0, 0.0, h)
            tbuf[n1 * P:n1 * P + N2, :] = h
        _fft_stage1(tbuf, sbuf, g_ref, dims)

        def per_k1(i, carry):
            k1 = i * FFT_BATCH
            X = _mm(f2_ref, 0, 2 * N2, _fft_stage2_load(sbuf, k1, dims))
            for j in range(FFT_BATCH):
                h0 = pl.multiple_of((k1 + j) * 2 * N2, 2 * N2)
                Xj = X[:, j * LANES:(j + 1) * LANES]
                if d == 0:
                    o_ref[pl.ds(h0, 2 * N2), :] = Xj
                else:
                    o_ref[pl.ds(h0, N2), :] += Xj[:N2]
                    o_ref[pl.ds(h0 + N2, N2), :] -= Xj[N2:]
            return carry

        lax.fori_loop(0, K1 // FFT_BATCH, per_k1, 0)


def _hyena_long(hy, lp, k_un, scale):
    B, L, _ = hy.shape
    C = D_HYENA
    dims = _fft_dims(L)
    N, N1, N2, P, Q, K1 = dims
    NH = N1 // 2
    tabs = _fft_tables(L)
    ncb = C // LANES
    nspec = HYENA_ORDER * ncb
    const1 = lambda a: pl.BlockSpec(a.shape, (lambda *i: (0,) * a.ndim), pipeline_mode=pl.Buffered(1))
    hcol = lambda d: pl.BlockSpec((L, LANES), lambda j: (0, (j // ncb) * 2 * ncb + d * ncb + j % ncb))
    H = pl.pallas_call(
        functools.partial(_fft_spectrum_body, dims=dims),
        grid=(nspec,),
        in_specs=[hcol(0), hcol(1), pl.BlockSpec((1, LANES), lambda j: (0, j)),
                  const1(tabs['g']), const1(tabs['f2'])],
        out_specs=pl.BlockSpec((K1 * 2 * N2, LANES), lambda j: (0, j)),
        out_shape=jax.ShapeDtypeStruct((K1 * 2 * N2, HYENA_ORDER * C), F32),
        scratch_shapes=[pltpu.VMEM((NH * P, LANES), F32), pltpu.VMEM((N2 * Q, LANES), F32)],
        name="filter_spectrum",
        compiler_params=_params("arbitrary"),
    )(k_un, k_un, scale, tabs['g'], tabs['f2'])

    one = pl.Buffered(1)
    hspec = lambda o: pl.BlockSpec((K1 * 2 * N2, LANES), lambda c, b: (0, o * ncb + c), pipeline_mode=one)
    u = lambda g: pl.BlockSpec((None, L, LANES), lambda c, b: (b, 0, g * ncb + c), pipeline_mode=one)
    cw = lambda g: pl.BlockSpec((3, LANES), lambda c, b: (0, g * ncb + c))
    cb = lambda g: pl.BlockSpec((1, LANES), lambda c, b: (0, g * ncb + c))
    return pl.pallas_call(
        functools.partial(_hyena_long_body, dims=dims),
        grid=(ncb, B),
        in_specs=[u(0), u(1), u(2), cw(0), cw(1), cw(2), cb(0), cb(1), cb(2),
                  pl.BlockSpec((HYENA_ORDER, LANES), lambda c, b: (0, c)),
                  hspec(0), hspec(1),
                  const1(tabs['g']), const1(tabs['ig']), const1(tabs['f2']), const1(tabs['if2'])],
        out_specs=pl.BlockSpec((None, L, LANES), lambda c, b: (b, 0, c)),
        out_shape=jax.ShapeDtypeStruct((B, L, C), F32),
        scratch_shapes=[pltpu.VMEM((NH * P, LANES), F32), pltpu.VMEM((NH * P, LANES), F32),
                        pltpu.VMEM((NH * P, LANES), F32), pltpu.VMEM((N2 * Q, LANES), F32)],
        name="hyena_long",
        compiler_params=_params("arbitrary", "arbitrary"),
    )(hy, hy, hy, lp['conv_w'], lp['conv_w'], lp['conv_w'], lp['conv_b'], lp['conv_b'], lp['conv_b'],
      lp['f_bias'], H, H, tabs['g'], tabs['ig'], tabs['f2'], tabs['if2'])


def _hyena(hy, lp):
    L = hy.shape[1]
    k_un, scale = _hyena_filters(L, lp)
    if 2 * L >= 2 * FFT_N1 * SUBLANES and (2 * L) % (FFT_N1 * SUBLANES) == 0:
        return _hyena_long(hy, lp, k_un, scale)
    return _hyena_short(hy, lp, k_un, scale)


def kernel(x_prompt, x_sample, cache_k, cache_v, c, c_ctx, w_mod, b_mod, g_mix, w_in, conv_w, conv_b, f_w1, f_b1,
           f_freq, f_w2, f_b2, f_w3, f_bias, rpb, g_out_hy, g_out_at, w_out, g_ffn, w_router, b_router, w_exp1,
           b_exp1, w_exp2, b_exp2, g_final):
    depth = w_mod.shape[0]
    Bp, Lp, D = x_prompt.shape
    Bs, Ls, _ = x_sample.shape
    n_ctx, n_lat = Bp * Lp, Bs * Ls
    n_tok = n_ctx + n_lat
    d_attn = N_HEADS * HEAD_DIM
    q_off = 3 * D_HYENA
    row2 = lambda a: a.reshape(1, -1)

    cond = jnp.zeros((SUBLANES, D), F32).at[0].set(c_ctx).at[1:1 + Bs].set(c)
    xp, xs = x_prompt, x_sample
    new_k, new_v = [], []
    for l in range(depth):
        wr = jnp.zeros((D, LANES), F32).at[:, :N_EXPERTS].set(w_router[l])
        wr_hi = wr.astype(BF16)
        lp = {
            'conv_w': conv_w[l], 'conv_b': row2(conv_b[l]), 'f_w1': f_w1[l], 'f_b1': row2(f_b1[l]),
            'f_freq': row2(f_freq[l]), 'f_w2': f_w2[l], 'f_b2': row2(f_b2[l]), 'f_w3': f_w3[l],
            'f_bias': f_bias[l], 'g_out_hy': row2(g_out_hy[l]), 'g_out_at': row2(g_out_at[l]),
            'w_out': w_out[l].astype(BF16), 'g_ffn': row2(g_ffn[l]),
            'wr_hi': wr_hi, 'wr_lo': (wr - wr_hi.astype(F32)).astype(BF16),
            'b_router': jnp.zeros((1, LANES), F32).at[0, :N_EXPERTS].set(b_router[l]),
            'w_exp1': w_exp1[l], 'b_exp1': b_exp1[l][:, None, :], 'w_exp2': w_exp2[l],
            'b_exp2': b_exp2[l][:, None, :],
        }
        mod = _modulation(cond, w_mod[l], row2(b_mod[l]))
        mod3 = mod.reshape(SUBLANES, 1, 6 * D)
        w_in_b = w_in[l].astype(BF16)
        g_mix_l = row2(g_mix[l])

        proj_p = _in_projection(xp, mod3, 0, g_mix_l, w_in_b)
        proj_s = _in_projection(xs, mod3, 1, g_mix_l, w_in_b)
        kv = proj_p[..., q_off + d_attn:].reshape(Bp, Lp, 2, N_HEADS, HEAD_DIM).transpose(2, 0, 3, 1, 4)
        new_k.append(kv[0])
        new_v.append(kv[1])

        hy_p = _hyena(proj_p, lp)
        hy_s = _hyena(proj_s, lp)
        at_p = _context_attention(proj_p, q_off)
        heads_last = lambda t: t.transpose(0, 2, 1, 3).reshape(Bs, t.shape[2], d_attn)
        at_s = _neighbourhood_attention(proj_s, q_off, heads_last(cache_k[:, l]), heads_last(cache_v[:, l]), rpb[l])

        x1, h2u, logits = _out_projection((hy_p, at_p, xp), (hy_s, at_s, xs), mod3, lp)

        gates, ce, nv, n_used, gidx, sidx = _routing(logits[:, :N_EXPERTS], n_tok)
        yp = _experts(ce, nv, n_used, gidx, sidx, h2u, lp, n_tok * TOP_K + MOE_CHUNK)
        last = l == depth - 1
        gf = row2(g_final) if last else None
        assert last, "deeper stacks need the un-normalised residual between layers"
        xp = _combine(x1, yp, gates, mod3, 0, gf, 0, Bp, Lp)
        xs = _combine(x1, yp, gates, mod3, 1, gf, n_ctx, Bs, Ls)

    return xp, xs, jnp.stack(new_k, axis=1), jnp.stack(new_v, axis=1)
```
